```python
import math
import jax, jax.numpy as jnp
from jax import lax
import numpy as np

D_MODEL = 1024
BATCH = 4
SEQ = 4096
DEPTH = 2

GRID_W = 64
CTX_LEN = 256
Q_BLOCK = 128
ROPE_BASE = 10000.0
NORM_EPS = 1e-6
SUBLN_EPS = 1e-5

A_HEADS = 4
A_DIM = 64
A_WIDTH = A_HEADS * 2 * A_DIM
A_COLS = 3 * A_WIDTH
B_HEADS = 4
B_NOPE = 64
B_ROPE = 32
B_VDIM = 64
B_Q_RANK = 192
B_KV_RANK = 128
B_WIDTH = B_HEADS * B_VDIM
B_COLS = B_Q_RANK + B_KV_RANK + B_ROPE
C_HEADS = 4
C_DIM = 64
C_WIDTH = C_HEADS * C_DIM
C_DECAY_LORA = 64
C_ICLR_LORA = 64
C_GATE_LORA = 128
C_GN_EPS = 64e-5
C_COLS = 3 * C_WIDTH + 2 * C_DECAY_LORA + 2 * C_ICLR_LORA + C_GATE_LORA

D_MIX = A_WIDTH + B_WIDTH + C_WIDTH
N_IN = A_COLS + B_COLS + C_COLS
IN_SPLITS = [A_COLS, A_COLS + B_COLS]
C_SPLITS = [int(s) for s in np.cumsum([C_WIDTH, C_WIDTH, C_WIDTH, C_DECAY_LORA, C_DECAY_LORA, C_ICLR_LORA, C_ICLR_LORA])]

D_FF = 256 * ((8 * D_MODEL // 3 + 255) // 256)

kernel_name = "hybrid_dit_diffattn_mla_rwkv7_block"

F32 = jnp.float32


def rmsnorm(x, gain, eps=NORM_EPS):
    x32 = x.astype(F32)
    y = x32 * lax.rsqrt(jnp.mean(x32 * x32, axis=-1, keepdims=True) + eps)
    return (y * gain.astype(F32)).astype(x.dtype)


def shift_prev(x):
    return jnp.pad(x, ((0, 0), (1, 0), (0, 0)))[:, :-1]


def shift_next(x):
    return jnp.pad(x, ((0, 0), (0, 1), (0, 0)))[:, 1:]


def dwconv3(x, w, b):
    return shift_prev(x) * w[0] + x * w[1] + shift_next(x) * w[2] + b


def rope_angles(rows, cols, dim):
    nf = dim // 4
    inv = ROPE_BASE ** (-jnp.arange(nf, dtype=F32) / nf)
    return jnp.concatenate([rows.astype(F32)[:, None] * inv, cols.astype(F32)[:, None] * inv], axis=-1)


def axial_rope(x, ang):
    T, d = x.shape[1], x.shape[-1]
    nf = d // 4
    xs = x.astype(F32).reshape(*x.shape[:-1], 2, 2, nf)
    x1, x2 = xs[..., 0, :], xs[..., 1, :]
    a = ang.reshape(T, 1, 2, nf)
    cos, sin = jnp.cos(a), jnp.sin(a)
    out = jnp.stack([x1 * cos - x2 * sin, x2 * cos + x1 * sin], axis=-2)
    return out.reshape(x.shape).astype(x.dtype)


def over_query_blocks(fn, *qs):
    Bt, T = qs[0].shape[:2]
    nb = T // Q_BLOCK
    split = lambda a: jnp.moveaxis(a.reshape(Bt, nb, Q_BLOCK, *a.shape[2:]), 1, 0)
    out = lax.map(lambda qb: fn(*qb), tuple(split(q) for q in qs))
    return jnp.moveaxis(out, 0, 1).reshape(Bt, T, *out.shape[3:])


def softmax32(s):
    return jax.nn.softmax(s.astype(F32), axis=-1)


def diff_qkv(pa, ang):
    Bt, T = pa.shape[:2]
    q, k, v = jnp.split(pa, 3, axis=-1)
    q = q.reshape(Bt, T, 2 * A_HEADS, A_DIM)
    k = k.reshape(Bt, T, 2 * A_HEADS, A_DIM)
    if ang is not None:
        q = axial_rope(q, ang)
        k = axial_rope(k, ang)
    q = q.reshape(Bt, T, A_HEADS, 2, A_DIM)
    k = k.reshape(Bt, T, A_HEADS, 2, A_DIM)
    v = v.reshape(Bt, T, A_HEADS, 2 * A_DIM)
    return q[..., 0, :], q[..., 1, :], k[..., 0, :], k[..., 1, :], v


def diff_attn(q1, q2, k1, k2, v, lam):
    scale = A_DIM ** -0.5
    p1 = softmax32(jnp.einsum('bqhd,bkhd->bhqk', q1, k1) * scale)
    p2 = softmax32(jnp.einsum('bqhd,bkhd->bhqk', q2, k2) * scale)
    a = (p1 - lam * p2).astype(v.dtype)
    return jnp.einsum('bhqk,bkhe->bqhe', a, v)


def mla_qkv(pb, P, ang):
    Bt, T = pb.shape[:2]
    cq, ckv, kr = jnp.split(pb, [B_Q_RANK, B_Q_RANK + B_KV_RANK], axis=-1)
    q = (rmsnorm(cq, P['b_q_norm_g']) @ P['b_w_q_up']).reshape(Bt, T, B_HEADS, B_NOPE + B_ROPE)
    kv = (rmsnorm(ckv, P['b_kv_norm_g']) @ P['b_w_kv_up']).reshape(Bt, T, B_HEADS, B_NOPE + B_VDIM)
    q_nope, q_rope = q[..., :B_NOPE], q[..., B_NOPE:]
    k_nope, v = kv[..., :B_NOPE], kv[..., B_NOPE:]
    kr = kr[:, :, None, :]
    if ang is not None:
        q_rope = axial_rope(q_rope, ang)
        kr = axial_rope(kr, ang)
    k = jnp.concatenate([k_nope, jnp.broadcast_to(kr, (Bt, T, B_HEADS, B_ROPE))], axis=-1)
    q = jnp.concatenate([q_nope, q_rope], axis=-1)
    return q, k, v


def softmax_attn(q, k, v):
    scale = (B_NOPE + B_ROPE) ** -0.5
    p = softmax32(jnp.einsum('bqhd,bkhd->bhqk', q, k) * scale).astype(v.dtype)
    return jnp.einsum('bhqk,bkhe->bqhe', p, v)


def rwkv7_scan(S0, r, w, k, v, a, b, reverse):
    def step(S, inp):
        r_t, w_t, k_t, v_t, a_t, b_t = inp
        sa = jnp.einsum('bhvk,bhk->bhv', S, a_t)
        S = S * w_t[:, :, None, :] + sa[..., None] * b_t[:, :, None, :] + v_t[..., None] * k_t[:, :, None, :]
        return S, jnp.einsum('bhvk,bhk->bhv', S, r_t)
    xs = tuple(jnp.moveaxis(t.astype(F32), 1, 0) for t in (r, w, k, v, a, b))
    S, ys = lax.scan(step, S0, xs, reverse=reverse)
    return S, jnp.moveaxis(ys, 0, 1)


def rwkv7_mix(pc, S0_f, S0_b, P, want_out):
    Bt, T, _ = pc.shape
    pc = pc + P['c_mu_prev'] * (shift_prev(pc) - pc) + P['c_mu_next'] * (shift_next(pc) - pc)
    r, k, v, wl_f, wl_b, al_f, al_b, gl = jnp.split(pc, C_SPLITS, axis=-1)
    heads = lambda t: t.reshape(Bt, T, C_HEADS, C_DIM)
    kk = heads(k * P['c_k_k']).astype(F32)
    kk = kk / jnp.maximum(jnp.linalg.norm(kk, axis=-1, keepdims=True), 1e-12)
    r_h, v_h, k_h = heads(r), heads(v), heads(k)
    k_a = P['c_k_a'].reshape(C_HEADS, C_DIM)
    states, ys, bonuses = [], [], []
    for d, (wl, al, S0, rev) in enumerate(((wl_f, al_f, S0_f, False), (wl_b, al_b, S0_b, True))):
        w = -jax.nn.softplus(-(P['c_w0'][d] + jnp.tanh(wl) @ P['c_w2'][d])) - 0.5
        decay = jnp.exp(-jnp.exp(w.astype(F32)))
        a = heads(jax.nn.sigmoid(P['c_a0'][d] + al @ P['c_a2'][d]))
        kd = k_h * (1 + (a - 1) * k_a)
        S, y = rwkv7_scan(S0, r_h, heads(decay), kd, v_h, -kk, kk * a, rev)
        states.append(S)
        ys.append(y)
        bonuses.append(jnp.sum(r_h * kd * P['c_r_k'], axis=-1, keepdims=True) * v_h)
    if not want_out:
        return None, states[0], states[1]
    y = ys[0] + ys[1]
    mu = jnp.mean(y, axis=-1, keepdims=True)
    var = jnp.mean(jnp.square(y - mu), axis=-1, keepdims=True)
    yn = ((y - mu) * lax.rsqrt(var + C_GN_EPS)).reshape(Bt, T, C_WIDTH) * P['c_gn_g'] + P['c_gn_b']
    yn = yn + (bonuses[0] + bonuses[1]).reshape(Bt, T, C_WIDTH)
    out = yn * (jax.nn.sigmoid(gl) @ P['c_g2'])
    return out.astype(pc.dtype), states[0], states[1]


def merge_groups(oa, ob, oc, P, lam_init):
    Bt, T = oa.shape[:2]
    oa = (rmsnorm(oa, P['a_subln_g'], SUBLN_EPS) * (1 - lam_init)).reshape(Bt, T, A_WIDTH)
    ob = ob.reshape(Bt, T, B_WIDTH)
    return jnp.concatenate([oa, ob, oc], axis=-1) @ P['w_out']


def token_mix(h, hc, P, lam_init, ang_a, ang_b, ctx_out):
    p = h @ P['w_in']
    pc = hc @ P['w_in']
    pa, pb, pr = jnp.split(p, IN_SPLITS, axis=-1)
    pca, pcb, pcr = jnp.split(pc, IN_SPLITS, axis=-1)
    q1, q2, k1, k2, v = diff_qkv(pa, ang_a)
    cq1, cq2, ck1, ck2, cv = diff_qkv(pca, None)
    lam = (jnp.exp(jnp.sum(P['lam_q1'] * P['lam_k1']).astype(F32))
           - jnp.exp(jnp.sum(P['lam_q2'] * P['lam_k2']).astype(F32)) + lam_init)
    K1 = jnp.concatenate([ck1, k1], axis=1)
    K2 = jnp.concatenate([ck2, k2], axis=1)
    V = jnp.concatenate([cv, v], axis=1)
    oa = over_query_blocks(lambda a_, b_: diff_attn(a_, b_, K1, K2, V, lam), q1, q2)
    qb, kb, vb = mla_qkv(pb, P, ang_b)
    cqb, ckb, cvb = mla_qkv(pcb, P, None)
    Kb = jnp.concatenate([ckb, kb], axis=1)
    Vb = jnp.concatenate([cvb, vb], axis=1)
    ob = over_query_blocks(lambda q_: softmax_attn(q_, Kb, Vb), qb)
    zero = jnp.zeros((h.shape[0], C_HEADS, C_DIM, C_DIM), F32)
    ocr, S_f, S_b = rwkv7_mix(pcr, zero, zero, P, ctx_out)
    orr, _, _ = rwkv7_mix(pr, S_f, S_b, P, True)
    o = merge_groups(oa, ob, orr, P, lam_init)
    if not ctx_out:
        return o, None
    oc = merge_groups(diff_attn(cq1, cq2, ck1, ck2, cv, lam), softmax_attn(cqb, ckb, cvb), ocr, P, lam_init)
    return o, oc


def conv_ffn(h, P):
    u = dwconv3(h @ P['ffn_w_up'], P['ffn_conv_w'], P['ffn_conv_b'])
    g, v = jnp.split(u, 2, axis=-1)
    return (jax.nn.silu(g) * v) @ P['ffn_w_down']


def trunk_layer(x, xc, c_act, cc_act, P, lam_init, ang_a, ang_b, ctx_out):
    mod = (c_act @ P['ada_w'] + P['ada_b'])[:, None, :]
    modc = (cc_act @ P['ada_w'] + P['ada_b'])[None, None, :]
    sh1, sc1, gt1, sh2, sc2, gt2 = jnp.split(mod, 6, axis=-1)
    csh1, csc1, cgt1, csh2, csc2, cgt2 = jnp.split(modc, 6, axis=-1)
    h = rmsnorm(x, P['mix_pre_g']) * (1 + sc1) + sh1
    hc = rmsnorm(xc, P['mix_pre_g']) * (1 + csc1) + csh1
    o, oc = token_mix(h, hc, P, lam_init, ang_a, ang_b, ctx_out)
    x = x + gt1 * rmsnorm(o, P['mix_post_g'])
    h = rmsnorm(x, P['ffn_pre_g']) * (1 + sc2) + sh2
    x = x + gt2 * rmsnorm(conv_ffn(h, P), P['ffn_post_g'])
    if ctx_out:
        xc = xc + cgt1 * rmsnorm(oc, P['mix_post_g'])
        hc = rmsnorm(xc, P['ffn_pre_g']) * (1 + csc2) + csh2
        xc = xc + cgt2 * rmsnorm(conv_ffn(hc, P), P['ffn_post_g'])
    return x, xc


def setup_inputs(seed: int = 0) -> dict:
    key = jax.random.key(seed)
    ks = iter(jax.random.split(key, 48))
    nrm = lambda shape, s: jax.random.normal(next(ks), shape, F32) * s
    uni = lambda shape, lo, hi: jax.random.uniform(next(ks), shape, F32, lo, hi)
    L = DEPTH
    return {
        "x": nrm((BATCH, SEQ, D_MODEL), 1.0),
        "c": nrm((BATCH, D_MODEL), 1.0),
        "ctx": nrm((BATCH, CTX_LEN, D_MODEL), 1.0),
        "c_ctx": nrm((D_MODEL,), 1.0),
        "ada_w": nrm((L, D_MODEL, 6 * D_MODEL), 0.5 * D_MODEL ** -0.5),
        "ada_b": nrm((L, 6 * D_MODEL), 0.01),
        "mix_pre_g": 1.0 + nrm((L, D_MODEL), 0.02),
        "mix_post_g": 1.0 + nrm((L, D_MODEL), 0.02),
        "ffn_pre_g": 1.0 + nrm((L, D_MODEL), 0.02),
        "ffn_post_g": 1.0 + nrm((L, D_MODEL), 0.02),
        "w_in": nrm((L, D_MODEL, N_IN), D_MODEL ** -0.5),
        "w_out": nrm((L, D_MIX, D_MODEL), D_MIX ** -0.5),
        "lam_q1": nrm((L, A_DIM), 0.1),
        "lam_k1": nrm((L, A_DIM), 0.1),
        "lam_q2": nrm((L, A_DIM), 0.1),
        "lam_k2": nrm((L, A_DIM), 0.1),
        "a_subln_g": 1.0 + nrm((L, 2 * A_DIM), 0.02),
        "b_q_norm_g": 1.0 + nrm((L, B_Q_RANK), 0.02),
        "b_w_q_up": nrm((L, B_Q_RANK, B_HEADS * (B_NOPE + B_ROPE)), B_Q_RANK ** -0.5),
        "b_kv_norm_g": 1.0 + nrm((L, B_KV_RANK), 0.02),
        "b_w_kv_up": nrm((L, B_KV_RANK, B_HEADS * (B_NOPE + B_VDIM)), B_KV_RANK ** -0.5),
        "c_mu_prev": uni((L, C_COLS), 0.0, 0.5),
        "c_mu_next": uni((L, C_COLS), 0.0, 0.5),
        "c_w0": uni((L, 2, C_WIDTH), -5.0, -1.0),
        "c_w2": nrm((L, 2, C_DECAY_LORA, C_WIDTH), 0.5 * C_DECAY_LORA ** -0.5),
        "c_a0": nrm((L, 2, C_WIDTH), 0.5),
        "c_a2": nrm((L, 2, C_ICLR_LORA, C_WIDTH), 0.5 * C_ICLR_LORA ** -0.5),
        "c_g2": nrm((L, C_GATE_LORA, C_WIDTH), C_GATE_LORA ** -0.5),
        "c_k_k": 0.85 + nrm((L, C_WIDTH), 0.05),
        "c_k_a": 1.0 + nrm((L, C_WIDTH), 0.05),
        "c_r_k": nrm((L, C_HEADS, C_DIM), 0.1),
        "c_gn_g": 1.0 + nrm((L, C_WIDTH), 0.02),
        "c_gn_b": nrm((L, C_WIDTH), 0.01),
        "ffn_w_up": nrm((L, D_MODEL, 2 * D_FF), D_MODEL ** -0.5),
        "ffn_conv_w": nrm((L, 3, 2 * D_FF), 0.5),
        "ffn_conv_b": nrm((L, 2 * D_FF), 0.01),
        "ffn_w_down": nrm((L, D_FF, D_MODEL), D_FF ** -0.5),
    }


def reference(x, c, ctx, c_ctx, ada_w, ada_b, mix_pre_g, mix_post_g, ffn_pre_g, ffn_post_g,
              w_in, w_out, lam_q1, lam_k1, lam_q2, lam_k2, a_subln_g,
              b_q_norm_g, b_w_q_up, b_kv_norm_g, b_w_kv_up,
              c_mu_prev, c_mu_next, c_w0, c_w2, c_a0, c_a2, c_g2, c_k_k, c_k_a, c_r_k, c_gn_g, c_gn_b,
              ffn_w_up, ffn_conv_w, ffn_conv_b, ffn_w_down):
    T = x.shape[1]
    ROWS = T // GRID_W
    rows = jnp.repeat(jnp.arange(ROWS, dtype=jnp.int32), GRID_W)
    cols = jnp.tile(jnp.arange(GRID_W, dtype=jnp.int32), ROWS)
    ang_a = rope_angles(rows, cols, A_DIM)
    ang_b = rope_angles(rows, cols, B_ROPE)
    c_act = jax.nn.silu(c)
    cc_act = jax.nn.silu(c_ctx)
    stacked = dict(
        ada_w=ada_w, ada_b=ada_b, mix_pre_g=mix_pre_g, mix_post_g=mix_post_g,
        ffn_pre_g=ffn_pre_g, ffn_post_g=ffn_post_g, w_in=w_in, w_out=w_out,
        lam_q1=lam_q1, lam_k1=lam_k1, lam_q2=lam_q2, lam_k2=lam_k2, a_subln_g=a_subln_g,
        b_q_norm_g=b_q_norm_g, b_w_q_up=b_w_q_up, b_kv_norm_g=b_kv_norm_g, b_w_kv_up=b_w_kv_up,
        c_mu_prev=c_mu_prev, c_mu_next=c_mu_next, c_w0=c_w0, c_w2=c_w2, c_a0=c_a0, c_a2=c_a2,
        c_g2=c_g2, c_k_k=c_k_k, c_k_a=c_k_a, c_r_k=c_r_k, c_gn_g=c_gn_g, c_gn_b=c_gn_b,
        ffn_w_up=ffn_w_up, ffn_conv_w=ffn_conv_w, ffn_conv_b=ffn_conv_b, ffn_w_down=ffn_w_down)
    xc = ctx
    for i in range(DEPTH):
        P = {name: w[i] for name, w in stacked.items()}
        lam_init = 0.8 - 0.6 * math.exp(-0.3 * i)
        x, xc = trunk_layer(x, xc, c_act, cc_act, P, lam_init, ang_a, ang_b, i < DEPTH - 1)
    return x
```

```python
import functools
import math

import jax
import jax.numpy as jnp
import numpy as np
from jax import lax
from jax.experimental import pallas as pl
from jax.experimental.pallas import tpu as pltpu

F32 = jnp.float32
BF16 = jnp.bfloat16

GRID_W = 64
ROPE_BASE = 10000.0
NORM_EPS = 1e-6
SUBLN_EPS = 1e-5
A_HEADS, A_DIM = 4, 64
A_WIDTH = A_HEADS * 2 * A_DIM
B_HEADS, B_NOPE, B_ROPE, B_VDIM = 4, 64, 32, 64
B_Q_RANK, B_KV_RANK = 192, 128
B_WIDTH = B_HEADS * B_VDIM
C_HEADS, C_DIM = 4, 64
C_WIDTH = C_HEADS * C_DIM
C_LORA = 64
C_GATE_LORA = 128
C_GN_EPS = 64e-5
C_COLS = 3 * C_WIDTH + 4 * C_LORA + C_GATE_LORA

LANE = 128
SUBLANE = 8
ROW_TILE = 256
KV_TILE = 256
CHUNK = 64
VMEM_LIMIT = 56 * 1024 * 1024

P_AQ, P_AK, P_AV = 0, 512, 1024
P_BQ = 1536
P_BKV = 1792
P_C = 1920
P_TOTAL = 3072


def _bf(x):
    return x.astype(BF16)


def _dot(a, b, ca=1, cb=0):
    return lax.dot_general(a, b, (((ca,), (cb,)), ((), ())), preferred_element_type=F32)


def _mm(a, b):
    return _dot(_bf(a), _bf(b))


def _mm_nt(a, b):
    return _dot(_bf(a), _bf(b), 1, 1)


def _mm_tn(a, b):
    return _dot(_bf(a.T), _bf(b))


def _split2(x):
    hi = _bf(x)
    lo = _bf(x - hi.astype(F32))
    return hi, lo


def _mm_hl(a, b):
    hi, lo = _split2(a)
    return _dot(hi, b) + _dot(lo, b)


def _mm3(a, b):
    ah, al = _split2(a)
    bh, bl = _split2(b)
    return _dot(ah, bh) + (_dot(ah, bl) + _dot(al, bh))


def _rms(x, eps):
    return x * lax.rsqrt(jnp.mean(x * x, axis=-1, keepdims=True) + eps)


def _sigmoid(x):
    return 1.0 / (1.0 + jnp.exp(-x))


def _silu(x):
    return x * _sigmoid(x)


def _softplus(x):
    return jnp.maximum(x, 0.0) + jnp.log(1.0 + jnp.exp(-jnp.abs(x)))


def _lane_iota(shape):
    return lax.broadcasted_iota(jnp.int32, shape, len(shape) - 1)


def _row_iota(shape):
    return lax.broadcasted_iota(jnp.int32, shape, len(shape) - 2)


def _rope(x, cos, sin, half):
    n = x.shape[-1]
    up = pltpu.roll(x, n - half, axis=1)
    dn = pltpu.roll(x, half, axis=1)
    first = (_lane_iota(x.shape) & half) == 0
    return x * cos + jnp.where(first, up, dn) * sin


def _params(sem):
    return pltpu.CompilerParams(dimension_semantics=sem, vmem_limit_bytes=VMEM_LIMIT)


def _full(shape):
    nd = len(shape)
    return pl.BlockSpec(shape, lambda *_: (0,) * nd)


def _mod_kernel(c_ref, w_ref, b_ref, o_ref):
    act = _silu(c_ref[...])
    o_ref[0] = _mm3(act, w_ref[0]) + b_ref[0]


def _modulation(cond, ada_w, ada_b):
    L, D, N = ada_w.shape
    R = cond.shape[0]
    tn = 1024
    return pl.pallas_call(
        _mod_kernel,
        out_shape=jax.ShapeDtypeStruct((L, R, N), F32),
        grid=(L, N // tn),
        in_specs=[pl.BlockSpec((R, D), lambda l, j: (0, 0)),
                  pl.BlockSpec((1, D, tn), lambda l, j: (l, 0, j)),
                  pl.BlockSpec((1, 1, tn), lambda l, j: (l, 0, j))],
        out_specs=pl.BlockSpec((1, R, tn), lambda l, j: (l, 0, j)),
        compiler_params=_params(("parallel", "parallel")),
        name="adaln_mod",
    )(cond, ada_w, ada_b.reshape(L, 1, N))


def _inproj_kernel(x_ref, mod_ref, g_ref, w_ref, ca_ref, sa_ref, cb_ref, sb_ref,
                   gq_ref, gkv_ref, wq_ref, wkv_ref,
                   qa_ref, ka_ref, va_ref, qb_ref, kb_ref, vb_ref, pr_ref):
    D = x_ref.shape[-1]
    x = x_ref[0]
    mod = mod_ref[0, 0]
    sh1, sc1 = mod[:, 0:D], mod[:, D:2 * D]
    h = _bf(_rms(x, NORM_EPS) * g_ref[...] * (1.0 + sc1) + sh1)

    def proj(lo, hi):
        return _dot(h, w_ref[:, lo:hi])

    ca, sa = ca_ref[...], sa_ref[...]
    cb, sb = cb_ref[...], sb_ref[...]

    pq = proj(P_AQ, P_AK)
    pk = proj(P_AK, P_AV)
    for j in range(A_WIDTH // LANE):
        sl = slice(j * LANE, (j + 1) * LANE)
        qa_ref[0, :, sl] = _bf(_rope(pq[:, sl], ca, sa, A_DIM // 4) * (A_DIM ** -0.5))
        ka_ref[0, :, sl] = _bf(_rope(pk[:, sl], ca, sa, A_DIM // 4))
    va_ref[0] = _bf(proj(P_AV, P_BQ))

    pbq = proj(P_BQ, P_BKV)
    lane = _lane_iota(pbq.shape)
    cq = jnp.where(lane < B_Q_RANK, pbq, 0.0)
    cqn = cq * lax.rsqrt(jnp.sum(cq * cq, axis=-1, keepdims=True) * (1.0 / B_Q_RANK) + NORM_EPS) * gq_ref[...]
    qb = _mm(cqn, wq_ref[...]) * ((B_NOPE + B_ROPE) ** -0.5)
    ckv = proj(P_BKV, P_C)
    ckvn = _rms(ckv, NORM_EPS) * gkv_ref[...]
    kv = _mm(ckvn, wkv_ref[...])
    krb = pbq[:, LANE:2 * LANE]
    l1 = _lane_iota(krb.shape)
    kr = _rope(jnp.where((l1 >= B_NOPE) & (l1 < B_NOPE + B_ROPE), krb, 0.0), cb, sb, B_ROPE // 4)
    for j in range(B_HEADS):
        sl = slice(j * LANE, (j + 1) * LANE)
        qb_ref[0, :, sl] = _bf(_rope(qb[:, sl], cb, sb, B_ROPE // 4))
        kb_ref[0, :, sl] = _bf(kv[:, sl] + kr)
    vb_ref[0] = _bf(kv[:, B_HEADS * LANE:])

    pr_ref[0] = proj(P_C, P_TOTAL)


def _inproj(X, modsel, g, w_in, ropeA, ropeB, gq, gkv, wq, wkv):
    B, S, D = X.shape
    tm = ROW_TILE
    nctx_tiles = modsel.shape[1]
    row = lambda w: pl.BlockSpec((1, tm, w), lambda b, i: (b, i, 0))
    tab = pl.BlockSpec((tm, LANE), lambda b, i: (i, 0))
    outs = [jax.ShapeDtypeStruct((B, S, 512), BF16)] * 5 + [
        jax.ShapeDtypeStruct((B, S, B_WIDTH), BF16), jax.ShapeDtypeStruct((B, S, C_COLS), F32)]
    return pl.pallas_call(
        _inproj_kernel,
        out_shape=outs,
        grid=(B, S // tm),
        in_specs=[row(D),
                  pl.BlockSpec((1, 1, 1, modsel.shape[-1]), lambda b, i: (b, jnp.minimum(i, 1), 0, 0)),
                  _full((1, D)), _full(w_in.shape), tab, tab, tab, tab,
                  _full(gq.shape), _full(gkv.shape), _full(wq.shape), _full(wkv.shape)],
        out_specs=[row(512)] * 5 + [row(B_WIDTH), row(C_COLS)],
        compiler_params=_params(("parallel", "parallel")),
        name="in_proj",
    )(X, modsel, g, w_in, ropeA[0], ropeA[1], ropeB[0], ropeB[1], gq, gkv, wq, wkv)


def _online_step(s, v, m_ref, l_ref, acc_ref):
    m_old = m_ref[...]
    m_new = jnp.maximum(m_old, jnp.max(s, axis=-1, keepdims=True))
    alpha = jnp.exp(m_old - m_new)
    p = jnp.exp(s - m_new)
    l_ref[...] = alpha * l_ref[...] + jnp.sum(p, axis=-1, keepdims=True)
    acc_ref[...] = alpha * acc_ref[...] + _dot(_bf(p), v)
    m_ref[...] = m_new


def _diff_attn_kernel(n_ctx_q, n_ctx_kv, lam_init,
                      q_ref, k_ref, v_ref, lq1, lk1, lq2, lk2, g_ref, o_ref,
                      m1, l1, a1, m2, l2, a2):
    qi = pl.program_id(2)
    q = q_ref[0]
    lane = _lane_iota(q.shape)
    zero = jnp.zeros_like(q)
    q1 = jnp.where(lane < A_DIM, q, zero)
    q2 = jnp.where(lane >= A_DIM, q, zero)
    for m, l, a in ((m1, l1, a1), (m2, l2, a2)):
        m[...] = jnp.full(m.shape, -jnp.inf, F32)
        l[...] = jnp.zeros(l.shape, F32)
        a[...] = jnp.zeros(a.shape, F32)
    n_kv = jnp.where(qi < n_ctx_q, n_ctx_kv, k_ref.shape[1] // KV_TILE)

    def body(j, c):
        off = pl.multiple_of(j * KV_TILE, KV_TILE)
        k = k_ref[0, pl.ds(off, KV_TILE), :]
        v = v_ref[0, pl.ds(off, KV_TILE), :]
        _online_step(_dot(q1, k, 1, 1), v, m1, l1, a1)
        _online_step(_dot(q2, k, 1, 1), v, m2, l2, a2)
        return c

    lax.fori_loop(0, n_kv, body, 0)
    lam = (jnp.exp(jnp.sum(lq1[...] * lk1[...], axis=-1, keepdims=True))
           - jnp.exp(jnp.sum(lq2[...] * lk2[...], axis=-1, keepdims=True)) + lam_init)
    o = a1[...] / l1[...] - lam * (a2[...] / l2[...])
    o_ref[0] = _bf(_rms(o, SUBLN_EPS) * g_ref[...] * (1.0 - lam_init))


def _diff_attn(qa, ka, va, lq1, lk1, lq2, lk2, subln_g, lam_init, n_ctx, q_start):
    B, S, _ = qa.shape
    tq = ROW_TILE
    nq = S // tq - q_start
    kern = functools.partial(_diff_attn_kernel, n_ctx // tq - q_start, n_ctx // KV_TILE, lam_init)
    vec = _full((1, A_DIM))
    return pl.pallas_call(
        kern,
        out_shape=jax.ShapeDtypeStruct((B, S, A_WIDTH), BF16),
        grid=(B, A_HEADS, nq),
        in_specs=[pl.BlockSpec((1, tq, LANE), lambda b, h, i: (b, i + q_start, h)),
                  pl.BlockSpec((1, S, LANE), lambda b, h, i: (b, 0, h)),
                  pl.BlockSpec((1, S, LANE), lambda b, h, i: (b, 0, h)),
                  vec, vec, vec, vec, _full((1, LANE))],
        out_specs=pl.BlockSpec((1, tq, LANE), lambda b, h, i: (b, i + q_start, h)),
        scratch_shapes=[pltpu.VMEM((tq, 1), F32), pltpu.VMEM((tq, 1), F32), pltpu.VMEM((tq, LANE), F32)] * 2,
        compiler_params=_params(("parallel", "parallel", "parallel")),
        name="diff_attn",
    )(qa, ka, va, lq1, lk1, lq2, lk2, subln_g)


def _mla_attn_kernel(n_ctx_q, n_ctx_kv, q_ref, k_ref, v_ref, o_ref, m, l, acc):
    qi = pl.program_id(1)
    n_kv = jnp.where(qi < n_ctx_q, n_ctx_kv, k_ref.shape[1] // KV_TILE)
    out = jnp.zeros(o_ref.shape[1:], F32)
    for h in range(B_HEADS):
        sl = slice(h * LANE, (h + 1) * LANE)
        q = q_ref[0, :, sl]
        m[...] = jnp.full(m.shape, -jnp.inf, F32)
        l[...] = jnp.zeros(l.shape, F32)
        acc[...] = jnp.zeros(acc.shape, F32)

        def body(j, c):
            off = pl.multiple_of(j * KV_TILE, KV_TILE)
            k = k_ref[0, pl.ds(off, KV_TILE), sl]
            v = v_ref[0, pl.ds(off, KV_TILE), :]
            _online_step(_dot(q, k, 1, 1), v, m, l, acc)
            return c

        lax.fori_loop(0, n_kv, body, 0)
        lane = _lane_iota(out.shape)
        mine = (lane >= h * B_VDIM) & (lane < (h + 1) * B_VDIM)
        out = jnp.where(mine, acc[...] / l[...], out)
    o_ref[0] = _bf(out)


def _mla_attn(qb, kb, vb, n_ctx, q_start):
    B, S, _ = qb.shape
    tq = ROW_TILE
    nq = S // tq - q_start
    kern = functools.partial(_mla_attn_kernel, n_ctx // tq - q_start, n_ctx // KV_TILE)
    return pl.pallas_call(
        kern,
        out_shape=jax.ShapeDtypeStruct((B, S, B_WIDTH), BF16),
        grid=(B, nq),
        in_specs=[pl.BlockSpec((1, tq, 512), lambda b, i: (b, i + q_start, 0)),
                  pl.BlockSpec((1, S, 512), lambda b, i: (b, 0, 0)),
                  pl.BlockSpec((1, S, B_WIDTH), lambda b, i: (b, 0, 0))],
        out_specs=pl.BlockSpec((1, tq, B_WIDTH), lambda b, i: (b, i + q_start, 0)),
        scratch_shapes=[pltpu.VMEM((tq, 1), F32), pltpu.VMEM((tq, 1), F32), pltpu.VMEM((tq, B_WIDTH), F32)],
        compiler_params=_params(("parallel", "parallel")),
        name="mla_attn",
    )(qb, kb, vb)


def _rwkv_chunk_index(rev, n_ctx_chunks, n_chunks, j):
    if not rev:
        return j
    return jnp.where(j < n_ctx_chunks, n_ctx_chunks - 1 - j, n_chunks + n_ctx_chunks - 1 - j)


def _rwkv_kernel(rev, n_ctx_chunks, n_chunks,
                 cur_ref, prv_ref, nxt_ref, mup_ref, mun_ref, w0_ref, w2_ref, a0_ref, a2_ref, g2_ref,
                 kk_ref, ka_ref, rk_ref, bd_ref,
                 y_ref, bonus_ref, gate_ref, s_ref):
    C, W = CHUNK, C_WIDTH
    j = pl.program_id(1)
    c = _rwkv_chunk_index(rev, n_ctx_chunks, n_chunks, j)
    seq_first = (c == 0) | (c == n_ctx_chunks)
    seq_last = (c == n_ctx_chunks - 1) | (c == n_chunks - 1)

    @pl.when(j == 0)
    def _():
        s_ref[...] = jnp.zeros(s_ref.shape, F32)

    x = cur_ref[0]
    rows = _row_iota(x.shape)
    prev_row = jnp.where(seq_first, 0.0, prv_ref[0, SUBLANE - 1:SUBLANE, :])
    next_row = jnp.where(seq_last, 0.0, nxt_ref[0, 0:1, :])
    xp = jnp.where(rows == 0, prev_row, pltpu.roll(x, 1, axis=0))
    xn = jnp.where(rows == C - 1, next_row, pltpu.roll(x, C - 1, axis=0))
    xs = x + mup_ref[...] * (xp - x) + mun_ref[...] * (xn - x)

    r, k, v = xs[:, 0:W], xs[:, W:2 * W], xs[:, 2 * W:3 * W]
    wl = xs[:, 3 * W:3 * W + 2 * C_LORA]
    al = xs[:, 3 * W + 2 * C_LORA:3 * W + 4 * C_LORA]
    gl = xs[:, 3 * W + 4 * C_LORA:]

    bd = bd_ref[...]
    w = -_softplus(-(w0_ref[...] + _mm3(jnp.tanh(wl), w2_ref[...]))) - 0.5
    lw = -jnp.exp(w)
    a_ic = _sigmoid(a0_ref[...] + _mm3(al, a2_ref[...]))
    kkr = k * kk_ref[...]
    kk = kkr / jnp.maximum(jnp.sqrt(_mm_hl(kkr * kkr, bd)), 1e-12)
    kd = k * (1.0 + (a_ic - 1.0) * ka_ref[...])
    avec = -kk
    bvec = kk * a_ic
    bonus_ref[0] = _mm_hl(r * kd * rk_ref[...], bd) * v
    gate_ref[0] = _mm(_sigmoid(gl), g2_ref[...])

    rc = _row_iota((C, C))
    cc = _lane_iota((C, C))
    incl = (cc >= rc) if rev else (cc <= rc)
    strict = (cc > rc) if rev else (cc < rc)
    tri = jnp.where(incl, 1.0, 0.0).astype(BF16)
    l_hi = _bf(lw)
    l_md = _bf(lw - l_hi.astype(F32))
    l_lo = _bf(lw - l_hi.astype(F32) - l_md.astype(F32))
    cs = _dot(tri, l_hi) + (_dot(tri, l_md) + _dot(tri, l_lo))
    tot = cs[0:1, :] if rev else cs[C - 1:C, :]
    e_neg = jnp.exp(-cs)
    e_rem = jnp.exp(tot - cs)
    At = avec * jnp.exp(cs - lw)
    Rt = r * jnp.exp(cs)
    Bt = bvec * e_neg
    Kt = kd * e_neg
    Bg = bvec * e_rem
    Kg = kd * e_rem
    g_tot = jnp.exp(tot)

    rc2 = _row_iota((C, 2 * C))
    cc2 = _lane_iota((C, 2 * C)) & (C - 1)
    incl2 = (cc2 >= rc2) if rev else (cc2 <= rc2)
    strict2 = (cc2 > rc2) if rev else (cc2 < rc2)
    eye = jnp.where(rc == cc, 1.0, 0.0)
    pair = [((rc >> (lvl + 1)) == (cc >> (lvl + 1))) & ((rc >> lvl) != (cc >> lvl))
            for lvl in range(int(math.log2(C)))]
    head4 = (_lane_iota((C, 4 * W)) & (W - 1)) >> 6
    zv = jnp.concatenate([jnp.zeros_like(v), v], axis=0)
    BK = jnp.concatenate([Bt, Kt], axis=0)
    lane = _lane_iota((C, W))

    WUQY = jnp.zeros((C, 4 * W), F32)
    for h in range(C_HEADS):
        mine = (lane >= h * C_DIM) & (lane < (h + 1) * C_DIM)
        X = jnp.concatenate([jnp.where(mine, At, 0.0), jnp.where(mine, Rt, 0.0)], axis=0)
        G = _mm_nt(X, BK)
        Ltop = jnp.where(strict2, G[0:C], 0.0)
        Lbot = jnp.where(incl2, G[C:2 * C], 0.0)
        Lab = Ltop[:, 0:C]
        T = eye + jnp.where(pair[0], Lab, 0.0)
        for lvl in range(1, len(pair)):
            T = T + _mm(_mm(T, jnp.where(pair[lvl], Lab, 0.0)), T)
        AkV = _mm(Ltop, zv)
        WU = _mm(T, jnp.concatenate([At, AkV], axis=1))
        QY = _mm(Lbot, jnp.concatenate([WU, jnp.concatenate([jnp.zeros_like(v), v], axis=1)], axis=0))
        WUQY = jnp.where(head4 == h, jnp.concatenate([WU, QY], axis=1), WUQY)

    Wt, U0 = WUQY[:, 0:W], WUQY[:, W:2 * W]
    Qt, Y0 = Rt + WUQY[:, 2 * W:3 * W], WUQY[:, 3 * W:]
    S = s_ref[...]
    y_ref[0] = _mm_nt(Qt, S) + Y0
    rw = _row_iota((W, W))
    cw = _lane_iota((W, W))
    same_head = (rw >> 6) == (cw >> 6)
    Mbd = jnp.where(same_head, _mm_tn(Wt, Bg), 0.0) + jnp.where(rw == cw, g_tot, 0.0)
    Nbd = jnp.where(same_head, _mm_tn(U0, Bg) + _mm_tn(v, Kg), 0.0)
    s_ref[...] = _mm(S, Mbd) + Nbd


def _rwkv_scan(pr, rev, n_ctx, mu_prev, mu_next, w0, w2p, a0, a2p, g2, k_k, k_a, r_k, bd):
    B, S, _ = pr.shape
    C = CHUNK
    nch, ncc = S // C, n_ctx // C
    bpc = C // SUBLANE
    nblk = S // SUBLANE
    cidx = functools.partial(_rwkv_chunk_index, rev, ncc, nch)
    kern = functools.partial(_rwkv_kernel, rev, ncc, nch)
    out = jax.ShapeDtypeStruct((B, S, C_WIDTH), F32)
    ospec = pl.BlockSpec((1, C, C_WIDTH), lambda b, j: (b, cidx(j), 0))
    vec = _full((1, C_WIDTH))
    return pl.pallas_call(
        kern,
        out_shape=[out, out, out],
        grid=(B, nch),
        in_specs=[pl.BlockSpec((1, C, C_COLS), lambda b, j: (b, cidx(j), 0)),
                  pl.BlockSpec((1, SUBLANE, C_COLS), lambda b, j: (b, jnp.maximum(cidx(j) * bpc - 1, 0), 0)),
                  pl.BlockSpec((1, SUBLANE, C_COLS), lambda b, j: (b, jnp.minimum((cidx(j) + 1) * bpc, nblk - 1), 0)),
                  _full((1, C_COLS)), _full((1, C_COLS)), vec, _full(w2p.shape), vec, _full(a2p.shape),
                  _full(g2.shape), vec, vec, vec, _full(bd.shape)],
        out_specs=[ospec, ospec, ospec],
        scratch_shapes=[pltpu.VMEM((C_WIDTH, C_WIDTH), F32)],
        compiler_params=_params(("parallel", "arbitrary")),
        name="rwkv_bwd" if rev else "rwkv_fwd",
    )(pr, pr, pr, mu_prev, mu_next, w0, w2p, a0, a2p, g2, k_k, k_a, r_k, bd)


def _rwkv_combine_kernel(yf_ref, yb_ref, bf_ref, bb_ref, gate_ref, gng_ref, gnb_ref, bdm_ref, o_ref):
    y = yf_ref[0] + yb_ref[0]
    bdm = bdm_ref[...]
    mu = _mm_hl(y, bdm)
    d = y - mu
    var = _mm_hl(d * d, bdm)
    yn = d * lax.rsqrt(var + C_GN_EPS) * gng_ref[...] + gnb_ref[...]
    o_ref[0] = _bf((yn + bf_ref[0] + bb_ref[0]) * gate_ref[0])


def _rwkv_combine(yf, yb, bf, bb, gate, gn_g, gn_b, bdm):
    B, S, W = yf.shape
    tm = ROW_TILE
    row = pl.BlockSpec((1, tm, W), lambda b, i: (b, i, 0))
    return pl.pallas_call(
        _rwkv_combine_kernel,
        out_shape=jax.ShapeDtypeStruct((B, S, W), BF16),
        grid=(B, S // tm),
        in_specs=[row] * 5 + [_full((1, W)), _full((1, W)), _full((W, W))],
        out_specs=row,
        compiler_params=_params(("parallel", "parallel")),
        name="rwkv_combine",
    )(yf, yb, bf, bb, gate, gn_g, gn_b, bdm)


def _outproj_kernel(x_ref, oa_ref, ob_ref, oc_ref, mod_ref, w_ref, gpost_ref, gpre_ref, x1_ref, h2_ref):
    D = x_ref.shape[-1]
    mod = mod_ref[0, 0]
    gt1 = mod[:, 2 * D:3 * D]
    sh2, sc2 = mod[:, 3 * D:4 * D], mod[:, 4 * D:5 * D]
    o = (_dot(oa_ref[0], w_ref[0:A_WIDTH, :])
         + _dot(ob_ref[0], w_ref[A_WIDTH:A_WIDTH + B_WIDTH, :])
         + _dot(oc_ref[0], w_ref[A_WIDTH + B_WIDTH:, :]))
    x1 = x_ref[0] + gt1 * (_rms(o, NORM_EPS) * gpost_ref[...])
    x1_ref[0] = x1
    h2_ref[0] = _bf(_rms(x1, NORM_EPS) * gpre_ref[...] * (1.0 + sc2) + sh2)


def _outproj(X, oa, ob, oc, modsel, w_out, g_post, g_pre, t_start):
    B, S, D = X.shape
    tm = ROW_TILE
    nt = S // tm - t_start
    row = lambda w: pl.BlockSpec((1, tm, w), lambda b, i: (b, i + t_start, 0))
    return pl.pallas_call(
        _outproj_kernel,
        out_shape=[jax.ShapeDtypeStruct((B, S, D), F32), jax.ShapeDtypeStruct((B, S, D), BF16)],
        grid=(B, nt),
        in_specs=[row(D), row(A_WIDTH), row(B_WIDTH), row(C_WIDTH),
                  pl.BlockSpec((1, 1, 1, modsel.shape[-1]), lambda b, i: (b, jnp.minimum(i + t_start, 1), 0, 0)),
                  _full(w_out.shape), _full((1, D)), _full((1, D))],
        out_specs=[row(D), row(D)],
        compiler_params=_params(("parallel", "parallel")),
        name="out_proj",
    )(X, oa, ob, oc, modsel, w_out, g_post, g_pre)


def _ffn_kernel(n_ctx_tiles, n_tiles, t_start, ff_tile,
                x_ref, h_ref, hp_ref, hn_ref, mod_ref, wup_ref, cw_ref, cb_ref, wdn_ref, gpost_ref, o_ref):
    D = x_ref.shape[-1]
    tm = x_ref.shape[1]
    dff = wdn_ref.shape[0]
    i = pl.program_id(1) + t_start
    seq_first = (i == 0) | (i == n_ctx_tiles)
    seq_last = (i == n_ctx_tiles - 1) | (i == n_tiles - 1)
    hp = hp_ref[0]
    hn = hn_ref[0]
    hp = jnp.where(seq_first, jnp.zeros_like(hp), hp)
    hn = jnp.where(seq_last, jnp.zeros_like(hn), hn)
    hext = jnp.concatenate([hp, h_ref[0], hn], axis=0)
    halo = hp.shape[0]
    acc = jnp.zeros((tm, D), F32)
    for f in range(dff // ff_tile):
        parts = []
        for base in (0, dff):
            lo = base + f * ff_tile
            u = _dot(hext, wup_ref[:, lo:lo + ff_tile])
            cw = cw_ref[:, lo:lo + ff_tile]
            n = u.shape[0]
            up = pltpu.roll(u, 1, axis=0)[halo:halo + tm]
            un = pltpu.roll(u, n - 1, axis=0)[halo:halo + tm]
            parts.append(up * cw[0:1] + u[halo:halo + tm] * cw[1:2] + un * cw[2:3]
                         + cb_ref[:, lo:lo + ff_tile])
        act = _bf(_silu(parts[0]) * parts[1])
        acc = acc + _dot(act, wdn_ref[f * ff_tile:(f + 1) * ff_tile, :])
    gt2 = mod_ref[0, 0][:, 5 * D:6 * D]
    o_ref[0] = x_ref[0] + gt2 * (_rms(acc, NORM_EPS) * gpost_ref[...])


def _ffn(X1, H2, modsel, w_up, conv_w, conv_b, w_dn, g_post, n_ctx, t_start):
    B, S, D = X1.shape
    tm = ROW_TILE
    halo = 16
    nt = S // tm - t_start
    hb = tm // halo
    nhb = S // halo
    kern = functools.partial(_ffn_kernel, n_ctx // tm, S // tm, t_start, 256)
    row = pl.BlockSpec((1, tm, D), lambda b, i: (b, i + t_start, 0))
    return pl.pallas_call(
        kern,
        out_shape=jax.ShapeDtypeStruct((B, S, D), F32),
        grid=(B, nt),
        in_specs=[row, row,
                  pl.BlockSpec((1, halo, D), lambda b, i: (b, jnp.maximum((i + t_start) * hb - 1, t_start * hb), 0)),
                  pl.BlockSpec((1, halo, D), lambda b, i: (b, jnp.minimum((i + t_start + 1) * hb, nhb - 1), 0)),
                  pl.BlockSpec((1, 1, 1, modsel.shape[-1]), lambda b, i: (b, jnp.minimum(i + t_start, 1), 0, 0)),
                  _full(w_up.shape), _full(conv_w.shape), _full(conv_b.shape), _full(w_dn.shape), _full((1, D))],
        out_specs=row,
        compiler_params=_params(("parallel", "parallel")),
        name="conv_ffn",
    )(X1, H2, H2, H2, modsel, w_up, conv_w, conv_b, w_dn, g_post)


def _rope_tables(S, n_ctx, dim, lane_lo):
    nf = dim // 4
    t = jnp.arange(S - n_ctx, dtype=jnp.int32)
    rows = (t // GRID_W).astype(F32)
    cols = (t % GRID_W).astype(F32)
    inv = ROPE_BASE ** (-jnp.arange(nf, dtype=F32) / nf)
    ar, ac = rows[:, None] * inv, cols[:, None] * inv
    cos = jnp.concatenate([jnp.cos(ar), jnp.cos(ar), jnp.cos(ac), jnp.cos(ac)], axis=-1)
    sin = jnp.concatenate([-jnp.sin(ar), jnp.sin(ar), -jnp.sin(ac), jnp.sin(ac)], axis=-1)
    if lane_lo == 0:
        reps = LANE // dim
        cos, sin = jnp.tile(cos, (1, reps)), jnp.tile(sin, (1, reps))
    else:
        pad = ((0, 0), (lane_lo, LANE - lane_lo - dim))
        cos = jnp.pad(cos, pad, constant_values=1.0)
        sin = jnp.pad(sin, pad)
    cos = jnp.concatenate([jnp.ones((n_ctx, LANE), F32), cos], axis=0)
    sin = jnp.concatenate([jnp.zeros((n_ctx, LANE), F32), sin], axis=0)
    return cos, sin


def _layout_w_in(w):
    D = w.shape[0]
    a = w[:, :3 * A_WIDTH]
    o = 3 * A_WIDTH
    cq = w[:, o:o + B_Q_RANK]
    ckv = w[:, o + B_Q_RANK:o + B_Q_RANK + B_KV_RANK]
    kr = w[:, o + B_Q_RANK + B_KV_RANK:o + B_Q_RANK + B_KV_RANK + B_ROPE]
    c = w[:, o + B_Q_RANK + B_KV_RANK + B_ROPE:]
    z = jnp.zeros((D, 2 * LANE - B_Q_RANK - B_ROPE), w.dtype)
    return _bf(jnp.concatenate([a, cq, kr, z, ckv, c], axis=1))


def _layout_wq(w):
    hd = B_NOPE + B_ROPE
    w = w.reshape(B_Q_RANK, B_HEADS, hd)
    w = jnp.pad(w, ((0, 2 * LANE - B_Q_RANK), (0, 0), (0, LANE - hd)))
    return _bf(w.reshape(2 * LANE, B_HEADS * LANE))


def _layout_wkv(w):
    w = w.reshape(B_KV_RANK, B_HEADS, B_NOPE + B_VDIM)
    kn = jnp.pad(w[:, :, :B_NOPE], ((0, 0), (0, 0), (0, LANE - B_NOPE))).reshape(B_KV_RANK, B_HEADS * LANE)
    v = w[:, :, B_NOPE:].reshape(B_KV_RANK, B_WIDTH)
    return _bf(jnp.concatenate([kn, v], axis=1))


def _layout_lora(w, d):
    z = jnp.zeros_like(w)
    return jnp.concatenate([z, w] if d else [w, z], axis=0)


def _block_diag_ones(scale):
    i = np.arange(C_WIDTH) // C_DIM
    return jnp.asarray((i[:, None] == i[None, :]).astype(np.float32) * scale, dtype=BF16)


def kernel(x, c, ctx, c_ctx, ada_w, ada_b, mix_pre_g, mix_post_g, ffn_pre_g, ffn_post_g, w_in, w_out, lam_q1, lam_k1, lam_q2, lam_k2, a_subln_g, b_q_norm_g, b_w_q_up, b_kv_norm_g, b_w_kv_up, c_mu_prev, c_mu_next, c_w0, c_w2, c_a0, c_a2, c_g2, c_k_k, c_k_a, c_r_k, c_gn_g, c_gn_b, ffn_w_up, ffn_conv_w, ffn_conv_b, ffn_w_down):
    B, T, D = x.shape
    n_ctx = ctx.shape[1]
    S = n_ctx + T
    L = ada_w.shape[0]
    assert n_ctx == ROW_TILE and T % ROW_TILE == 0 and D % LANE == 0

    X = jnp.concatenate([ctx, x], axis=1)
    cond = jnp.concatenate([c, c_ctx[None, :], jnp.zeros((SUBLANE - B - 1, D), F32)], axis=0)
    mod = _modulation(cond, ada_w, ada_b)
    ropeA = _rope_tables(S, n_ctx, A_DIM, 0)
    ropeB = _rope_tables(S, n_ctx, B_ROPE, B_NOPE)
    bd1 = _block_diag_ones(1.0)
    bdm = _block_diag_ones(1.0 / C_DIM)
    row = lambda v: v.reshape(1, -1)

    for i in range(L):
        last = i == L - 1
        t0 = 1 if last else 0
        lam_init = 0.8 - 0.6 * math.exp(-0.3 * i)
        modsel = jnp.stack([jnp.broadcast_to(mod[i, B], (B, 6 * D)), mod[i, :B]], axis=1)[:, :, None, :]
        gq = jnp.pad(b_q_norm_g[i], (0, 2 * LANE - B_Q_RANK)).reshape(1, -1)
        qa, ka, va, qb, kb, vb, pr = _inproj(
            X, modsel, row(mix_pre_g[i]), _layout_w_in(w_in[i]), ropeA, ropeB,
            gq, row(b_kv_norm_g[i]), _layout_wq(b_w_q_up[i]), _layout_wkv(b_w_kv_up[i]))
        oa = _diff_attn(qa, ka, va, row(lam_q1[i]), row(lam_k1[i]), row(lam_q2[i]), row(lam_k2[i]),
                        jnp.tile(a_subln_g[i], LANE // (2 * A_DIM)).reshape(1, -1), lam_init, n_ctx, t0)
        ob = _mla_attn(qb, kb, vb, n_ctx, t0)
        scans = []
        for d in (0, 1):
            scans.append(_rwkv_scan(
                pr, bool(d), n_ctx, row(c_mu_prev[i]), row(c_mu_next[i]),
                row(c_w0[i, d]), _layout_lora(c_w2[i, d], d), row(c_a0[i, d]), _layout_lora(c_a2[i, d], d),
                c_g2[i], row(c_k_k[i]), row(c_k_a[i]), row(c_r_k[i]), bd1))
        (yf, bf_, gate), (yb, bb_, _) = scans
        oc = _rwkv_combine(yf, yb, bf_, bb_, gate, row(c_gn_g[i]), row(c_gn_b[i]), bdm)
        X1, H2 = _outproj(X, oa, ob, oc, modsel, _bf(w_out[i]), row(mix_post_g[i]), row(ffn_pre_g[i]), t0)
        X = _ffn(X1, H2, modsel, _bf(ffn_w_up[i]), ffn_conv_w[i], row(ffn_conv_b[i]), _bf(ffn_w_down[i]),
                 row(ffn_post_g[i]), n_ctx, t0)
    return X[:, n_ctx:]
```

```python
import functools
import math

import jax
import jax.numpy as jnp
import numpy as np
from jax import lax
from jax.experimental import pallas as pl
from jax.experimental.pallas import tpu as pltpu

F32 = jnp.float32
BF16 = jnp.bfloat16

GRID_W = 64
ROPE_BASE = 10000.0
NORM_EPS = 1e-6
SUBLN_EPS = 1e-5
A_HEADS, A_DIM = 4, 64
A_WIDTH = A_HEADS * 2 * A_DIM
B_HEADS, B_NOPE, B_ROPE, B_VDIM = 4, 64, 32, 64
B_Q_RANK, B_KV_RANK = 192, 128
B_WIDTH = B_HEADS * B_VDIM
C_HEADS, C_DIM = 4, 64
C_WIDTH = C_HEADS * C_DIM
C_LORA = 64
C_GATE_LORA = 128
C_GN_EPS = 64e-5
C_COLS = 3 * C_WIDTH + 4 * C_LORA + C_GATE_LORA

LANE = 128
SUBLANE = 8
ROW_TILE = 256
LAT_BLOCK = 512
LOG2E = 1.4426950408889634
CHUNK = 64
VMEM_LIMIT = 56 * 1024 * 1024

P_AQ, P_AK, P_AV = 0, 512, 1024
P_BQ = 1536
P_BKV = 1792
P_C = 1920
P_TOTAL = 3072


def _bf(x):
    return x.astype(BF16)


def _dot(a, b, ca=1, cb=0):
    return lax.dot_general(a, b, (((ca,), (cb,)), ((), ())), preferred_element_type=F32)


def _mm(a, b):
    return _dot(_bf(a), _bf(b))


def _mm_nt(a, b):
    return _dot(_bf(a), _bf(b), 1, 1)


def _mm_tn(a, b):
    return _dot(_bf(a.T), _bf(b))


def _split2(x):
    hi = _bf(x)
    lo = _bf(x - hi.astype(F32))
    return hi, lo


def _mm_hl(a, b):
    hi, lo = _split2(a)
    return _dot(hi, b) + _dot(lo, b)


def _mm3(a, b):
    ah, al = _split2(a)
    bh, bl = _split2(b)
    return _dot(ah, bh) + (_dot(ah, bl) + _dot(al, bh))


def _rms(x, eps):
    return x * lax.rsqrt(jnp.mean(x * x, axis=-1, keepdims=True) + eps)


def _sigmoid(x):
    return 1.0 / (1.0 + jnp.exp(-x))


def _silu(x):
    return x * _sigmoid(x)


def _softplus(x):
    return jnp.maximum(x, 0.0) + jnp.log(1.0 + jnp.exp(-jnp.abs(x)))


def _lane_iota(shape):
    return lax.broadcasted_iota(jnp.int32, shape, len(shape) - 1)


def _row_iota(shape):
    return lax.broadcasted_iota(jnp.int32, shape, len(shape) - 2)


def _rope(x, cos, sin, half):
    n = x.shape[-1]
    up = pltpu.roll(x, n - half, axis=1)
    dn = pltpu.roll(x, half, axis=1)
    first = (_lane_iota(x.shape) & half) == 0
    return x * cos + jnp.where(first, up, dn) * sin


def _params(sem):
    return pltpu.CompilerParams(dimension_semantics=sem, vmem_limit_bytes=VMEM_LIMIT)


def _full(shape):
    nd = len(shape)
    return pl.BlockSpec(shape, lambda *_: (0,) * nd)


def _mod_kernel(c_ref, w_ref, b_ref, o_ref):
    act = _silu(c_ref[...])
    o_ref[0] = _mm3(act, w_ref[0]) + b_ref[0]


def _modulation(cond, ada_w, ada_b):
    L, D, N = ada_w.shape
    R = cond.shape[0]
    tn = 1024
    return pl.pallas_call(
        _mod_kernel,
        out_shape=jax.ShapeDtypeStruct((L, R, N), F32),
        grid=(L, N // tn),
        in_specs=[pl.BlockSpec((R, D), lambda l, j: (0, 0)),
                  pl.BlockSpec((1, D, tn), lambda l, j: (l, 0, j)),
                  pl.BlockSpec((1, 1, tn), lambda l, j: (l, 0, j))],
        out_specs=pl.BlockSpec((1, R, tn), lambda l, j: (l, 0, j)),
        compiler_params=_params(("parallel", "parallel")),
        name="adaln_mod",
    )(cond, ada_w, ada_b.reshape(L, 1, N))


def _inproj_kernel(x_ref, mod_ref, g_ref, w_ref, ca_ref, sa_ref, cb_ref, sb_ref,
                   gq_ref, gkv_ref, wq_ref, wkv_ref,
                   qa_ref, ka_ref, va_ref, qb_ref, kb_ref, vb_ref, pr_ref):
    D = x_ref.shape[-1]
    x = x_ref[0]
    mod = mod_ref[0, 0]
    sh1, sc1 = mod[:, 0:D], mod[:, D:2 * D]
    h = _bf(_rms(x, NORM_EPS) * g_ref[...] * (1.0 + sc1) + sh1)

    def proj(lo, hi):
        return _dot(h, w_ref[:, lo:hi])

    ca, sa = ca_ref[...], sa_ref[...]
    cb, sb = cb_ref[...], sb_ref[...]

    pq = proj(P_AQ, P_AK)
    pk = proj(P_AK, P_AV)
    for j in range(A_WIDTH // LANE):
        sl = slice(j * LANE, (j + 1) * LANE)
        qa_ref[0, :, sl] = _bf(_rope(pq[:, sl], ca, sa, A_DIM // 4) * (A_DIM ** -0.5 * LOG2E))
        ka_ref[0, :, sl] = _bf(_rope(pk[:, sl], ca, sa, A_DIM // 4))
    va_ref[0] = _bf(proj(P_AV, P_BQ))

    pbq = proj(P_BQ, P_BKV)
    lane = _lane_iota(pbq.shape)
    cq = jnp.where(lane < B_Q_RANK, pbq, 0.0)
    cqn = cq * lax.rsqrt(jnp.sum(cq * cq, axis=-1, keepdims=True) * (1.0 / B_Q_RANK) + NORM_EPS) * gq_ref[...]
    qb = _mm(cqn, wq_ref[...]) * ((B_NOPE + B_ROPE) ** -0.5 * LOG2E)
    ckv = proj(P_BKV, P_C)
    ckvn = _rms(ckv, NORM_EPS) * gkv_ref[...]
    kv = _mm(ckvn, wkv_ref[...])
    krb = pbq[:, LANE:2 * LANE]
    l1 = _lane_iota(krb.shape)
    kr = _rope(jnp.where((l1 >= B_NOPE) & (l1 < B_NOPE + B_ROPE), krb, 0.0), cb, sb, B_ROPE // 4)
    for j in range(B_HEADS):
        sl = slice(j * LANE, (j + 1) * LANE)
        qb_ref[0, :, sl] = _bf(_rope(qb[:, sl], cb, sb, B_ROPE // 4))
        kb_ref[0, :, sl] = _bf(kv[:, sl] + kr)
        vh = kv[:, B_HEADS * LANE + j * LANE:B_HEADS * LANE + (j + 1) * LANE]
        vb_ref[0, :, sl] = _bf(jnp.where(l1 < B_VDIM, vh, 1.0))

    pr_ref[0] = proj(P_C, P_TOTAL)


def _inproj(X, modsel, g, w_in, ropeA, ropeB, gq, gkv, wq, wkv):
    B, S, D = X.shape
    tm = ROW_TILE
    nctx_tiles = modsel.shape[1]
    row = lambda w: pl.BlockSpec((1, tm, w), lambda b, i: (b, i, 0))
    tab = pl.BlockSpec((tm, LANE), lambda b, i: (i, 0))
    outs = [jax.ShapeDtypeStruct((B, S, 512), BF16)] * 6 + [jax.ShapeDtypeStruct((B, S, C_COLS), F32)]
    return pl.pallas_call(
        _inproj_kernel,
        out_shape=outs,
        grid=(B, S // tm),
        in_specs=[row(D),
                  pl.BlockSpec((1, 1, 1, modsel.shape[-1]), lambda b, i: (b, jnp.minimum(i, 1), 0, 0)),
                  _full((1, D)), _full(w_in.shape), tab, tab, tab, tab,
                  _full(gq.shape), _full(gkv.shape), _full(wq.shape), _full(wkv.shape)],
        out_specs=[row(512)] * 6 + [row(C_COLS)],
        compiler_params=_params(("parallel", "parallel")),
        name="in_proj",
    )(X, modsel, g, w_in, ropeA[0], ropeA[1], ropeB[0], ropeB[1], gq, gkv, wq, wkv)


def _attn_kernel(diff, lam_init, n_ctx, n_ctx_q, q_ref, k_ref, v_ref, *rest):
    if diff:
        lq1, lk1, lq2, lk2, g_ref, o_ref, sc_ref, sl_ref, m_ref, acc_ref = rest
    else:
        o_ref, sc_ref, sl_ref, m_ref, acc_ref = rest
    tq = q_ref.shape[1]
    n_lat = (k_ref.shape[1] - n_ctx) // LAT_BLOCK
    sees_latent = pl.program_id(2) >= n_ctx_q
    q = q_ref[0]
    lane = _lane_iota(q.shape)
    if diff:
        zero = jnp.zeros_like(q)
        qs = [jnp.where(lane < A_DIM, q, zero), jnp.where(lane >= A_DIM, q, zero)]
    else:
        qs = [q]

    def lat_off(j):
        return n_ctx + j * LAT_BLOCK

    def v_ext(off, size):
        v = v_ref[0, pl.ds(off, size), :]
        return jnp.concatenate([v, jnp.ones_like(v)], axis=1) if diff else v

    outs = []
    for mi, qm in enumerate(qs):
        def scores(off, size):
            return _dot(qm, k_ref[0, pl.ds(off, size), :], 1, 1)

        def colmax(s, m):
            for c in range(s.shape[1] // LANE):
                m = jnp.maximum(m, s[:, c * LANE:(c + 1) * LANE])
            return m

        s0 = scores(0, n_ctx)
        sc_ref[mi] = s0
        m_ref[...] = colmax(s0, jnp.full((tq, LANE), -jnp.inf, F32))

        @pl.when(sees_latent)
        def _():
            mrun = m_ref[...]
            for j in range(n_lat):
                s = scores(lat_off(j), LAT_BLOCK)
                sl_ref[mi * n_lat + j] = s
                mrun = colmax(s, mrun)
            m_ref[...] = mrun

        m = jnp.max(m_ref[...], axis=-1, keepdims=True)
        acc_ref[...] = _dot(_bf(jnp.exp2(sc_ref[mi] - m)), v_ext(0, n_ctx))

        @pl.when(sees_latent)
        def _():
            acc = acc_ref[...]
            for j in range(n_lat):
                p = _bf(jnp.exp2(sl_ref[mi * n_lat + j] - m))
                acc = acc + _dot(p, v_ext(lat_off(j), LAT_BLOCK))
            acc_ref[...] = acc

        a = acc_ref[...]
        if diff:
            outs.append(a[:, :LANE] / a[:, LANE:])
        else:
            outs.append(jnp.where(lane < B_VDIM, a / pltpu.roll(a, B_VDIM, axis=1), 0.0))

    if diff:
        lam = (jnp.exp(jnp.sum(lq1[...] * lk1[...], axis=-1, keepdims=True))
               - jnp.exp(jnp.sum(lq2[...] * lk2[...], axis=-1, keepdims=True)) + lam_init)
        o = outs[0] - lam * outs[1]
        o_ref[0] = _bf(_rms(o, SUBLN_EPS) * g_ref[...] * (1.0 - lam_init))
    else:
        o_ref[0] = _bf(outs[0])


def _attention(q, k, v, n_ctx, q_start, diff_args=None):
    B, S, _ = q.shape
    tq = ROW_TILE
    nq = S // tq - q_start
    diff = diff_args is not None
    n_maps = 2 if diff else 1
    n_lat_all = (S - n_ctx) // LAT_BLOCK
    if diff:
        lq1, lk1, lq2, lk2, subln_g, lam_init = diff_args
        extra = [lq1, lk1, lq2, lk2, subln_g]
        vec = _full((1, A_DIM))
        extra_specs = [vec, vec, vec, vec, _full((1, LANE))]
    else:
        lam_init, extra, extra_specs = None, [], []
    kern = functools.partial(_attn_kernel, diff, lam_init, n_ctx, n_ctx // tq - q_start)
    qspec = pl.BlockSpec((1, tq, LANE), lambda b, h, i: (b, i + q_start, h))
    kvspec = pl.BlockSpec((1, S, LANE), lambda b, h, i: (b, 0, h))
    return pl.pallas_call(
        kern,
        out_shape=jax.ShapeDtypeStruct(q.shape, BF16),
        grid=(B, q.shape[-1] // LANE, nq),
        in_specs=[qspec, kvspec, kvspec] + extra_specs,
        out_specs=qspec,
        scratch_shapes=[pltpu.VMEM((n_maps, tq, n_ctx), F32),
                        pltpu.VMEM((n_maps * n_lat_all, tq, LAT_BLOCK), F32),
                        pltpu.VMEM((tq, LANE), F32),
                        pltpu.VMEM((tq, 2 * LANE if diff else LANE), F32)],
        compiler_params=_params(("parallel", "parallel", "parallel")),
        name="diff_attn" if diff else "mla_attn",
    )(q, k, v, *extra)


def _rwkv_chunk_index(rev, n_ctx_chunks, n_chunks, j):
    if not rev:
        return j
    return jnp.where(j < n_ctx_chunks, n_ctx_chunks - 1 - j, n_chunks + n_ctx_chunks - 1 - j)


def _rwkv_kernel(rev, n_ctx_chunks, n_chunks,
                 cur_ref, prv_ref, nxt_ref, mup_ref, mun_ref, w0_ref, w2_ref, a0_ref, a2_ref, g2_ref,
                 kk_ref, ka_ref, rk_ref, bd_ref,
                 y_ref, bonus_ref, gate_ref, s_ref):
    C, W = CHUNK, C_WIDTH
    j = pl.program_id(1)
    c = _rwkv_chunk_index(rev, n_ctx_chunks, n_chunks, j)
    seq_first = (c == 0) | (c == n_ctx_chunks)
    seq_last = (c == n_ctx_chunks - 1) | (c == n_chunks - 1)

    @pl.when(j == 0)
    def _():
        s_ref[...] = jnp.zeros(s_ref.shape, F32)

    x = cur_ref[0]
    rows = _row_iota(x.shape)
    prev_row = jnp.where(seq_first, 0.0, prv_ref[0, SUBLANE - 1:SUBLANE, :])
    next_row = jnp.where(seq_last, 0.0, nxt_ref[0, 0:1, :])
    xp = jnp.where(rows == 0, prev_row, pltpu.roll(x, 1, axis=0))
    xn = jnp.where(rows == C - 1, next_row, pltpu.roll(x, C - 1, axis=0))
    xs = x + mup_ref[...] * (xp - x) + mun_ref[...] * (xn - x)

    r, k, v = xs[:, 0:W], xs[:, W:2 * W], xs[:, 2 * W:3 * W]
    wl = xs[:, 3 * W:3 * W + 2 * C_LORA]
    al = xs[:, 3 * W + 2 * C_LORA:3 * W + 4 * C_LORA]
    gl = xs[:, 3 * W + 4 * C_LORA:]

    bd = bd_ref[...]
    w = -_softplus(-(w0_ref[...] + _mm3(jnp.tanh(wl), w2_ref[...]))) - 0.5
    lw = -jnp.exp(w)
    a_ic = _sigmoid(a0_ref[...] + _mm3(al, a2_ref[...]))
    kkr = k * kk_ref[...]
    kk = kkr / jnp.maximum(jnp.sqrt(_mm_hl(kkr * kkr, bd)), 1e-12)
    kd = k * (1.0 + (a_ic - 1.0) * ka_ref[...])
    avec = -kk
    bvec = kk * a_ic
    bonus_ref[0] = _mm_hl(r * kd * rk_ref[...], bd) * v
    gate_ref[0] = _mm(_sigmoid(gl), g2_ref[...])

    rc = _row_iota((C, C))
    cc = _lane_iota((C, C))
    incl = (cc >= rc) if rev else (cc <= rc)
    strict = (cc > rc) if rev else (cc < rc)
    tri = jnp.where(incl, 1.0, 0.0).astype(BF16)
    l_hi = _bf(lw)
    l_md = _bf(lw - l_hi.astype(F32))
    l_lo = _bf(lw - l_hi.astype(F32) - l_md.astype(F32))
    cs = _dot(tri, l_hi) + (_dot(tri, l_md) + _dot(tri, l_lo))
    tot = cs[0:1, :] if rev else cs[C - 1:C, :]
    e_neg = jnp.exp(-cs)
    e_rem = jnp.exp(tot - cs)
    At = avec * jnp.exp(cs - lw)
    Rt = r * jnp.exp(cs)
    Bt = bvec * e_neg
    Kt = kd * e_neg
    Bg = bvec * e_rem
    Kg = kd * e_rem
    g_tot = jnp.exp(tot)

    rc2 = _row_iota((C, 2 * C))
    cc2 = _lane_iota((C, 2 * C)) & (C - 1)
    incl2 = (cc2 >= rc2) if rev else (cc2 <= rc2)
    strict2 = (cc2 > rc2) if rev else (cc2 < rc2)
    eye = jnp.where(rc == cc, 1.0, 0.0)
    pair = [((rc >> (lvl + 1)) == (cc >> (lvl + 1))) & ((rc >> lvl) != (cc >> lvl))
            for lvl in range(int(math.log2(C)))]
    head4 = (_lane_iota((C, 4 * W)) & (W - 1)) >> 6
    zv = jnp.concatenate([jnp.zeros_like(v), v], axis=0)
    BK = jnp.concatenate([Bt, Kt], axis=0)
    lane = _lane_iota((C, W))

    WUQY = jnp.zeros((C, 4 * W), F32)
    for h in range(C_HEADS):
        mine = (lane >= h * C_DIM) & (lane < (h + 1) * C_DIM)
        X = jnp.concatenate([jnp.where(mine, At, 0.0), jnp.where(mine, Rt, 0.0)], axis=0)
        G = _mm_nt(X, BK)
        Ltop = jnp.where(strict2, G[0:C], 0.0)
        Lbot = jnp.where(incl2, G[C:2 * C], 0.0)
        Lab = Ltop[:, 0:C]
        T = eye + jnp.where(pair[0], Lab, 0.0)
        for lvl in range(1, len(pair)):
            T = T + _mm(_mm(T, jnp.where(pair[lvl], Lab, 0.0)), T)
        AkV = _mm(Ltop, zv)
        WU = _mm(T, jnp.concatenate([At, AkV], axis=1))
        QY = _mm(Lbot, jnp.concatenate([WU, jnp.concatenate([jnp.zeros_like(v), v], axis=1)], axis=0))
        WUQY = jnp.where(head4 == h, jnp.concatenate([WU, QY], axis=1), WUQY)

    Wt, U0 = WUQY[:, 0:W], WUQY[:, W:2 * W]
    Qt, Y0 = Rt + WUQY[:, 2 * W:3 * W], WUQY[:, 3 * W:]
    S = s_ref[...]
    y_ref[0] = _mm_nt(Qt, S) + Y0
    rw = _row_iota((W, W))
    cw = _lane_iota((W, W))
    same_head = (rw >> 6) == (cw >> 6)
    Mbd = jnp.where(same_head, _mm_tn(Wt, Bg), 0.0) + jnp.where(rw == cw, g_tot, 0.0)
    Nbd = jnp.where(same_head, _mm_tn(U0, Bg) + _mm_tn(v, Kg), 0.0)
    s_ref[...] = _mm(S, Mbd) + Nbd


def _rwkv_scan(pr, rev, n_ctx, mu_prev, mu_next, w0, w2p, a0, a2p, g2, k_k, k_a, r_k, bd):
    B, S, _ = pr.shape
    C = CHUNK
    nch, ncc = S // C, n_ctx // C
    bpc = C // SUBLANE
    nblk = S // SUBLANE
    cidx = functools.partial(_rwkv_chunk_index, rev, ncc, nch)
    kern = functools.partial(_rwkv_kernel, rev, ncc, nch)
    out = jax.ShapeDtypeStruct((B, S, C_WIDTH), F32)
    ospec = pl.BlockSpec((1, C, C_WIDTH), lambda b, j: (b, cidx(j), 0))
    vec = _full((1, C_WIDTH))
    return pl.pallas_call(
        kern,
        out_shape=[out, out, out],
        grid=(B, nch),
        in_specs=[pl.BlockSpec((1, C, C_COLS), lambda b, j: (b, cidx(j), 0)),
                  pl.BlockSpec((1, SUBLANE, C_COLS), lambda b, j: (b, jnp.maximum(cidx(j) * bpc - 1, 0), 0)),
                  pl.BlockSpec((1, SUBLANE, C_COLS), lambda b, j: (b, jnp.minimum((cidx(j) + 1) * bpc, nblk - 1), 0)),
                  _full((1, C_COLS)), _full((1, C_COLS)), vec, _full(w2p.shape), vec, _full(a2p.shape),
                  _full(g2.shape), vec, vec, vec, _full(bd.shape)],
        out_specs=[ospec, ospec, ospec],
        scratch_shapes=[pltpu.VMEM((C_WIDTH, C_WIDTH), F32)],
        compiler_params=_params(("parallel", "arbitrary")),
        name="rwkv_bwd" if rev else "rwkv_fwd",
    )(pr, pr, pr, mu_prev, mu_next, w0, w2p, a0, a2p, g2, k_k, k_a, r_k, bd)


def _rwkv_combine_kernel(yf_ref, yb_ref, bf_ref, bb_ref, gate_ref, gng_ref, gnb_ref, bdm_ref, o_ref):
    y = yf_ref[0] + yb_ref[0]
    bdm = bdm_ref[...]
    mu = _mm_hl(y, bdm)
    d = y - mu
    var = _mm_hl(d * d, bdm)
    yn = d * lax.rsqrt(var + C_GN_EPS) * gng_ref[...] + gnb_ref[...]
    o_ref[0] = _bf((yn + bf_ref[0] + bb_ref[0]) * gate_ref[0])


def _rwkv_combine(yf, yb, bf, bb, gate, gn_g, gn_b, bdm):
    B, S, W = yf.shape
    tm = ROW_TILE
    row = pl.BlockSpec((1, tm, W), lambda b, i: (b, i, 0))
    return pl.pallas_call(
        _rwkv_combine_kernel,
        out_shape=jax.ShapeDtypeStruct((B, S, W), BF16),
        grid=(B, S // tm),
        in_specs=[row] * 5 + [_full((1, W)), _full((1, W)), _full((W, W))],
        out_specs=row,
        compiler_params=_params(("parallel", "parallel")),
        name="rwkv_combine",
    )(yf, yb, bf, bb, gate, gn_g, gn_b, bdm)


def _outproj_kernel(x_ref, oa_ref, ob_ref, oc_ref, mod_ref, w_ref, gpost_ref, gpre_ref, x1_ref, h2_ref):
    D = x_ref.shape[-1]
    mod = mod_ref[0, 0]
    gt1 = mod[:, 2 * D:3 * D]
    sh2, sc2 = mod[:, 3 * D:4 * D], mod[:, 4 * D:5 * D]
    nb = ob_ref.shape[-1]
    o = (_dot(oa_ref[0], w_ref[0:A_WIDTH, :])
         + _dot(ob_ref[0], w_ref[A_WIDTH:A_WIDTH + nb, :])
         + _dot(oc_ref[0], w_ref[A_WIDTH + nb:, :]))
    x1 = x_ref[0] + gt1 * (_rms(o, NORM_EPS) * gpost_ref[...])
    x1_ref[0] = x1
    h2_ref[0] = _bf(_rms(x1, NORM_EPS) * gpre_ref[...] * (1.0 + sc2) + sh2)


def _outproj(X, oa, ob, oc, modsel, w_out, g_post, g_pre, t_start):
    B, S, D = X.shape
    tm = ROW_TILE
    nt = S // tm - t_start
    row = lambda w: pl.BlockSpec((1, tm, w), lambda b, i: (b, i + t_start, 0))
    return pl.pallas_call(
        _outproj_kernel,
        out_shape=[jax.ShapeDtypeStruct((B, S, D), F32), jax.ShapeDtypeStruct((B, S, D), BF16)],
        grid=(B, nt),
        in_specs=[row(D), row(A_WIDTH), row(ob.shape[-1]), row(C_WIDTH),
                  pl.BlockSpec((1, 1, 1, modsel.shape[-1]), lambda b, i: (b, jnp.minimum(i + t_start, 1), 0, 0)),
                  _full(w_out.shape), _full((1, D)), _full((1, D))],
        out_specs=[row(D), row(D)],
        compiler_params=_params(("parallel", "parallel")),
        name="out_proj",
    )(X, oa, ob, oc, modsel, w_out, g_post, g_pre)


def _ffn_kernel(n_ctx_tiles, n_tiles, t_start, ff_tile,
                x_ref, h_ref, hp_ref, hn_ref, mod_ref, wup_ref, cw_ref, cb_ref, wdn_ref, gpost_ref, o_ref):
    D = x_ref.shape[-1]
    tm = x_ref.shape[1]
    dff = wdn_ref.shape[0]
    i = pl.program_id(1) + t_start
    seq_first = (i == 0) | (i == n_ctx_tiles)
    seq_last = (i == n_ctx_tiles - 1) | (i == n_tiles - 1)
    hp = hp_ref[0]
    hn = hn_ref[0]
    hp = jnp.where(seq_first, jnp.zeros_like(hp), hp)
    hn = jnp.where(seq_last, jnp.zeros_like(hn), hn)
    hext = jnp.concatenate([hp, h_ref[0], hn], axis=0)
    halo = hp.shape[0]
    acc = jnp.zeros((tm, D), F32)
    for f in range(dff // ff_tile):
        parts = []
        for base in (0, dff):
            lo = base + f * ff_tile
            u = _dot(hext, wup_ref[:, lo:lo + ff_tile])
            cw = cw_ref[:, lo:lo + ff_tile]
            n = u.shape[0]
            up = pltpu.roll(u, 1, axis=0)[halo:halo + tm]
            un = pltpu.roll(u, n - 1, axis=0)[halo:halo + tm]
            parts.append(up * cw[0:1] + u[halo:halo + tm] * cw[1:2] + un * cw[2:3]
                         + cb_ref[:, lo:lo + ff_tile])
        act = _bf(_silu(parts[0]) * parts[1])
        acc = acc + _dot(act, wdn_ref[f * ff_tile:(f + 1) * ff_tile, :])
    gt2 = mod_ref[0, 0][:, 5 * D:6 * D]
    o_ref[0] = x_ref[0] + gt2 * (_rms(acc, NORM_EPS) * gpost_ref[...])


def _ffn(X1, H2, modsel, w_up, conv_w, conv_b, w_dn, g_post, n_ctx, t_start):
    B, S, D = X1.shape
    tm = ROW_TILE
    halo = 16
    nt = S // tm - t_start
    hb = tm // halo
    nhb = S // halo
    kern = functools.partial(_ffn_kernel, n_ctx // tm, S // tm, t_start, 256)
    row = pl.BlockSpec((1, tm, D), lambda b, i: (b, i + t_start, 0))
    return pl.pallas_call(
        kern,
        out_shape=jax.ShapeDtypeStruct((B, S, D), F32),
        grid=(B, nt),
        in_specs=[row, row,
                  pl.BlockSpec((1, halo, D), lambda b, i: (b, jnp.maximum((i + t_start) * hb - 1, t_start * hb), 0)),
                  pl.BlockSpec((1, halo, D), lambda b, i: (b, jnp.minimum((i + t_start + 1) * hb, nhb - 1), 0)),
                  pl.BlockSpec((1, 1, 1, modsel.shape[-1]), lambda b, i: (b, jnp.minimum(i + t_start, 1), 0, 0)),
                  _full(w_up.shape), _full(conv_w.shape), _full(conv_b.shape), _full(w_dn.shape), _full((1, D))],
        out_specs=row,
        compiler_params=_params(("parallel", "parallel")),
        name="conv_ffn",
    )(X1, H2, H2, H2, modsel, w_up, conv_w, conv_b, w_dn, g_post)


def _rope_tables(S, n_ctx, dim, lane_lo):
    nf = dim // 4
    t = jnp.arange(S - n_ctx, dtype=jnp.int32)
    rows = (t // GRID_W).astype(F32)
    cols = (t % GRID_W).astype(F32)
    inv = ROPE_BASE ** (-jnp.arange(nf, dtype=F32) / nf)
    ar, ac = rows[:, None] * inv, cols[:, None] * inv
    cos = jnp.concatenate([jnp.cos(ar), jnp.cos(ar), jnp.cos(ac), jnp.cos(ac)], axis=-1)
    sin = jnp.concatenate([-jnp.sin(ar), jnp.sin(ar), -jnp.sin(ac), jnp.sin(ac)], axis=-1)
    if lane_lo == 0:
        reps = LANE // dim
        cos, sin = jnp.tile(cos, (1, reps)), jnp.tile(sin, (1, reps))
    else:
        pad = ((0, 0), (lane_lo, LANE - lane_lo - dim))
        cos = jnp.pad(cos, pad, constant_values=1.0)
        sin = jnp.pad(sin, pad)
    cos = jnp.concatenate([jnp.ones((n_ctx, LANE), F32), cos], axis=0)
    sin = jnp.concatenate([jnp.zeros((n_ctx, LANE), F32), sin], axis=0)
    return cos, sin


def _layout_w_in(w):
    D = w.shape[0]
    a = w[:, :3 * A_WIDTH]
    o = 3 * A_WIDTH
    cq = w[:, o:o + B_Q_RANK]
    ckv = w[:, o + B_Q_RANK:o + B_Q_RANK + B_KV_RANK]
    kr = w[:, o + B_Q_RANK + B_KV_RANK:o + B_Q_RANK + B_KV_RANK + B_ROPE]
    c = w[:, o + B_Q_RANK + B_KV_RANK + B_ROPE:]
    z = jnp.zeros((D, 2 * LANE - B_Q_RANK - B_ROPE), w.dtype)
    return _bf(jnp.concatenate([a, cq, kr, z, ckv, c], axis=1))


def _layout_wq(w):
    hd = B_NOPE + B_ROPE
    w = w.reshape(B_Q_RANK, B_HEADS, hd)
    w = jnp.pad(w, ((0, 2 * LANE - B_Q_RANK), (0, 0), (0, LANE - hd)))
    return _bf(w.reshape(2 * LANE, B_HEADS * LANE))


def _layout_wkv(w):
    w = w.reshape(B_KV_RANK, B_HEADS, B_NOPE + B_VDIM)
    pad = lambda t: jnp.pad(t, ((0, 0), (0, 0), (0, LANE - t.shape[-1]))).reshape(B_KV_RANK, B_HEADS * LANE)
    return _bf(jnp.concatenate([pad(w[:, :, :B_NOPE]), pad(w[:, :, B_NOPE:])], axis=1))


def _layout_w_out(w):
    D = w.shape[1]
    wb = w[A_WIDTH:A_WIDTH + B_WIDTH].reshape(B_HEADS, B_VDIM, D)
    wb = jnp.pad(wb, ((0, 0), (0, LANE - B_VDIM), (0, 0))).reshape(B_HEADS * LANE, D)
    return _bf(jnp.concatenate([w[:A_WIDTH], wb, w[A_WIDTH + B_WIDTH:]], axis=0))


def _layout_lora(w, d):
    z = jnp.zeros_like(w)
    return jnp.concatenate([z, w] if d else [w, z], axis=0)


def _block_diag_ones(scale):
    i = np.arange(C_WIDTH) // C_DIM
    return jnp.asarray((i[:, None] == i[None, :]).astype(np.float32) * scale, dtype=BF16)


def kernel(x, c, ctx, c_ctx, ada_w, ada_b, mix_pre_g, mix_post_g, ffn_pre_g, ffn_post_g, w_in, w_out, lam_q1, lam_k1, lam_q2, lam_k2, a_subln_g, b_q_norm_g, b_w_q_up, b_kv_norm_g, b_w_kv_up, c_mu_prev, c_mu_next, c_w0, c_w2, c_a0, c_a2, c_g2, c_k_k, c_k_a, c_r_k, c_gn_g, c_gn_b, ffn_w_up, ffn_conv_w, ffn_conv_b, ffn_w_down):
    B, T, D = x.shape
    n_ctx = ctx.shape[1]
    S = n_ctx + T
    L = ada_w.shape[0]
    assert n_ctx == ROW_TILE and T % ROW_TILE == 0 and D % LANE == 0

    X = jnp.concatenate([ctx, x], axis=1)
    cond = jnp.concatenate([c, c_ctx[None, :], jnp.zeros((SUBLANE - B - 1, D), F32)], axis=0)
    mod = _modulation(cond, ada_w, ada_b)
    ropeA = _rope_tables(S, n_ctx, A_DIM, 0)
    ropeB = _rope_tables(S, n_ctx, B_ROPE, B_NOPE)
    bd1 = _block_diag_ones(1.0)
    bdm = _block_diag_ones(1.0 / C_DIM)
    row = lambda v: v.reshape(1, -1)

    for i in range(L):
        last = i == L - 1
        t0 = 1 if last else 0
        lam_init = 0.8 - 0.6 * math.exp(-0.3 * i)
        modsel = jnp.stack([jnp.broadcast_to(mod[i, B], (B, 6 * D)), mod[i, :B]], axis=1)[:, :, None, :]
        gq = jnp.pad(b_q_norm_g[i], (0, 2 * LANE - B_Q_RANK)).reshape(1, -1)
        qa, ka, va, qb, kb, vb, pr = _inproj(
            X, modsel, row(mix_pre_g[i]), _layout_w_in(w_in[i]), ropeA, ropeB,
            gq, row(b_kv_norm_g[i]), _layout_wq(b_w_q_up[i]), _layout_wkv(b_w_kv_up[i]))
        oa = _attention(qa, ka, va, n_ctx, t0, (row(lam_q1[i]), row(lam_k1[i]), row(lam_q2[i]), row(lam_k2[i]),
                                                row(a_subln_g[i]), lam_init))
        ob = _attention(qb, kb, vb, n_ctx, t0)
        scans = []
        for d in (0, 1):
            scans.append(_rwkv_scan(
                pr, bool(d), n_ctx, row(c_mu_prev[i]), row(c_mu_next[i]),
                row(c_w0[i, d]), _layout_lora(c_w2[i, d], d), row(c_a0[i, d]), _layout_lora(c_a2[i, d], d),
                c_g2[i], row(c_k_k[i]), row(c_k_a[i]), row(c_r_k[i]), bd1))
        (yf, bf_, gate), (yb, bb_, _) = scans
        oc = _rwkv_combine(yf, yb, bf_, bb_, gate, row(c_gn_g[i]), row(c_gn_b[i]), bdm)
        X1, H2 = _outproj(X, oa, ob, oc, modsel, _layout_w_out(w_out[i]), row(mix_post_g[i]), row(ffn_pre_g[i]), t0)
        X = _ffn(X1, H2, modsel, _bf(ffn_w_up[i]), ffn_conv_w[i], row(ffn_conv_b[i]), _bf(ffn_w_down[i]),
                 row(ffn_post_g[i]), n_ctx, t0)
    return X[:, n_ctx:]
```

```python
import functools
import math

import jax
import jax.numpy as jnp
import numpy as np
from jax import lax
from jax.experimental import pallas as pl
from jax.experimental.pallas import tpu as pltpu

F32 = jnp.float32
BF16 = jnp.bfloat16

GRID_W = 64
ROPE_BASE = 10000.0
NORM_EPS = 1e-6
SUBLN_EPS = 1e-5
A_HEADS, A_DIM = 4, 64
A_WIDTH = A_HEADS * 2 * A_DIM
B_HEADS, B_NOPE, B_ROPE, B_VDIM = 4, 64, 32, 64
B_Q_RANK, B_KV_RANK = 192, 128
B_WIDTH = B_HEADS * B_VDIM
C_HEADS, C_DIM = 4, 64
C_WIDTH = C_HEADS * C_DIM
C_LORA = 64
C_GATE_LORA = 128
C_GN_EPS = 64e-5
C_COLS = 3 * C_WIDTH + 4 * C_LORA + C_GATE_LORA

LANE = 128
SUBLANE = 8
ROW_TILE = 256
LAT_BLOCK = 512
LOG2E = 1.4426950408889634
CHUNK = 64
VMEM_LIMIT = 56 * 1024 * 1024

P_AQ, P_AK, P_AV = 0, 512, 1024
P_BQ = 1536
P_BKV = 1792
P_C = 1920
P_TOTAL = 3072


def _bf(x):
    return x.astype(BF16)


def _dot(a, b, ca=1, cb=0):
    return lax.dot_general(a, b, (((ca,), (cb,)), ((), ())), preferred_element_type=F32)


def _mm(a, b):
    return _dot(_bf(a), _bf(b))


def _mm_nt(a, b):
    return _dot(_bf(a), _bf(b), 1, 1)


def _mm_tn(a, b):
    return _dot(_bf(a.T), _bf(b))


def _split2(x):
    hi = _bf(x)
    lo = _bf(x - hi.astype(F32))
    return hi, lo


def _mm_hl(a, b):
    hi, lo = _split2(a)
    return _dot(hi, b) + _dot(lo, b)


def _mm3(a, b):
    ah, al = _split2(a)
    bh, bl = _split2(b)
    return _dot(ah, bh) + (_dot(ah, bl) + _dot(al, bh))


def _rms(x, eps):
    return x * lax.rsqrt(jnp.mean(x * x, axis=-1, keepdims=True) + eps)


def _sigmoid(x):
    return 1.0 / (1.0 + jnp.exp(-x))


def _silu(x):
    return x * _sigmoid(x)


def _softplus(x):
    return jnp.maximum(x, 0.0) + jnp.log(1.0 + jnp.exp(-jnp.abs(x)))


def _lane_iota(shape):
    return lax.broadcasted_iota(jnp.int32, shape, len(shape) - 1)


def _row_iota(shape):
    return lax.broadcasted_iota(jnp.int32, shape, len(shape) - 2)


def _rope(x, cos, sin, half):
    n = x.shape[-1]
    up = pltpu.roll(x, n - half, axis=1)
    dn = pltpu.roll(x, half, axis=1)
    first = (_lane_iota(x.shape) & half) == 0
    return x * cos + jnp.where(first, up, dn) * sin


def _params(sem):
    return pltpu.CompilerParams(dimension_semantics=sem, vmem_limit_bytes=VMEM_LIMIT)


def _full(shape):
    nd = len(shape)
    return pl.BlockSpec(shape, lambda *_: (0,) * nd)


def _mod_kernel(c_ref, w_ref, b_ref, o_ref):
    act = _silu(c_ref[...])
    o_ref[0] = _mm3(act, w_ref[0]) + b_ref[0]


def _modulation(cond, ada_w, ada_b):
    L, D, N = ada_w.shape
    R = cond.shape[0]
    tn = 1024
    return pl.pallas_call(
        _mod_kernel,
        out_shape=jax.ShapeDtypeStruct((L, R, N), F32),
        grid=(L, N // tn),
        in_specs=[pl.BlockSpec((R, D), lambda l, j: (0, 0)),
                  pl.BlockSpec((1, D, tn), lambda l, j: (l, 0, j)),
                  pl.BlockSpec((1, 1, tn), lambda l, j: (l, 0, j))],
        out_specs=pl.BlockSpec((1, R, tn), lambda l, j: (l, 0, j)),
        compiler_params=_params(("parallel", "parallel")),
        name="adaln_mod",
    )(cond, ada_w, ada_b.reshape(L, 1, N))


def _inproj_kernel(x_ref, mod_ref, g_ref, w_ref, ca_ref, sa_ref, cb_ref, sb_ref,
                   gq_ref, gkv_ref, wq_ref, wkv_ref,
                   qa_ref, ka_ref, va_ref, qb_ref, kb_ref, vb_ref, pr_ref):
    D = x_ref.shape[-1]
    x = x_ref[0]
    mod = mod_ref[0, 0]
    sh1, sc1 = mod[:, 0:D], mod[:, D:2 * D]
    h = _bf(_rms(x, NORM_EPS) * g_ref[...] * (1.0 + sc1) + sh1)

    def proj(lo, hi):
        return _dot(h, w_ref[:, lo:hi])

    ca, sa = ca_ref[...], sa_ref[...]
    cb, sb = cb_ref[...], sb_ref[...]

    pq = proj(P_AQ, P_AK)
    pk = proj(P_AK, P_AV)
    for j in range(A_WIDTH // LANE):
        sl = slice(j * LANE, (j + 1) * LANE)
        qa_ref[0, :, sl] = _bf(_rope(pq[:, sl], ca, sa, A_DIM // 4) * (A_DIM ** -0.5 * LOG2E))
        ka_ref[0, :, sl] = _bf(_rope(pk[:, sl], ca, sa, A_DIM // 4))
    va_ref[0] = _bf(proj(P_AV, P_BQ))

    pbq = proj(P_BQ, P_BKV)
    lane = _lane_iota(pbq.shape)
    cq = jnp.where(lane < B_Q_RANK, pbq, 0.0)
    cqn = cq * lax.rsqrt(jnp.sum(cq * cq, axis=-1, keepdims=True) * (1.0 / B_Q_RANK) + NORM_EPS) * gq_ref[...]
    qb = _mm(cqn, wq_ref[...]) * ((B_NOPE + B_ROPE) ** -0.5 * LOG2E)
    ckv = proj(P_BKV, P_C)
    ckvn = _rms(ckv, NORM_EPS) * gkv_ref[...]
    kv = _mm(ckvn, wkv_ref[...])
    krb = pbq[:, LANE:2 * LANE]
    l1 = _lane_iota(krb.shape)
    kr = _rope(jnp.where((l1 >= B_NOPE) & (l1 < B_NOPE + B_ROPE), krb, 0.0), cb, sb, B_ROPE // 4)
    for j in range(B_HEADS):
        sl = slice(j * LANE, (j + 1) * LANE)
        qb_ref[0, :, sl] = _bf(_rope(qb[:, sl], cb, sb, B_ROPE // 4))
        kb_ref[0, :, sl] = _bf(kv[:, sl] + kr)
        vh = kv[:, B_HEADS * LANE + j * LANE:B_HEADS * LANE + (j + 1) * LANE]
        vb_ref[0, :, sl] = _bf(jnp.where(l1 < B_VDIM, vh, 1.0))

    pr_ref[0] = proj(P_C, P_TOTAL)


def _inproj(X, modsel, g, w_in, ropeA, ropeB, gq, gkv, wq, wkv):
    B, S, D = X.shape
    tm = ROW_TILE
    row = lambda w: pl.BlockSpec((1, tm, w), lambda b, i: (b, i, 0))
    tab = pl.BlockSpec((tm, LANE), lambda b, i: (i, 0))
    outs = [jax.ShapeDtypeStruct((B, S, 512), BF16)] * 6 + [jax.ShapeDtypeStruct((B, S, C_COLS), F32)]
    return pl.pallas_call(
        _inproj_kernel,
        out_shape=outs,
        grid=(B, S // tm),
        in_specs=[row(D),
                  pl.BlockSpec((1, 1, 1, modsel.shape[-1]), lambda b, i: (b, jnp.minimum(i, 1), 0, 0)),
                  _full((1, D)), _full(w_in.shape), tab, tab, tab, tab,
                  _full(gq.shape), _full(gkv.shape), _full(wq.shape), _full(wkv.shape)],
        out_specs=[row(512)] * 6 + [row(C_COLS)],
        compiler_params=_params(("parallel", "parallel")),
        name="in_proj",
    )(X, modsel, g, w_in, ropeA[0], ropeA[1], ropeB[0], ropeB[1], gq, gkv, wq, wkv)


def _attn_kernel(diff, lam_init, n_ctx, n_ctx_q, q_ref, k_ref, v_ref, *rest):
    if diff:
        lq1, lk1, lq2, lk2, g_ref, o_ref, sc_ref, sl_ref, m_ref, acc_ref = rest
    else:
        o_ref, sc_ref, sl_ref, m_ref, acc_ref = rest
    tq = q_ref.shape[1]
    n_lat = (k_ref.shape[1] - n_ctx) // LAT_BLOCK
    sees_latent = pl.program_id(2) >= n_ctx_q
    q = q_ref[0]
    lane = _lane_iota(q.shape)
    if diff:
        zero = jnp.zeros_like(q)
        qs = [jnp.where(lane < A_DIM, q, zero), jnp.where(lane >= A_DIM, q, zero)]
    else:
        qs = [q]

    def lat_off(j):
        return n_ctx + j * LAT_BLOCK

    def v_ext(off, size):
        v = v_ref[0, pl.ds(off, size), :]
        return jnp.concatenate([v, jnp.ones_like(v)], axis=1) if diff else v

    outs = []
    for mi, qm in enumerate(qs):
        def scores(off, size):
            return _dot(qm, k_ref[0, pl.ds(off, size), :], 1, 1)

        def colmax(s, m):
            for c in range(s.shape[1] // LANE):
                m = jnp.maximum(m, s[:, c * LANE:(c + 1) * LANE])
            return m

        s0 = scores(0, n_ctx)
        sc_ref[mi] = s0
        m_ref[...] = colmax(s0, jnp.full((tq, LANE), -jnp.inf, F32))

        @pl.when(sees_latent)
        def _():
            mrun = m_ref[...]
            for j in range(n_lat):
                s = scores(lat_off(j), LAT_BLOCK)
                sl_ref[mi * n_lat + j] = s
                mrun = colmax(s, mrun)
            m_ref[...] = mrun

        m = jnp.max(m_ref[...], axis=-1, keepdims=True)
        acc_ref[...] = _dot(_bf(jnp.exp2(sc_ref[mi] - m)), v_ext(0, n_ctx))

        @pl.when(sees_latent)
        def _():
            acc = acc_ref[...]
            for j in range(n_lat):
                p = _bf(jnp.exp2(sl_ref[mi * n_lat + j] - m))
                acc = acc + _dot(p, v_ext(lat_off(j), LAT_BLOCK))
            acc_ref[...] = acc

        a = acc_ref[...]
        if diff:
            outs.append(a[:, :LANE] / a[:, LANE:])
        else:
            outs.append(jnp.where(lane < B_VDIM, a / pltpu.roll(a, B_VDIM, axis=1), 0.0))

    if diff:
        lam = (jnp.exp(jnp.sum(lq1[...] * lk1[...], axis=-1, keepdims=True))
               - jnp.exp(jnp.sum(lq2[...] * lk2[...], axis=-1, keepdims=True)) + lam_init)
        o = outs[0] - lam * outs[1]
        o_ref[0] = _bf(_rms(o, SUBLN_EPS) * g_ref[...] * (1.0 - lam_init))
    else:
        o_ref[0] = _bf(outs[0])


def _attention(q, k, v, n_ctx, q_start, diff_args=None):
    B, S, _ = q.shape
    tq = ROW_TILE
    nq = S // tq - q_start
    diff = diff_args is not None
    n_maps = 2 if diff else 1
    assert (S - n_ctx) % LAT_BLOCK == 0
    n_lat_all = (S - n_ctx) // LAT_BLOCK
    if diff:
        lq1, lk1, lq2, lk2, subln_g, lam_init = diff_args
        extra = [lq1, lk1, lq2, lk2, subln_g]
        vec = _full((1, A_DIM))
        extra_specs = [vec, vec, vec, vec, _full((1, LANE))]
    else:
        lam_init, extra, extra_specs = None, [], []
    kern = functools.partial(_attn_kernel, diff, lam_init, n_ctx, n_ctx // tq - q_start)
    qspec = pl.BlockSpec((1, tq, LANE), lambda b, h, i: (b, i + q_start, h))
    kvspec = pl.BlockSpec((1, S, LANE), lambda b, h, i: (b, 0, h))
    return pl.pallas_call(
        kern,
        out_shape=jax.ShapeDtypeStruct((B, nq * tq, q.shape[-1]), BF16),
        grid=(B, q.shape[-1] // LANE, nq),
        in_specs=[qspec, kvspec, kvspec] + extra_specs,
        out_specs=pl.BlockSpec((1, tq, LANE), lambda b, h, i: (b, i, h)),
        scratch_shapes=[pltpu.VMEM((n_maps, tq, n_ctx), F32),
                        pltpu.VMEM((n_maps * n_lat_all, tq, LAT_BLOCK), F32),
                        pltpu.VMEM((tq, LANE), F32),
                        pltpu.VMEM((tq, 2 * LANE if diff else LANE), F32)],
        compiler_params=_params(("parallel", "parallel", "parallel")),
        name="diff_attn" if diff else "mla_attn",
    )(q, k, v, *extra)


def _rwkv_chunk_index(rev, n_ctx_chunks, n_chunks, j):
    if not rev:
        return j
    return jnp.where(j < n_ctx_chunks, n_ctx_chunks - 1 - j, n_chunks + n_ctx_chunks - 1 - j)


def _rwkv_local_kernel(n_tiles,
                       cur_ref, prv_ref, nxt_ref, mup_ref, mun_ref, w0_ref, w2_ref, a0_ref, a2_ref, g2_ref,
                       kk_ref, ka_ref, rk_ref, bd_ref,
                       qf_ref, y0f_ref, mf_ref, nf_ref, qb_ref, y0b_ref, mb_ref, nb_ref, bonus_ref, gate_ref):
    C, W = CHUNK, C_WIDTH
    R = cur_ref.shape[1]
    i = pl.program_id(1)
    seq_first = i <= 1
    seq_last = (i == 0) | (i == n_tiles - 1)

    x = cur_ref[0]
    rows = _row_iota(x.shape)
    prev_row = jnp.where(seq_first, 0.0, prv_ref[0, SUBLANE - 1:SUBLANE, :])
    next_row = jnp.where(seq_last, 0.0, nxt_ref[0, 0:1, :])
    xp = jnp.where(rows == 0, prev_row, pltpu.roll(x, 1, axis=0))
    xn = jnp.where(rows == R - 1, next_row, pltpu.roll(x, R - 1, axis=0))
    xs = x + mup_ref[...] * (xp - x) + mun_ref[...] * (xn - x)

    r, k, v = xs[:, 0:W], xs[:, W:2 * W], xs[:, 2 * W:3 * W]
    wl = jnp.tanh(xs[:, 3 * W:3 * W + 2 * C_LORA])
    al = xs[:, 3 * W + 2 * C_LORA:3 * W + 4 * C_LORA]
    gl = xs[:, 3 * W + 4 * C_LORA:]

    bd = bd_ref[...]
    kkr = k * kk_ref[...]
    kk = kkr / jnp.maximum(jnp.sqrt(_mm_hl(kkr * kkr, bd)), 1e-12)
    gate_ref[0] = _mm(_sigmoid(gl), g2_ref[...])

    rr = _row_iota((R, R))
    cr = _lane_iota((R, R))
    same_chunk = (rr >> 6) == (cr >> 6)
    rc = _row_iota((C, C))
    cc = _lane_iota((C, C))
    rc2 = _row_iota((C, 2 * C))
    cc2 = _lane_iota((C, 2 * C)) & (C - 1)
    eye = jnp.where(rc == cc, 1.0, 0.0)
    pair = [((rc >> (lvl + 1)) == (cc >> (lvl + 1))) & ((rc >> lvl) != (cc >> lvl))
            for lvl in range(int(math.log2(C)))]
    lane = _lane_iota((C, W))
    head4 = (_lane_iota((C, 4 * W)) & (W - 1)) >> 6
    rw = _row_iota((W, W))
    cw = _lane_iota((W, W))
    same_head = (rw >> 6) == (cw >> 6)
    diag_c = _row_iota((C, W)) == (lane & (C - 1))

    bonus = jnp.zeros((R, W), F32)
    outs = ((qf_ref, y0f_ref, mf_ref, nf_ref), (qb_ref, y0b_ref, mb_ref, nb_ref))
    head_masks = [(lane >= h * C_DIM) & (lane < (h + 1) * C_DIM) for h in range(C_HEADS)]
    chunks = []
    for d, rev in enumerate((False, True)):
        w = -_softplus(-(w0_ref[d:d + 1] + _mm3(wl, w2_ref[d]))) - 0.5
        lw = -jnp.exp(w)
        a_ic = _sigmoid(a0_ref[d:d + 1] + _mm3(al, a2_ref[d]))
        kd = k * (1.0 + (a_ic - 1.0) * ka_ref[...])
        avec = -kk
        bvec = kk * a_ic
        bonus = bonus + _mm_hl(r * kd * rk_ref[...], bd) * v

        tri = jnp.where(same_chunk & ((cr >= rr) if rev else (cr <= rr)), 1.0, 0.0).astype(BF16)
        l_hi = _bf(lw)
        l_md = _bf(lw - l_hi.astype(F32))
        l_lo = _bf(lw - l_hi.astype(F32) - l_md.astype(F32))
        cs = _dot(tri, l_hi) + (_dot(tri, l_md) + _dot(tri, l_lo))
        e_neg = jnp.exp(-cs)
        At_all = avec * jnp.exp(cs - lw)
        Rt_all = r * jnp.exp(cs)
        Bt_all = bvec * e_neg
        Kt_all = kd * e_neg
        for g in range(R // C):
            sl = slice(g * C, (g + 1) * C)
            csg = cs[sl]
            tot = csg[0:1, :] if rev else csg[C - 1:C, :]
            e_rem = jnp.exp(tot - csg)
            chunks.append(dict(d=d, g=g, rev=rev, sl=sl, At=At_all[sl], Rt=Rt_all[sl], v=v[sl], tot=tot,
                               BK=jnp.concatenate([Bt_all[sl], Kt_all[sl]], axis=0),
                               Bg=bvec[sl] * e_rem, Kg=kd[sl] * e_rem))
    bonus_ref[0] = bonus

    chains = [(ch, h) for ch in chunks for h in range(C_HEADS)]
    G = [_mm_nt(jnp.concatenate([jnp.where(head_masks[h], ch["At"], 0.0),
                                 jnp.where(head_masks[h], ch["Rt"], 0.0)], axis=0), ch["BK"])
         for ch, h in chains]
    strict2 = {False: cc2 < rc2, True: cc2 > rc2}
    incl2 = {False: cc2 <= rc2, True: cc2 >= rc2}
    Ltop = [jnp.where(strict2[ch["rev"]], g_[0:C], 0.0) for g_, (ch, h) in zip(G, chains)]
    Lbot = [jnp.where(incl2[ch["rev"]], g_[C:2 * C], 0.0) for g_, (ch, h) in zip(G, chains)]
    Lab = [lt[:, 0:C] for lt in Ltop]
    T = [eye + jnp.where(pair[0], la, 0.0) for la in Lab]
    for lvl in range(1, len(pair)):
        P = [_mm(t, jnp.where(pair[lvl], la, 0.0)) for t, la in zip(T, Lab)]
        T = [t + _mm(p, t) for t, p in zip(T, P)]
    zero_v = jnp.zeros((C, W), F32)
    AkV = [_mm(lt, jnp.concatenate([zero_v, ch["v"]], axis=0)) for lt, (ch, h) in zip(Ltop, chains)]
    WU = [_mm(t, jnp.concatenate([ch["At"], akv], axis=1)) for t, akv, (ch, h) in zip(T, AkV, chains)]
    QY = [_mm(lb, jnp.concatenate([wu, jnp.concatenate([zero_v, ch["v"]], axis=1)], axis=0))
          for lb, wu, (ch, h) in zip(Lbot, WU, chains)]

    for ci, ch in enumerate(chunks):
        WUQY = jnp.zeros((C, 4 * W), F32)
        for h in range(C_HEADS):
            k_ = ci * C_HEADS + h
            WUQY = jnp.where(head4 == h, jnp.concatenate([WU[k_], QY[k_]], axis=1), WUQY)
        q_ref, y0_ref, m_ref, n_ref = outs[ch["d"]]
        q_ref[0, ch["sl"], :] = _bf(ch["Rt"] + WUQY[:, 2 * W:3 * W])
        y0_ref[0, ch["sl"], :] = WUQY[:, 3 * W:]
        Mbd = jnp.where(same_head, _mm_tn(WUQY[:, 0:W], ch["Bg"]), 0.0)
        Nbd = jnp.where(same_head, _mm_tn(WUQY[:, W:2 * W], ch["Bg"]) + _mm_tn(ch["v"], ch["Kg"]), 0.0)
        Mc = Mbd[0:C] + Mbd[C:2 * C] + Mbd[2 * C:3 * C] + Mbd[3 * C:]
        m_ref[0, ch["g"]] = _bf(Mc + jnp.where(diag_c, jnp.exp(ch["tot"]), 0.0))
        n_ref[0, ch["g"]] = Nbd[0:C] + Nbd[C:2 * C] + Nbd[2 * C:3 * C] + Nbd[3 * C:]


def _rwkv_local(pr, mu_prev, mu_next, w0, w2p, a0, a2p, g2, k_k, k_a, r_k, bd):
    B, S, _ = pr.shape
    R, C, W = ROW_TILE, CHUNK, C_WIDTH
    nt, gpt = S // R, R // C
    bpt = R // SUBLANE
    nblk = S // SUBLANE
    rowspec = lambda w: pl.BlockSpec((1, R, w), lambda b, i: (b, i, 0))
    mspec = pl.BlockSpec((1, gpt, C, W), lambda b, i: (b, i, 0, 0))
    seq = lambda dt: jax.ShapeDtypeStruct((B, S, W), dt)
    mat = lambda dt: jax.ShapeDtypeStruct((B, S // C, C, W), dt)
    per_dir = [seq(BF16), seq(F32), mat(BF16), mat(F32)]
    vec = _full((1, W))
    return pl.pallas_call(
        functools.partial(_rwkv_local_kernel, nt),
        out_shape=per_dir * 2 + [seq(F32), seq(F32)],
        grid=(B, nt),
        in_specs=[rowspec(C_COLS),
                  pl.BlockSpec((1, SUBLANE, C_COLS), lambda b, i: (b, jnp.maximum(i * bpt - 1, 0), 0)),
                  pl.BlockSpec((1, SUBLANE, C_COLS), lambda b, i: (b, jnp.minimum((i + 1) * bpt, nblk - 1), 0)),
                  _full((1, C_COLS)), _full((1, C_COLS)), _full(w0.shape), _full(w2p.shape), _full(a0.shape),
                  _full(a2p.shape), _full(g2.shape), vec, vec, vec, _full(bd.shape)],
        out_specs=[rowspec(W), rowspec(W), mspec, mspec] * 2 + [rowspec(W), rowspec(W)],
        compiler_params=_params(("parallel", "parallel")),
        name="rwkv_local",
    )(pr, pr, pr, mu_prev, mu_next, w0, w2p, a0, a2p, g2, k_k, k_a, r_k, bd)


def _rwkv_state_kernel(qf_ref, y0f_ref, mf_ref, nf_ref, qb_ref, y0b_ref, mb_ref, nb_ref, yf_ref, yb_ref, s_ref):
    C, W = CHUNK, C_WIDTH
    nb = qf_ref.shape[0]

    @pl.when(pl.program_id(0) == 0)
    def _():
        s_ref[...] = jnp.zeros(s_ref.shape, F32)

    head = _lane_iota((C, W)) >> 6

    def expand(mc):
        return jnp.concatenate([jnp.where(head == h, mc, jnp.zeros_like(mc)) for h in range(C_HEADS)], axis=0)

    dirs = ((qf_ref, y0f_ref, mf_ref, nf_ref, yf_ref), (qb_ref, y0b_ref, mb_ref, nb_ref, yb_ref))
    for d, (q_ref, y0_ref, m_ref, n_ref, y_ref) in enumerate(dirs):
        for b in range(nb):
            S = _bf(s_ref[d, b])
            y_ref[b] = _dot(q_ref[b], S, 1, 1) + y0_ref[b]
            s_ref[d, b] = _dot(S, expand(m_ref[b, 0])) + expand(n_ref[b, 0])


def _rwkv_state(loc, n_ctx):
    qf = loc[0]
    B, S, W = qf.shape
    C = CHUNK
    nch, ncc = S // C, n_ctx // C
    fwd = lambda j: j
    bwd = functools.partial(_rwkv_chunk_index, True, ncc, nch)
    seqspec = lambda idx: pl.BlockSpec((B, C, W), lambda j: (0, idx(j), 0))
    matspec = lambda idx: pl.BlockSpec((B, 1, C, W), lambda j: (0, idx(j), 0, 0))
    out = jax.ShapeDtypeStruct((B, S, W), F32)
    return pl.pallas_call(
        _rwkv_state_kernel,
        out_shape=[out, out],
        grid=(nch,),
        in_specs=[seqspec(fwd), seqspec(fwd), matspec(fwd), matspec(fwd),
                  seqspec(bwd), seqspec(bwd), matspec(bwd), matspec(bwd)],
        out_specs=[seqspec(fwd), seqspec(bwd)],
        scratch_shapes=[pltpu.VMEM((2, B, W, W), F32)],
        compiler_params=_params(("arbitrary",)),
        name="rwkv_state",
    )(*loc[:8])


def _rwkv_combine_kernel(yf_ref, yb_ref, bonus_ref, gate_ref, gng_ref, gnb_ref, bdm_ref, o_ref):
    y = yf_ref[0] + yb_ref[0]
    bdm = bdm_ref[...]
    mu = _mm_hl(y, bdm)
    d = y - mu
    var = _mm_hl(d * d, bdm)
    yn = d * lax.rsqrt(var + C_GN_EPS) * gng_ref[...] + gnb_ref[...]
    o_ref[0] = _bf((yn + bonus_ref[0]) * gate_ref[0])


def _rwkv_combine(yf, yb, bonus, gate, gn_g, gn_b, bdm, t0):
    B, S, W = yf.shape
    tm = ROW_TILE
    nt = S // tm - t0
    row = pl.BlockSpec((1, tm, W), lambda b, i: (b, i + t0, 0))
    return pl.pallas_call(
        _rwkv_combine_kernel,
        out_shape=jax.ShapeDtypeStruct((B, nt * tm, W), BF16),
        grid=(B, nt),
        in_specs=[row] * 4 + [_full((1, W)), _full((1, W)), _full((W, W))],
        out_specs=pl.BlockSpec((1, tm, W), lambda b, i: (b, i, 0)),
        compiler_params=_params(("parallel", "parallel")),
        name="rwkv_combine",
    )(yf, yb, bonus, gate, gn_g, gn_b, bdm)


def _outproj_kernel(x_ref, oa_ref, ob_ref, oc_ref, mod_ref, w_ref, gpost_ref, gpre_ref, x1_ref, h2_ref):
    D = x_ref.shape[-1]
    mod = mod_ref[0, 0]
    gt1 = mod[:, 2 * D:3 * D]
    sh2, sc2 = mod[:, 3 * D:4 * D], mod[:, 4 * D:5 * D]
    nb = ob_ref.shape[-1]
    o = (_dot(oa_ref[0], w_ref[0:A_WIDTH, :])
         + _dot(ob_ref[0], w_ref[A_WIDTH:A_WIDTH + nb, :])
         + _dot(oc_ref[0], w_ref[A_WIDTH + nb:, :]))
    x1 = x_ref[0] + gt1 * (_rms(o, NORM_EPS) * gpost_ref[...])
    x1_ref[0] = x1
    h2_ref[0] = _bf(_rms(x1, NORM_EPS) * gpre_ref[...] * (1.0 + sc2) + sh2)


def _outproj(X, oa, ob, oc, modsel, w_out, g_post, g_pre, t_start):
    B, S, D = X.shape
    tm = ROW_TILE
    nt = S // tm - t_start
    row = lambda w: pl.BlockSpec((1, tm, w), lambda b, i: (b, i, 0))
    return pl.pallas_call(
        _outproj_kernel,
        out_shape=[jax.ShapeDtypeStruct((B, nt * tm, D), F32), jax.ShapeDtypeStruct((B, nt * tm, D), BF16)],
        grid=(B, nt),
        in_specs=[pl.BlockSpec((1, tm, D), lambda b, i: (b, i + t_start, 0)),
                  row(A_WIDTH), row(ob.shape[-1]), row(C_WIDTH),
                  pl.BlockSpec((1, 1, 1, modsel.shape[-1]), lambda b, i: (b, jnp.minimum(i + t_start, 1), 0, 0)),
                  _full(w_out.shape), _full((1, D)), _full((1, D))],
        out_specs=[row(D), row(D)],
        compiler_params=_params(("parallel", "parallel")),
        name="out_proj",
    )(X, oa, ob, oc, modsel, w_out, g_post, g_pre)


def _ffn_kernel(n_ctx_tiles, n_tiles, t_start, ff_tile,
                x_ref, h_ref, hp_ref, hn_ref, mod_ref, wup_ref, cw_ref, cb_ref, wdn_ref, gpost_ref, o_ref):
    D = x_ref.shape[-1]
    tm = x_ref.shape[1]
    dff = wdn_ref.shape[0]
    i = pl.program_id(1) + t_start
    seq_first = (i == 0) | (i == n_ctx_tiles)
    seq_last = (i == n_ctx_tiles - 1) | (i == n_tiles - 1)
    hp = hp_ref[0]
    hn = hn_ref[0]
    hp = jnp.where(seq_first, jnp.zeros_like(hp), hp)
    hn = jnp.where(seq_last, jnp.zeros_like(hn), hn)
    hext = jnp.concatenate([hp, h_ref[0], hn], axis=0)
    halo = hp.shape[0]
    acc = jnp.zeros((tm, D), F32)
    for f in range(dff // ff_tile):
        parts = []
        for base in (0, dff):
            lo = base + f * ff_tile
            u = _dot(hext, wup_ref[:, lo:lo + ff_tile])
            cw = cw_ref[:, lo:lo + ff_tile]
            n = u.shape[0]
            up = pltpu.roll(u, 1, axis=0)[halo:halo + tm]
            un = pltpu.roll(u, n - 1, axis=0)[halo:halo + tm]
            parts.append(up * cw[0:1] + u[halo:halo + tm] * cw[1:2] + un * cw[2:3]
                         + cb_ref[:, lo:lo + ff_tile])
        act = _bf(_silu(parts[0]) * parts[1])
        acc = acc + _dot(act, wdn_ref[f * ff_tile:(f + 1) * ff_tile, :])
    gt2 = mod_ref[0, 0][:, 5 * D:6 * D]
    o_ref[0] = x_ref[0] + gt2 * (_rms(acc, NORM_EPS) * gpost_ref[...])


def _ffn(X1, H2, modsel, w_up, conv_w, conv_b, w_dn, g_post, n_ctx, t_start):
    B, S, D = X1.shape
    tm = ROW_TILE
    halo = 16
    nt = S // tm
    hb = tm // halo
    nhb = S // halo
    kern = functools.partial(_ffn_kernel, n_ctx // tm, nt + t_start, t_start, 256)
    row = pl.BlockSpec((1, tm, D), lambda b, i: (b, i, 0))
    return pl.pallas_call(
        kern,
        out_shape=jax.ShapeDtypeStruct((B, S, D), F32),
        grid=(B, nt),
        in_specs=[row, row,
                  pl.BlockSpec((1, halo, D), lambda b, i: (b, jnp.maximum(i * hb - 1, 0), 0)),
                  pl.BlockSpec((1, halo, D), lambda b, i: (b, jnp.minimum((i + 1) * hb, nhb - 1), 0)),
                  pl.BlockSpec((1, 1, 1, modsel.shape[-1]), lambda b, i: (b, jnp.minimum(i + t_start, 1), 0, 0)),
                  _full(w_up.shape), _full(conv_w.shape), _full(conv_b.shape), _full(w_dn.shape), _full((1, D))],
        out_specs=row,
        compiler_params=_params(("parallel", "parallel")),
        name="conv_ffn",
    )(X1, H2, H2, H2, modsel, w_up, conv_w, conv_b, w_dn, g_post)


def _rope_tables(S, n_ctx, dim, lane_lo):
    nf = dim // 4
    t = jnp.arange(S - n_ctx, dtype=jnp.int32)
    rows = (t // GRID_W).astype(F32)
    cols = (t % GRID_W).astype(F32)
    inv = ROPE_BASE ** (-jnp.arange(nf, dtype=F32) / nf)
    ar, ac = rows[:, None] * inv, cols[:, None] * inv
    cos = jnp.concatenate([jnp.cos(ar), jnp.cos(ar), jnp.cos(ac), jnp.cos(ac)], axis=-1)
    sin = jnp.concatenate([-jnp.sin(ar), jnp.sin(ar), -jnp.sin(ac), jnp.sin(ac)], axis=-1)
    if lane_lo == 0:
        reps = LANE // dim
        cos, sin = jnp.tile(cos, (1, reps)), jnp.tile(sin, (1, reps))
    else:
        pad = ((0, 0), (lane_lo, LANE - lane_lo - dim))
        cos = jnp.pad(cos, pad, constant_values=1.0)
        sin = jnp.pad(sin, pad)
    cos = jnp.concatenate([jnp.ones((n_ctx, LANE), F32), cos], axis=0)
    sin = jnp.concatenate([jnp.zeros((n_ctx, LANE), F32), sin], axis=0)
    return cos, sin


def _layout_w_in(w):
    D = w.shape[0]
    a = w[:, :3 * A_WIDTH]
    o = 3 * A_WIDTH
    cq = w[:, o:o + B_Q_RANK]
    ckv = w[:, o + B_Q_RANK:o + B_Q_RANK + B_KV_RANK]
    kr = w[:, o + B_Q_RANK + B_KV_RANK:o + B_Q_RANK + B_KV_RANK + B_ROPE]
    c = w[:, o + B_Q_RANK + B_KV_RANK + B_ROPE:]
    z = jnp.zeros((D, 2 * LANE - B_Q_RANK - B_ROPE), w.dtype)
    return _bf(jnp.concatenate([a, cq, kr, z, ckv, c], axis=1))


def _layout_wq(w):
    hd = B_NOPE + B_ROPE
    w = w.reshape(B_Q_RANK, B_HEADS, hd)
    w = jnp.pad(w, ((0, 2 * LANE - B_Q_RANK), (0, 0), (0, LANE - hd)))
    return _bf(w.reshape(2 * LANE, B_HEADS * LANE))


def _layout_wkv(w):
    w = w.reshape(B_KV_RANK, B_HEADS, B_NOPE + B_VDIM)
    pad = lambda t: jnp.pad(t, ((0, 0), (0, 0), (0, LANE - t.shape[-1]))).reshape(B_KV_RANK, B_HEADS * LANE)
    return _bf(jnp.concatenate([pad(w[:, :, :B_NOPE]), pad(w[:, :, B_NOPE:])], axis=1))


def _layout_w_out(w):
    D = w.shape[1]
    wb = w[A_WIDTH:A_WIDTH + B_WIDTH].reshape(B_HEADS, B_VDIM, D)
    wb = jnp.pad(wb, ((0, 0), (0, LANE - B_VDIM), (0, 0))).reshape(B_HEADS * LANE, D)
    return _bf(jnp.concatenate([w[:A_WIDTH], wb, w[A_WIDTH + B_WIDTH:]], axis=0))


def _layout_lora(w, d):
    z = jnp.zeros_like(w)
    return jnp.concatenate([z, w] if d else [w, z], axis=0)


def _block_diag_ones(scale):
    i = np.arange(C_WIDTH) // C_DIM
    return jnp.asarray((i[:, None] == i[None, :]).astype(np.float32) * scale, dtype=BF16)


def kernel(x, c, ctx, c_ctx, ada_w, ada_b, mix_pre_g, mix_post_g, ffn_pre_g, ffn_post_g, w_in, w_out, lam_q1, lam_k1, lam_q2, lam_k2, a_subln_g, b_q_norm_g, b_w_q_up, b_kv_norm_g, b_w_kv_up, c_mu_prev, c_mu_next, c_w0, c_w2, c_a0, c_a2, c_g2, c_k_k, c_k_a, c_r_k, c_gn_g, c_gn_b, ffn_w_up, ffn_conv_w, ffn_conv_b, ffn_w_down):
    B, T, D = x.shape
    n_ctx = ctx.shape[1]
    S = n_ctx + T
    L = ada_w.shape[0]
    assert n_ctx == ROW_TILE and T % ROW_TILE == 0 and D % LANE == 0

    X = jnp.concatenate([ctx, x], axis=1)
    cond = jnp.concatenate([c, c_ctx[None, :], jnp.zeros((SUBLANE - B - 1, D), F32)], axis=0)
    mod = _modulation(cond, ada_w, ada_b)
    ropeA = _rope_tables(S, n_ctx, A_DIM, 0)
    ropeB = _rope_tables(S, n_ctx, B_ROPE, B_NOPE)
    bd1 = _block_diag_ones(1.0)
    bdm = _block_diag_ones(1.0 / C_DIM)
    row = lambda v: v.reshape(1, -1)

    for i in range(L):
        last = i == L - 1
        t0 = 1 if last else 0
        lam_init = 0.8 - 0.6 * math.exp(-0.3 * i)
        modsel = jnp.stack([jnp.broadcast_to(mod[i, B], (B, 6 * D)), mod[i, :B]], axis=1)[:, :, None, :]
        gq = jnp.pad(b_q_norm_g[i], (0, 2 * LANE - B_Q_RANK)).reshape(1, -1)
        qa, ka, va, qb, kb, vb, pr = _inproj(
            X, modsel, row(mix_pre_g[i]), _layout_w_in(w_in[i]), ropeA, ropeB,
            gq, row(b_kv_norm_g[i]), _layout_wq(b_w_q_up[i]), _layout_wkv(b_w_kv_up[i]))
        oa = _attention(qa, ka, va, n_ctx, t0, (row(lam_q1[i]), row(lam_k1[i]), row(lam_q2[i]), row(lam_k2[i]),
                                                row(a_subln_g[i]), lam_init))
        ob = _attention(qb, kb, vb, n_ctx, t0)
        loc = _rwkv_local(pr, row(c_mu_prev[i]), row(c_mu_next[i]),
                          c_w0[i], jnp.stack([_layout_lora(c_w2[i, d], d) for d in (0, 1)]),
                          c_a0[i], jnp.stack([_layout_lora(c_a2[i, d], d) for d in (0, 1)]),
                          c_g2[i], row(c_k_k[i]), row(c_k_a[i]), row(c_r_k[i]), bd1)
        yf, yb = _rwkv_state(loc, n_ctx)
        oc = _rwkv_combine(yf, yb, loc[8], loc[9], row(c_gn_g[i]), row(c_gn_b[i]), bdm, t0)
        X1, H2 = _outproj(X, oa, ob, oc, modsel, _layout_w_out(w_out[i]), row(mix_post_g[i]), row(ffn_pre_g[i]), t0)
        X = _ffn(X1, H2, modsel, _bf(ffn_w_up[i]), ffn_conv_w[i], row(ffn_conv_b[i]), _bf(ffn_w_down[i]),
                 row(ffn_post_g[i]), n_ctx, t0)
    return X
```

```python
import functools
import math

import jax
import jax.numpy as jnp
import numpy as np
from jax import lax
from jax.experimental import pallas as pl
from jax.experimental.pallas import tpu as pltpu

F32 = jnp.float32
BF16 = jnp.bfloat16

GRID_W = 64
ROPE_BASE = 10000.0
NORM_EPS = 1e-6
SUBLN_EPS = 1e-5
A_HEADS, A_DIM = 4, 64
A_WIDTH = A_HEADS * 2 * A_DIM
B_HEADS, B_NOPE, B_ROPE, B_VDIM = 4, 64, 32, 64
B_Q_RANK, B_KV_RANK = 192, 128
B_WIDTH = B_HEADS * B_VDIM
C_HEADS, C_DIM = 4, 64
C_WIDTH = C_HEADS * C_DIM
C_LORA = 64
C_GATE_LORA = 128
C_GN_EPS = 64e-5
C_COLS = 3 * C_WIDTH + 4 * C_LORA + C_GATE_LORA

LANE = 128
SUBLANE = 8
ROW_TILE = 256
LAT_BLOCK = 512
LOG2E = 1.4426950408889634
CHUNK = 64
VMEM_LIMIT = 56 * 1024 * 1024

P_AQ, P_AK, P_AV = 0, 512, 1024
P_BQ = 1536
P_BKV = 1792
P_C = 1920
P_TOTAL = 3072


def _bf(x):
    return x.astype(BF16)


def _dot(a, b, ca=1, cb=0):
    return lax.dot_general(a, b, (((ca,), (cb,)), ((), ())), preferred_element_type=F32)


def _mm(a, b):
    return _dot(_bf(a), _bf(b))


def _mm_nt(a, b):
    return _dot(_bf(a), _bf(b), 1, 1)


def _mm_tn(a, b):
    return _dot(_bf(a.T), _bf(b))


def _split2(x):
    hi = _bf(x)
    lo = _bf(x - hi.astype(F32))
    return hi, lo


def _mm_hl(a, b):
    hi, lo = _split2(a)
    return _dot(hi, b) + _dot(lo, b)


def _mm3(a, b):
    ah, al = _split2(a)
    bh, bl = _split2(b)
    return _dot(ah, bh) + (_dot(ah, bl) + _dot(al, bh))


def _rms(x, eps):
    return x * lax.rsqrt(jnp.mean(x * x, axis=-1, keepdims=True) + eps)


def _sigmoid(x):
    return 1.0 / (1.0 + jnp.exp(-x))


def _silu(x):
    return x * _sigmoid(x)


def _softplus(x):
    return jnp.maximum(x, 0.0) + jnp.log(1.0 + jnp.exp(-jnp.abs(x)))


def _lane_iota(shape):
    return lax.broadcasted_iota(jnp.int32, shape, len(shape) - 1)


def _row_iota(shape):
    return lax.broadcasted_iota(jnp.int32, shape, len(shape) - 2)


def _rope(x, cos, sin, half):
    n = x.shape[-1]
    up = pltpu.roll(x, n - half, axis=1)
    dn = pltpu.roll(x, half, axis=1)
    first = (_lane_iota(x.shape) & half) == 0
    return x * cos + jnp.where(first, up, dn) * sin


def _params(sem):
    return pltpu.CompilerParams(dimension_semantics=sem, vmem_limit_bytes=VMEM_LIMIT)


def _full(shape):
    nd = len(shape)
    return pl.BlockSpec(shape, lambda *_: (0,) * nd)


def _mod_kernel(c_ref, w_ref, b_ref, o_ref):
    act = _silu(c_ref[...])
    o_ref[0] = _mm3(act, w_ref[0]) + b_ref[0]


def _modulation(cond, ada_w, ada_b):
    L, D, N = ada_w.shape
    R = cond.shape[0]
    tn = 1024
    return pl.pallas_call(
        _mod_kernel,
        out_shape=jax.ShapeDtypeStruct((L, R, N), F32),
        grid=(L, N // tn),
        in_specs=[pl.BlockSpec((R, D), lambda l, j: (0, 0)),
                  pl.BlockSpec((1, D, tn), lambda l, j: (l, 0, j)),
                  pl.BlockSpec((1, 1, tn), lambda l, j: (l, 0, j))],
        out_specs=pl.BlockSpec((1, R, tn), lambda l, j: (l, 0, j)),
        compiler_params=_params(("parallel", "parallel")),
        name="adaln_mod",
    )(cond, ada_w, ada_b.reshape(L, 1, N))


def _inproj_kernel(x_ref, mod_ref, g_ref, w_ref, ca_ref, sa_ref, cb_ref, sb_ref,
                   gq_ref, gkv_ref, wq_ref, wkv_ref,
                   qa_ref, ka_ref, va_ref, qb_ref, kb_ref, vb_ref, pr_ref):
    D = x_ref.shape[-1]
    x = x_ref[0]
    mod = mod_ref[0, 0]
    sh1, sc1 = mod[:, 0:D], mod[:, D:2 * D]
    h = _bf(_rms(x, NORM_EPS) * g_ref[...] * (1.0 + sc1) + sh1)

    def proj(lo, hi):
        return _dot(h, w_ref[:, lo:hi])

    ca, sa = ca_ref[...], sa_ref[...]
    cb, sb = cb_ref[...], sb_ref[...]

    pq = proj(P_AQ, P_AK)
    pk = proj(P_AK, P_AV)
    for j in range(A_WIDTH // LANE):
        sl = slice(j * LANE, (j + 1) * LANE)
        qa_ref[0, :, sl] = _bf(_rope(pq[:, sl], ca, sa, A_DIM // 4) * (A_DIM ** -0.5 * LOG2E))
        ka_ref[0, :, sl] = _bf(_rope(pk[:, sl], ca, sa, A_DIM // 4))
    va_ref[0] = _bf(proj(P_AV, P_BQ))

    pbq = proj(P_BQ, P_BKV)
    lane = _lane_iota(pbq.shape)
    cq = jnp.where(lane < B_Q_RANK, pbq, 0.0)
    cqn = cq * lax.rsqrt(jnp.sum(cq * cq, axis=-1, keepdims=True) * (1.0 / B_Q_RANK) + NORM_EPS) * gq_ref[...]
    qb = _mm(cqn, wq_ref[...]) * ((B_NOPE + B_ROPE) ** -0.5 * LOG2E)
    ckv = proj(P_BKV, P_C)
    ckvn = _rms(ckv, NORM_EPS) * gkv_ref[...]
    kv = _mm(ckvn, wkv_ref[...])
    krb = pbq[:, LANE:2 * LANE]
    l1 = _lane_iota(krb.shape)
    kr = _rope(jnp.where((l1 >= B_NOPE) & (l1 < B_NOPE + B_ROPE), krb, 0.0), cb, sb, B_ROPE // 4)
    for j in range(B_HEADS):
        sl = slice(j * LANE, (j + 1) * LANE)
        qb_ref[0, :, sl] = _bf(_rope(qb[:, sl], cb, sb, B_ROPE // 4))
        kb_ref[0, :, sl] = _bf(kv[:, sl] + kr)
        vh = kv[:, B_HEADS * LANE + j * LANE:B_HEADS * LANE + (j + 1) * LANE]
        vb_ref[0, :, sl] = _bf(jnp.where(l1 < B_VDIM, vh, 1.0))

    pr_ref[0] = proj(P_C, P_TOTAL)


def _inproj(X, modsel, g, w_in, ropeA, ropeB, gq, gkv, wq, wkv):
    B, S, D = X.shape
    tm = ROW_TILE
    row = lambda w: pl.BlockSpec((1, tm, w), lambda b, i: (b, i, 0))
    tab = pl.BlockSpec((tm, LANE), lambda b, i: (i, 0))
    outs = [jax.ShapeDtypeStruct((B, S, 512), BF16)] * 6 + [jax.ShapeDtypeStruct((B, S, C_COLS), F32)]
    return pl.pallas_call(
        _inproj_kernel,
        out_shape=outs,
        grid=(B, S // tm),
        in_specs=[row(D),
                  pl.BlockSpec((1, 1, 1, modsel.shape[-1]), lambda b, i: (b, jnp.minimum(i, 1), 0, 0)),
                  _full((1, D)), _full(w_in.shape), tab, tab, tab, tab,
                  _full(gq.shape), _full(gkv.shape), _full(wq.shape), _full(wkv.shape)],
        out_specs=[row(512)] * 6 + [row(C_COLS)],
        compiler_params=_params(("parallel", "parallel")),
        name="in_proj",
    )(X, modsel, g, w_in, ropeA[0], ropeA[1], ropeB[0], ropeB[1], gq, gkv, wq, wkv)


def _attn_kernel(diff, lam_init, n_ctx, n_lat, q_ref, k_ref, v_ref, *rest):
    if diff:
        lq1, lk1, lq2, lk2, g_ref, o_ref, sc_ref, sl_ref = rest
    else:
        o_ref, sc_ref, sl_ref = rest
    tq = q_ref.shape[1]
    q = q_ref[0]
    lane = _lane_iota(q.shape)
    if diff:
        zero = jnp.zeros_like(q)
        qs = [jnp.where(lane < A_DIM, q, zero), jnp.where(lane >= A_DIM, q, zero)]
    else:
        qs = [q]
    blocks = [(0, n_ctx, lambda mi: sc_ref.at[mi])]
    blocks += [(n_ctx + j * LAT_BLOCK, LAT_BLOCK, lambda mi, j=j: sl_ref.at[mi * n_lat + j]) for j in range(n_lat)]

    ms = []
    for mi, qm in enumerate(qs):
        mrun = jnp.full((tq, LANE), -jnp.inf, F32)
        for off, size, buf in blocks:
            s = _dot(qm, k_ref[0, off:off + size, :], 1, 1)
            buf(mi)[...] = s
            for c in range(size // LANE):
                mrun = jnp.maximum(mrun, s[:, c * LANE:(c + 1) * LANE])
        ms.append(jnp.max(mrun, axis=-1, keepdims=True))

    outs = []
    for mi in range(len(qs)):
        acc = None
        for off, size, buf in blocks:
            v = v_ref[0, off:off + size, :]
            if diff:
                v = jnp.concatenate([v, jnp.ones_like(v)], axis=1)
            pv = _dot(_bf(jnp.exp2(buf(mi)[...] - ms[mi])), v)
            acc = pv if acc is None else acc + pv
        if diff:
            outs.append(acc[:, :LANE] / acc[:, LANE:])
        else:
            outs.append(jnp.where(lane < B_VDIM, acc / pltpu.roll(acc, B_VDIM, axis=1), 0.0))

    if diff:
        lam = (jnp.exp(jnp.sum(lq1[...] * lk1[...], axis=-1, keepdims=True))
               - jnp.exp(jnp.sum(lq2[...] * lk2[...], axis=-1, keepdims=True)) + lam_init)
        o = outs[0] - lam * outs[1]
        o_ref[0] = _bf(_rms(o, SUBLN_EPS) * g_ref[...] * (1.0 - lam_init))
    else:
        o_ref[0] = _bf(outs[0])


def _attention(q, k, v, n_ctx, latent, diff_args=None):
    B, S, _ = q.shape
    tq = ROW_TILE
    q_start = n_ctx // tq if latent else 0
    nq = S // tq - q_start if latent else n_ctx // tq
    n_keys = S if latent else n_ctx
    diff = diff_args is not None
    n_maps = 2 if diff else 1
    assert (n_keys - n_ctx) % LAT_BLOCK == 0
    n_lat = (n_keys - n_ctx) // LAT_BLOCK
    if diff:
        lq1, lk1, lq2, lk2, subln_g, lam_init = diff_args
        extra = [lq1, lk1, lq2, lk2, subln_g]
        vec = _full((1, A_DIM))
        extra_specs = [vec, vec, vec, vec, _full((1, LANE))]
    else:
        lam_init, extra, extra_specs = None, [], []
    kern = functools.partial(_attn_kernel, diff, lam_init, n_ctx, n_lat)
    qspec = pl.BlockSpec((1, tq, LANE), lambda b, h, i: (b, i + q_start, h))
    kvspec = pl.BlockSpec((1, n_keys, LANE), lambda b, h, i: (b, 0, h))
    return pl.pallas_call(
        kern,
        out_shape=jax.ShapeDtypeStruct((B, nq * tq, q.shape[-1]), BF16),
        grid=(B, q.shape[-1] // LANE, nq),
        in_specs=[qspec, kvspec, kvspec] + extra_specs,
        out_specs=pl.BlockSpec((1, tq, LANE), lambda b, h, i: (b, i, h)),
        scratch_shapes=[pltpu.VMEM((n_maps, tq, n_ctx), F32),
                        pltpu.VMEM((max(n_maps * n_lat, 1), tq, LAT_BLOCK), F32)],
        compiler_params=_params(("parallel", "parallel", "parallel")),
        name=("diff_attn" if diff else "mla_attn") + ("" if latent else "_ctx"),
    )(q, k, v, *extra)


def _rwkv_chunk_index(rev, n_ctx_chunks, n_chunks, j):
    if not rev:
        return j
    return jnp.where(j < n_ctx_chunks, n_ctx_chunks - 1 - j, n_chunks + n_ctx_chunks - 1 - j)


def _rwkv_local_kernel(n_tiles,
                       cur_ref, prv_ref, nxt_ref, mup_ref, mun_ref, w0_ref, w2_ref, a0_ref, a2_ref, g2_ref,
                       kk_ref, ka_ref, rk_ref, bd_ref,
                       qf_ref, y0f_ref, mf_ref, nf_ref, qb_ref, y0b_ref, mb_ref, nb_ref, bonus_ref, gate_ref):
    C, W = CHUNK, C_WIDTH
    R = cur_ref.shape[1]
    i = pl.program_id(1)
    seq_first = i <= 1
    seq_last = (i == 0) | (i == n_tiles - 1)

    x = cur_ref[0]
    rows = _row_iota(x.shape)
    prev_row = jnp.where(seq_first, 0.0, prv_ref[0, SUBLANE - 1:SUBLANE, :])
    next_row = jnp.where(seq_last, 0.0, nxt_ref[0, 0:1, :])
    xp = jnp.where(rows == 0, prev_row, pltpu.roll(x, 1, axis=0))
    xn = jnp.where(rows == R - 1, next_row, pltpu.roll(x, R - 1, axis=0))
    xs = x + mup_ref[...] * (xp - x) + mun_ref[...] * (xn - x)

    r, k, v = xs[:, 0:W], xs[:, W:2 * W], xs[:, 2 * W:3 * W]
    wl = jnp.tanh(xs[:, 3 * W:3 * W + 2 * C_LORA])
    al = xs[:, 3 * W + 2 * C_LORA:3 * W + 4 * C_LORA]
    gl = xs[:, 3 * W + 4 * C_LORA:]

    bd = bd_ref[...]
    kkr = k * kk_ref[...]
    kk = kkr / jnp.maximum(jnp.sqrt(_mm_hl(kkr * kkr, bd)), 1e-12)
    gate_ref[0] = _mm(_sigmoid(gl), g2_ref[...])

    rr = _row_iota((R, R))
    cr = _lane_iota((R, R))
    same_chunk = (rr >> 6) == (cr >> 6)
    rc = _row_iota((C, C))
    cc = _lane_iota((C, C))
    rc2 = _row_iota((C, 2 * C))
    cc2 = _lane_iota((C, 2 * C)) & (C - 1)
    eye = jnp.where(rc == cc, 1.0, 0.0)
    pair = [((rc >> (lvl + 1)) == (cc >> (lvl + 1))) & ((rc >> lvl) != (cc >> lvl))
            for lvl in range(int(math.log2(C)))]
    lane = _lane_iota((C, W))
    head4 = (_lane_iota((C, 4 * W)) & (W - 1)) >> 6
    rw = _row_iota((W, W))
    cw = _lane_iota((W, W))
    same_head = (rw >> 6) == (cw >> 6)
    diag_c = _row_iota((C, W)) == (lane & (C - 1))

    bonus = jnp.zeros((R, W), F32)
    outs = ((qf_ref, y0f_ref, mf_ref, nf_ref), (qb_ref, y0b_ref, mb_ref, nb_ref))
    head_masks = [(lane >= h * C_DIM) & (lane < (h + 1) * C_DIM) for h in range(C_HEADS)]
    chunks = []
    for d, rev in enumerate((False, True)):
        w = -_softplus(-(w0_ref[d:d + 1] + _mm3(wl, w2_ref[d]))) - 0.5
        lw = -jnp.exp(w)
        a_ic = _sigmoid(a0_ref[d:d + 1] + _mm3(al, a2_ref[d]))
        kd = k * (1.0 + (a_ic - 1.0) * ka_ref[...])
        avec = -kk
        bvec = kk * a_ic
        bonus = bonus + _mm_hl(r * kd * rk_ref[...], bd) * v

        tri = jnp.where(same_chunk & ((cr >= rr) if rev else (cr <= rr)), 1.0, 0.0).astype(BF16)
        l_hi = _bf(lw)
        l_md = _bf(lw - l_hi.astype(F32))
        l_lo = _bf(lw - l_hi.astype(F32) - l_md.astype(F32))
        cs = _dot(tri, l_hi) + (_dot(tri, l_md) + _dot(tri, l_lo))
        e_neg = jnp.exp(-cs)
        At_all = avec * jnp.exp(cs - lw)
        Rt_all = r * jnp.exp(cs)
        Bt_all = bvec * e_neg
        Kt_all = kd * e_neg
        for g in range(R // C):
            sl = slice(g * C, (g + 1) * C)
            csg = cs[sl]
            tot = csg[0:1, :] if rev else csg[C - 1:C, :]
            e_rem = jnp.exp(tot - csg)
            chunks.append(dict(d=d, g=g, rev=rev, sl=sl, At=At_all[sl], Rt=Rt_all[sl], v=v[sl], tot=tot,
                               BK=jnp.concatenate([Bt_all[sl], Kt_all[sl]], axis=0),
                               Bg=bvec[sl] * e_rem, Kg=kd[sl] * e_rem))
    bonus_ref[0] = bonus

    chains = [(ch, h) for ch in chunks for h in range(C_HEADS)]
    G = [_mm_nt(jnp.concatenate([jnp.where(head_masks[h], ch["At"], 0.0),
                                 jnp.where(head_masks[h], ch["Rt"], 0.0)], axis=0), ch["BK"])
         for ch, h in chains]
    strict2 = {False: cc2 < rc2, True: cc2 > rc2}
    incl2 = {False: cc2 <= rc2, True: cc2 >= rc2}
    Ltop = [jnp.where(strict2[ch["rev"]], g_[0:C], 0.0) for g_, (ch, h) in zip(G, chains)]
    Lbot = [jnp.where(incl2[ch["rev"]], g_[C:2 * C], 0.0) for g_, (ch, h) in zip(G, chains)]
    Lab = [lt[:, 0:C] for lt in Ltop]
    T = [eye + jnp.where(pair[0], la, 0.0) for la in Lab]
    for lvl in range(1, len(pair)):
        P = [_mm(t, jnp.where(pair[lvl], la, 0.0)) for t, la in zip(T, Lab)]
        T = [t + _mm(p, t) for t, p in zip(T, P)]
    zero_v = jnp.zeros((C, W), F32)
    AkV = [_mm(lt, jnp.concatenate([zero_v, ch["v"]], axis=0)) for lt, (ch, h) in zip(Ltop, chains)]
    WU = [_mm(t, jnp.concatenate([ch["At"], akv], axis=1)) for t, akv, (ch, h) in zip(T, AkV, chains)]
    QY = [_mm(lb, jnp.concatenate([wu, jnp.concatenate([zero_v, ch["v"]], axis=1)], axis=0))
          for lb, wu, (ch, h) in zip(Lbot, WU, chains)]

    for ci, ch in enumerate(chunks):
        WUQY = jnp.zeros((C, 4 * W), F32)
        for h in range(C_HEADS):
            k_ = ci * C_HEADS + h
            WUQY = jnp.where(head4 == h, jnp.concatenate([WU[k_], QY[k_]], axis=1), WUQY)
        q_ref, y0_ref, m_ref, n_ref = outs[ch["d"]]
        q_ref[0, ch["sl"], :] = _bf(ch["Rt"] + WUQY[:, 2 * W:3 * W])
        y0_ref[0, ch["sl"], :] = WUQY[:, 3 * W:]
        Mbd = jnp.where(same_head, _mm_tn(WUQY[:, 0:W], ch["Bg"]), 0.0)
        Nbd = jnp.where(same_head, _mm_tn(WUQY[:, W:2 * W], ch["Bg"]) + _mm_tn(ch["v"], ch["Kg"]), 0.0)
        Mc = Mbd[0:C] + Mbd[C:2 * C] + Mbd[2 * C:3 * C] + Mbd[3 * C:]
        m_ref[0, ch["g"]] = _bf(Mc + jnp.where(diag_c, jnp.exp(ch["tot"]), 0.0))
        n_ref[0, ch["g"]] = Nbd[0:C] + Nbd[C:2 * C] + Nbd[2 * C:3 * C] + Nbd[3 * C:]


def _rwkv_local(pr, mu_prev, mu_next, w0, w2p, a0, a2p, g2, k_k, k_a, r_k, bd):
    B, S, _ = pr.shape
    R, C, W = ROW_TILE, CHUNK, C_WIDTH
    nt, gpt = S // R, R // C
    bpt = R // SUBLANE
    nblk = S // SUBLANE
    rowspec = lambda w: pl.BlockSpec((1, R, w), lambda b, i: (b, i, 0))
    mspec = pl.BlockSpec((1, gpt, C, W), lambda b, i: (b, i, 0, 0))
    seq = lambda dt: jax.ShapeDtypeStruct((B, S, W), dt)
    mat = lambda dt: jax.ShapeDtypeStruct((B, S // C, C, W), dt)
    per_dir = [seq(BF16), seq(F32), mat(BF16), mat(F32)]
    vec = _full((1, W))
    return pl.pallas_call(
        functools.partial(_rwkv_local_kernel, nt),
        out_shape=per_dir * 2 + [seq(F32), seq(F32)],
        grid=(B, nt),
        in_specs=[rowspec(C_COLS),
                  pl.BlockSpec((1, SUBLANE, C_COLS), lambda b, i: (b, jnp.maximum(i * bpt - 1, 0), 0)),
                  pl.BlockSpec((1, SUBLANE, C_COLS), lambda b, i: (b, jnp.minimum((i + 1) * bpt, nblk - 1), 0)),
                  _full((1, C_COLS)), _full((1, C_COLS)), _full(w0.shape), _full(w2p.shape), _full(a0.shape),
                  _full(a2p.shape), _full(g2.shape), vec, vec, vec, _full(bd.shape)],
        out_specs=[rowspec(W), rowspec(W), mspec, mspec] * 2 + [rowspec(W), rowspec(W)],
        compiler_params=_params(("parallel", "parallel")),
        name="rwkv_local",
    )(pr, pr, pr, mu_prev, mu_next, w0, w2p, a0, a2p, g2, k_k, k_a, r_k, bd)


def _rwkv_state_kernel(qf_ref, y0f_ref, mf_ref, nf_ref, qb_ref, y0b_ref, mb_ref, nb_ref, yf_ref, yb_ref, s_ref):
    C, W = CHUNK, C_WIDTH
    nb = qf_ref.shape[0]

    @pl.when(pl.program_id(0) == 0)
    def _():
        s_ref[...] = jnp.zeros(s_ref.shape, F32)

    head = _lane_iota((C, W)) >> 6

    def expand(mc):
        return jnp.concatenate([jnp.where(head == h, mc, jnp.zeros_like(mc)) for h in range(C_HEADS)], axis=0)

    dirs = ((qf_ref, y0f_ref, mf_ref, nf_ref, yf_ref), (qb_ref, y0b_ref, mb_ref, nb_ref, yb_ref))
    for d, (q_ref, y0_ref, m_ref, n_ref, y_ref) in enumerate(dirs):
        for b in range(nb):
            S = _bf(s_ref[d, b])
            y_ref[b] = _dot(q_ref[b], S, 1, 1) + y0_ref[b]
            s_ref[d, b] = _dot(S, expand(m_ref[b, 0])) + expand(n_ref[b, 0])


def _rwkv_state(loc, n_ctx):
    qf = loc[0]
    B, S, W = qf.shape
    C = CHUNK
    nch, ncc = S // C, n_ctx // C
    fwd = lambda j: j
    bwd = functools.partial(_rwkv_chunk_index, True, ncc, nch)
    seqspec = lambda idx: pl.BlockSpec((B, C, W), lambda j: (0, idx(j), 0))
    matspec = lambda idx: pl.BlockSpec((B, 1, C, W), lambda j: (0, idx(j), 0, 0))
    out = jax.ShapeDtypeStruct((B, S, W), F32)
    return pl.pallas_call(
        _rwkv_state_kernel,
        out_shape=[out, out],
        grid=(nch,),
        in_specs=[seqspec(fwd), seqspec(fwd), matspec(fwd), matspec(fwd),
                  seqspec(bwd), seqspec(bwd), matspec(bwd), matspec(bwd)],
        out_specs=[seqspec(fwd), seqspec(bwd)],
        scratch_shapes=[pltpu.VMEM((2, B, W, W), F32)],
        compiler_params=_params(("arbitrary",)),
        name="rwkv_state",
    )(*loc[:8])


def _rwkv_combine_kernel(yf_ref, yb_ref, bonus_ref, gate_ref, gng_ref, gnb_ref, bdm_ref, o_ref):
    y = yf_ref[0] + yb_ref[0]
    bdm = bdm_ref[...]
    mu = _mm_hl(y, bdm)
    d = y - mu
    var = _mm_hl(d * d, bdm)
    yn = d * lax.rsqrt(var + C_GN_EPS) * gng_ref[...] + gnb_ref[...]
    o_ref[0] = _bf((yn + bonus_ref[0]) * gate_ref[0])


def _rwkv_combine(yf, yb, bonus, gate, gn_g, gn_b, bdm, t0):
    B, S, W = yf.shape
    tm = ROW_TILE
    nt = S // tm - t0
    row = pl.BlockSpec((1, tm, W), lambda b, i: (b, i + t0, 0))
    return pl.pallas_call(
        _rwkv_combine_kernel,
        out_shape=jax.ShapeDtypeStruct((B, nt * tm, W), BF16),
        grid=(B, nt),
        in_specs=[row] * 4 + [_full((1, W)), _full((1, W)), _full((W, W))],
        out_specs=pl.BlockSpec((1, tm, W), lambda b, i: (b, i, 0)),
        compiler_params=_params(("parallel", "parallel")),
        name="rwkv_combine",
    )(yf, yb, bonus, gate, gn_g, gn_b, bdm)


def _outproj_kernel(x_ref, oa_ref, ob_ref, oc_ref, mod_ref, w_ref, gpost_ref, gpre_ref, x1_ref, h2_ref):
    D = x_ref.shape[-1]
    mod = mod_ref[0, 0]
    gt1 = mod[:, 2 * D:3 * D]
    sh2, sc2 = mod[:, 3 * D:4 * D], mod[:, 4 * D:5 * D]
    nb = ob_ref.shape[-1]
    o = (_dot(oa_ref[0], w_ref[0:A_WIDTH, :])
         + _dot(ob_ref[0], w_ref[A_WIDTH:A_WIDTH + nb, :])
         + _dot(oc_ref[0], w_ref[A_WIDTH + nb:, :]))
    x1 = x_ref[0] + gt1 * (_rms(o, NORM_EPS) * gpost_ref[...])
    x1_ref[0] = x1
    h2_ref[0] = _bf(_rms(x1, NORM_EPS) * gpre_ref[...] * (1.0 + sc2) + sh2)


def _outproj(X, oa, ob, oc, modsel, w_out, g_post, g_pre, t_start):
    B, S, D = X.shape
    tm = ROW_TILE
    nt = S // tm - t_start
    row = lambda w: pl.BlockSpec((1, tm, w), lambda b, i: (b, i, 0))
    return pl.pallas_call(
        _outproj_kernel,
        out_shape=[jax.ShapeDtypeStruct((B, nt * tm, D), F32), jax.ShapeDtypeStruct((B, nt * tm, D), BF16)],
        grid=(B, nt),
        in_specs=[pl.BlockSpec((1, tm, D), lambda b, i: (b, i + t_start, 0)),
                  row(A_WIDTH), row(ob.shape[-1]), row(C_WIDTH),
                  pl.BlockSpec((1, 1, 1, modsel.shape[-1]), lambda b, i: (b, jnp.minimum(i + t_start, 1), 0, 0)),
                  _full(w_out.shape), _full((1, D)), _full((1, D))],
        out_specs=[row(D), row(D)],
        compiler_params=_params(("parallel", "parallel")),
        name="out_proj",
    )(X, oa, ob, oc, modsel, w_out, g_post, g_pre)


def _ffn_kernel(n_ctx_tiles, n_tiles, t_start, ff_tile,
                x_ref, h_ref, hp_ref, hn_ref, mod_ref, wup_ref, cw_ref, cb_ref, wdn_ref, gpost_ref, o_ref,
                act_ref):
    D = x_ref.shape[-1]
    tm = x_ref.shape[1]
    dff = wdn_ref.shape[0]
    i = pl.program_id(1) + t_start
    seq_first = (i == 0) | (i == n_ctx_tiles)
    seq_last = (i == n_ctx_tiles - 1) | (i == n_tiles - 1)
    hp = hp_ref[0]
    hn = hn_ref[0]
    hp = jnp.where(seq_first, jnp.zeros_like(hp), hp)
    hn = jnp.where(seq_last, jnp.zeros_like(hn), hn)
    hext = jnp.concatenate([hp, h_ref[0], hn], axis=0)
    halo = hp.shape[0]
    for f in range(dff // ff_tile):
        parts = []
        for base in (0, dff):
            lo = base + f * ff_tile
            u = _dot(hext, wup_ref[:, lo:lo + ff_tile])
            cw = cw_ref[:, lo:lo + ff_tile]
            n = u.shape[0]
            up = pltpu.roll(u, 1, axis=0)[halo:halo + tm]
            un = pltpu.roll(u, n - 1, axis=0)[halo:halo + tm]
            parts.append(up * cw[0:1] + u[halo:halo + tm] * cw[1:2] + un * cw[2:3]
                         + cb_ref[:, lo:lo + ff_tile])
        act_ref[:, f * ff_tile:(f + 1) * ff_tile] = _bf(_silu(parts[0]) * parts[1])
    acc = _dot(act_ref[...], wdn_ref[...])
    gt2 = mod_ref[0, 0][:, 5 * D:6 * D]
    o_ref[0] = x_ref[0] + gt2 * (_rms(acc, NORM_EPS) * gpost_ref[...])


def _ffn(X1, H2, modsel, w_up, conv_w, conv_b, w_dn, g_post, n_ctx, t_start):
    B, S, D = X1.shape
    tm = ROW_TILE
    halo = 16
    nt = S // tm
    hb = tm // halo
    nhb = S // halo
    kern = functools.partial(_ffn_kernel, n_ctx // tm, nt + t_start, t_start, 256)
    row = pl.BlockSpec((1, tm, D), lambda b, i: (b, i, 0))
    return pl.pallas_call(
        kern,
        out_shape=jax.ShapeDtypeStruct((B, S, D), F32),
        grid=(B, nt),
        in_specs=[row, row,
                  pl.BlockSpec((1, halo, D), lambda b, i: (b, jnp.maximum(i * hb - 1, 0), 0)),
                  pl.BlockSpec((1, halo, D), lambda b, i: (b, jnp.minimum((i + 1) * hb, nhb - 1), 0)),
                  pl.BlockSpec((1, 1, 1, modsel.shape[-1]), lambda b, i: (b, jnp.minimum(i + t_start, 1), 0, 0)),
                  _full(w_up.shape), _full(conv_w.shape), _full(conv_b.shape), _full(w_dn.shape), _full((1, D))],
        out_specs=row,
        scratch_shapes=[pltpu.VMEM((tm, w_dn.shape[0]), BF16)],
        compiler_params=_params(("parallel", "parallel")),
        name="conv_ffn",
    )(X1, H2, H2, H2, modsel, w_up, conv_w, conv_b, w_dn, g_post)


def _rope_tables(S, n_ctx, dim, lane_lo):
    nf = dim // 4
    t = jnp.arange(S - n_ctx, dtype=jnp.int32)
    rows = (t // GRID_W).astype(F32)
    cols = (t % GRID_W).astype(F32)
    inv = ROPE_BASE ** (-jnp.arange(nf, dtype=F32) / nf)
    ar, ac = rows[:, None] * inv, cols[:, None] * inv
    cos = jnp.concatenate([jnp.cos(ar), jnp.cos(ar), jnp.cos(ac), jnp.cos(ac)], axis=-1)
    sin = jnp.concatenate([-jnp.sin(ar), jnp.sin(ar), -jnp.sin(ac), jnp.sin(ac)], axis=-1)
    if lane_lo == 0:
        reps = LANE // dim
        cos, sin = jnp.tile(cos, (1, reps)), jnp.tile(sin, (1, reps))
    else:
        pad = ((0, 0), (lane_lo, LANE - lane_lo - dim))
        cos = jnp.pad(cos, pad, constant_values=1.0)
        sin = jnp.pad(sin, pad)
    cos = jnp.concatenate([jnp.ones((n_ctx, LANE), F32), cos], axis=0)
    sin = jnp.concatenate([jnp.zeros((n_ctx, LANE), F32), sin], axis=0)
    return cos, sin


def _layout_w_in(w):
    D = w.shape[0]
    a = w[:, :3 * A_WIDTH]
    o = 3 * A_WIDTH
    cq = w[:, o:o + B_Q_RANK]
    ckv = w[:, o + B_Q_RANK:o + B_Q_RANK + B_KV_RANK]
    kr = w[:, o + B_Q_RANK + B_KV_RANK:o + B_Q_RANK + B_KV_RANK + B_ROPE]
    c = w[:, o + B_Q_RANK + B_KV_RANK + B_ROPE:]
    z = jnp.zeros((D, 2 * LANE - B_Q_RANK - B_ROPE), w.dtype)
    return _bf(jnp.concatenate([a, cq, kr, z, ckv, c], axis=1))


def _layout_wq(w):
    hd = B_NOPE + B_ROPE
    w = w.reshape(B_Q_RANK, B_HEADS, hd)
    w = jnp.pad(w, ((0, 2 * LANE - B_Q_RANK), (0, 0), (0, LANE - hd)))
    return _bf(w.reshape(2 * LANE, B_HEADS * LANE))


def _layout_wkv(w):
    w = w.reshape(B_KV_RANK, B_HEADS, B_NOPE + B_VDIM)
    pad = lambda t: jnp.pad(t, ((0, 0), (0, 0), (0, LANE - t.shape[-1]))).reshape(B_KV_RANK, B_HEADS * LANE)
    return _bf(jnp.concatenate([pad(w[:, :, :B_NOPE]), pad(w[:, :, B_NOPE:])], axis=1))


def _layout_w_out(w):
    D = w.shape[1]
    wb = w[A_WIDTH:A_WIDTH + B_WIDTH].reshape(B_HEADS, B_VDIM, D)
    wb = jnp.pad(wb, ((0, 0), (0, LANE - B_VDIM), (0, 0))).reshape(B_HEADS * LANE, D)
    return _bf(jnp.concatenate([w[:A_WIDTH], wb, w[A_WIDTH + B_WIDTH:]], axis=0))


def _layout_lora(w, d):
    z = jnp.zeros_like(w)
    return jnp.concatenate([z, w] if d else [w, z], axis=0)


def _block_diag_ones(scale):
    i = np.arange(C_WIDTH) // C_DIM
    return jnp.asarray((i[:, None] == i[None, :]).astype(np.float32) * scale, dtype=BF16)


def kernel(x, c, ctx, c_ctx, ada_w, ada_b, mix_pre_g, mix_post_g, ffn_pre_g, ffn_post_g, w_in, w_out, lam_q1, lam_k1, lam_q2, lam_k2, a_subln_g, b_q_norm_g, b_w_q_up, b_kv_norm_g, b_w_kv_up, c_mu_prev, c_mu_next, c_w0, c_w2, c_a0, c_a2, c_g2, c_k_k, c_k_a, c_r_k, c_gn_g, c_gn_b, ffn_w_up, ffn_conv_w, ffn_conv_b, ffn_w_down):
    B, T, D = x.shape
    n_ctx = ctx.shape[1]
    S = n_ctx + T
    L = ada_w.shape[0]
    assert n_ctx == ROW_TILE and T % ROW_TILE == 0 and D % LANE == 0

    X = jnp.concatenate([ctx, x], axis=1)
    cond = jnp.concatenate([c, c_ctx[None, :], jnp.zeros((SUBLANE - B - 1, D), F32)], axis=0)
    mod = _modulation(cond, ada_w, ada_b)
    ropeA = _rope_tables(S, n_ctx, A_DIM, 0)
    ropeB = _rope_tables(S, n_ctx, B_ROPE, B_NOPE)
    bd1 = _block_diag_ones(1.0)
    bdm = _block_diag_ones(1.0 / C_DIM)
    row = lambda v: v.reshape(1, -1)

    for i in range(L):
        last = i == L - 1
        t0 = 1 if last else 0
        lam_init = 0.8 - 0.6 * math.exp(-0.3 * i)
        modsel = jnp.stack([jnp.broadcast_to(mod[i, B], (B, 6 * D)), mod[i, :B]], axis=1)[:, :, None, :]
        gq = jnp.pad(b_q_norm_g[i], (0, 2 * LANE - B_Q_RANK)).reshape(1, -1)
        qa, ka, va, qb, kb, vb, pr = _inproj(
            X, modsel, row(mix_pre_g[i]), _layout_w_in(w_in[i]), ropeA, ropeB,
            gq, row(b_kv_norm_g[i]), _layout_wq(b_w_q_up[i]), _layout_wkv(b_w_kv_up[i]))
        dargs = (row(lam_q1[i]), row(lam_k1[i]), row(lam_q2[i]), row(lam_k2[i]), row(a_subln_g[i]), lam_init)
        oa = _attention(qa, ka, va, n_ctx, True, dargs)
        ob = _attention(qb, kb, vb, n_ctx, True)
        if not last:
            oa = jnp.concatenate([_attention(qa, ka, va, n_ctx, False, dargs), oa], axis=1)
            ob = jnp.concatenate([_attention(qb, kb, vb, n_ctx, False), ob], axis=1)
        loc = _rwkv_local(pr, row(c_mu_prev[i]), row(c_mu_next[i]),
                          c_w0[i], jnp.stack([_layout_lora(c_w2[i, d], d) for d in (0, 1)]),
                          c_a0[i], jnp.stack([_layout_lora(c_a2[i, d], d) for d in (0, 1)]),
                          c_g2[i], row(c_k_k[i]), row(c_k_a[i]), row(c_r_k[i]), bd1)
        yf, yb = _rwkv_state(loc, n_ctx)
        oc = _rwkv_combine(yf, yb, loc[8], loc[9], row(c_gn_g[i]), row(c_gn_b[i]), bdm, t0)
        X1, H2 = _outproj(X, oa, ob, oc, modsel, _layout_w_out(w_out[i]), row(mix_post_g[i]), row(ffn_pre_g[i]), t0)
        X = _ffn(X1, H2, modsel, _bf(ffn_w_up[i]), ffn_conv_w[i], row(ffn_conv_b[i]), _bf(ffn_w_down[i]),
                 row(ffn_post_g[i]), n_ctx, t0)
    return X
```

```python
import functools
import math

import jax
import jax.numpy as jnp
import numpy as np
from jax import lax
from jax.experimental import pallas as pl
from jax.experimental.pallas import tpu as pltpu

F32 = jnp.float32
BF16 = jnp.bfloat16

GRID_W = 64
ROPE_BASE = 10000.0
NORM_EPS = 1e-6
SUBLN_EPS = 1e-5
A_HEADS, A_DIM = 4, 64
A_WIDTH = A_HEADS * 2 * A_DIM
B_HEADS, B_NOPE, B_ROPE, B_VDIM = 4, 64, 32, 64
B_Q_RANK, B_KV_RANK = 192, 128
B_WIDTH = B_HEADS * B_VDIM
C_HEADS, C_DIM = 4, 64
C_WIDTH = C_HEADS * C_DIM
C_LORA = 64
C_GATE_LORA = 128
C_GN_EPS = 64e-5
C_COLS = 3 * C_WIDTH + 4 * C_LORA + C_GATE_LORA

LANE = 128
SUBLANE = 8
ROW_TILE = 256
Q_TILE = 512
LAT_BLOCK = 512
LOG2E = 1.4426950408889634
CHUNK = 64
VMEM_LIMIT = 56 * 1024 * 1024

P_AQ, P_AK, P_AV = 0, 512, 1024
P_BQ = 1536
P_BKV = 1792
P_C = 1920
P_TOTAL = 3072


def _bf(x):
    return x.astype(BF16)


def _dot(a, b, ca=1, cb=0):
    return lax.dot_general(a, b, (((ca,), (cb,)), ((), ())), preferred_element_type=F32)


def _mm(a, b):
    return _dot(_bf(a), _bf(b))


def _mm_nt(a, b):
    return _dot(_bf(a), _bf(b), 1, 1)


def _mm_tn(a, b):
    return _dot(_bf(a.T), _bf(b))


def _split2(x):
    hi = _bf(x)
    lo = _bf(x - hi.astype(F32))
    return hi, lo


def _mm_hl(a, b):
    hi, lo = _split2(a)
    return _dot(hi, b) + _dot(lo, b)


def _mm3(a, b):
    ah, al = _split2(a)
    bh, bl = _split2(b)
    return _dot(ah, bh) + (_dot(ah, bl) + _dot(al, bh))


def _rms(x, eps):
    return x * lax.rsqrt(jnp.mean(x * x, axis=-1, keepdims=True) + eps)


def _sigmoid(x):
    return 1.0 / (1.0 + jnp.exp(-x))


def _silu(x):
    return x * _sigmoid(x)


def _softplus(x):
    return jnp.maximum(x, 0.0) + jnp.log(1.0 + jnp.exp(-jnp.abs(x)))


def _lane_iota(shape):
    return lax.broadcasted_iota(jnp.int32, shape, len(shape) - 1)


def _row_iota(shape):
    return lax.broadcasted_iota(jnp.int32, shape, len(shape) - 2)


def _rope(x, cos, sin, half):
    n = x.shape[-1]
    up = pltpu.roll(x, n - half, axis=1)
    dn = pltpu.roll(x, half, axis=1)
    first = (_lane_iota(x.shape) & half) == 0
    return x * cos + jnp.where(first, up, dn) * sin


def _params(sem):
    return pltpu.CompilerParams(dimension_semantics=sem, vmem_limit_bytes=VMEM_LIMIT)


def _full(shape):
    nd = len(shape)
    return pl.BlockSpec(shape, lambda *_: (0,) * nd)


def _mod_kernel(c_ref, w_ref, b_ref, o_ref):
    act = _silu(c_ref[...])
    o_ref[0] = _mm3(act, w_ref[0]) + b_ref[0]


def _modulation(cond, ada_w, ada_b):
    L, D, N = ada_w.shape
    R = cond.shape[0]
    tn = 1024
    return pl.pallas_call(
        _mod_kernel,
        out_shape=jax.ShapeDtypeStruct((L, R, N), F32),
        grid=(L, N // tn),
        in_specs=[pl.BlockSpec((R, D), lambda l, j: (0, 0)),
                  pl.BlockSpec((1, D, tn), lambda l, j: (l, 0, j)),
                  pl.BlockSpec((1, 1, tn), lambda l, j: (l, 0, j))],
        out_specs=pl.BlockSpec((1, R, tn), lambda l, j: (l, 0, j)),
        compiler_params=_params(("parallel", "parallel")),
        name="adaln_mod",
    )(cond, ada_w, ada_b.reshape(L, 1, N))


def _inproj_kernel(x_ref, mod_ref, g_ref, w_ref, ca_ref, sa_ref, cb_ref, sb_ref,
                   gq_ref, gkv_ref, wq_ref, wkv_ref,
                   qa_ref, ka_ref, va_ref, qb_ref, kb_ref, vb_ref, pr_ref):
    D = x_ref.shape[-1]
    x = x_ref[0]
    mod = mod_ref[0, 0]
    sh1, sc1 = mod[:, 0:D], mod[:, D:2 * D]
    h = _bf(_rms(x, NORM_EPS) * g_ref[...] * (1.0 + sc1) + sh1)

    def proj(lo, hi):
        return _dot(h, w_ref[:, lo:hi])

    ca, sa = ca_ref[...], sa_ref[...]
    cb, sb = cb_ref[...], sb_ref[...]

    pq = proj(P_AQ, P_AK)
    pk = proj(P_AK, P_AV)
    for j in range(A_WIDTH // LANE):
        sl = slice(j * LANE, (j + 1) * LANE)
        qa_ref[0, :, sl] = _bf(_rope(pq[:, sl], ca, sa, A_DIM // 4) * (A_DIM ** -0.5 * LOG2E))
        ka_ref[0, :, sl] = _bf(_rope(pk[:, sl], ca, sa, A_DIM // 4))
    va_ref[0] = _bf(proj(P_AV, P_BQ))

    pbq = proj(P_BQ, P_BKV)
    lane = _lane_iota(pbq.shape)
    cq = jnp.where(lane < B_Q_RANK, pbq, 0.0)
    cqn = cq * lax.rsqrt(jnp.sum(cq * cq, axis=-1, keepdims=True) * (1.0 / B_Q_RANK) + NORM_EPS) * gq_ref[...]
    qb = _mm(cqn, wq_ref[...]) * ((B_NOPE + B_ROPE) ** -0.5 * LOG2E)
    ckv = proj(P_BKV, P_C)
    ckvn = _rms(ckv, NORM_EPS) * gkv_ref[...]
    kv = _mm(ckvn, wkv_ref[...])
    krb = pbq[:, LANE:2 * LANE]
    l1 = _lane_iota(krb.shape)
    kr = _rope(jnp.where((l1 >= B_NOPE) & (l1 < B_NOPE + B_ROPE), krb, 0.0), cb, sb, B_ROPE // 4)
    for j in range(B_HEADS):
        sl = slice(j * LANE, (j + 1) * LANE)
        qb_ref[0, :, sl] = _bf(_rope(qb[:, sl], cb, sb, B_ROPE // 4))
        kb_ref[0, :, sl] = _bf(kv[:, sl] + kr)
        vh = kv[:, B_HEADS * LANE + j * LANE:B_HEADS * LANE + (j + 1) * LANE]
        vb_ref[0, :, sl] = _bf(jnp.where(l1 < B_VDIM, vh, 1.0))

    pr_ref[0] = proj(P_C, P_TOTAL)


def _inproj(X, modsel, g, w_in, ropeA, ropeB, gq, gkv, wq, wkv):
    B, S, D = X.shape
    tm = ROW_TILE
    row = lambda w: pl.BlockSpec((1, tm, w), lambda b, i: (b, i, 0))
    tab = pl.BlockSpec((tm, LANE), lambda b, i: (i, 0))
    outs = [jax.ShapeDtypeStruct((B, S, 512), BF16)] * 6 + [jax.ShapeDtypeStruct((B, S, C_COLS), F32)]
    return pl.pallas_call(
        _inproj_kernel,
        out_shape=outs,
        grid=(B, S // tm),
        in_specs=[row(D),
                  pl.BlockSpec((1, 1, 1, modsel.shape[-1]), lambda b, i: (b, jnp.minimum(i, 1), 0, 0)),
                  _full((1, D)), _full(w_in.shape), tab, tab, tab, tab,
                  _full(gq.shape), _full(gkv.shape), _full(wq.shape), _full(wkv.shape)],
        out_specs=[row(512)] * 6 + [row(C_COLS)],
        compiler_params=_params(("parallel", "parallel")),
        name="in_proj",
    )(X, modsel, g, w_in, ropeA[0], ropeA[1], ropeB[0], ropeB[1], gq, gkv, wq, wkv)


def _attn_kernel(diff, lam_init, n_ctx, n_lat, q_ref, k_ref, v_ref, *rest):
    if diff:
        lq1, lk1, lq2, lk2, g_ref, o_ref, sc_ref, sl_ref = rest
    else:
        o_ref, sc_ref, sl_ref = rest
    tq = q_ref.shape[1]
    lane = _lane_iota((tq, LANE))
    if diff:
        q = q_ref[0]
        zero = jnp.zeros_like(q)
        maps = [(jnp.where(lane < A_DIM, q, zero), slice(0, LANE)), (jnp.where(lane >= A_DIM, q, zero), slice(0, LANE))]
    else:
        maps = [(q_ref[0, :, h * LANE:(h + 1) * LANE], slice(h * LANE, (h + 1) * LANE))
                for h in range(q_ref.shape[2] // LANE)]
    blocks = [(0, n_ctx, lambda mi: sc_ref.at[mi])]
    blocks += [(n_ctx + j * LAT_BLOCK, LAT_BLOCK, lambda mi, j=j: sl_ref.at[mi * n_lat + j]) for j in range(n_lat)]

    ms = []
    for mi, (qm, ksl) in enumerate(maps):
        mrun = jnp.full((tq, LANE), -jnp.inf, F32)
        for off, size, buf in blocks:
            s = _dot(qm, k_ref[0, off:off + size, ksl], 1, 1)
            buf(mi)[...] = s
            for c in range(size // LANE):
                mrun = jnp.maximum(mrun, s[:, c * LANE:(c + 1) * LANE])
        ms.append(jnp.max(mrun, axis=-1, keepdims=True))

    outs = []
    for mi, (qm, ksl) in enumerate(maps):
        acc = None
        for off, size, buf in blocks:
            v = v_ref[0, off:off + size, ksl]
            if diff:
                v = jnp.concatenate([v, jnp.ones_like(v)], axis=1)
            pv = _dot(_bf(jnp.exp2(buf(mi)[...] - ms[mi])), v)
            acc = pv if acc is None else acc + pv
        if diff:
            outs.append(acc[:, :LANE] / acc[:, LANE:])
        else:
            outs.append(jnp.where(lane < B_VDIM, acc / pltpu.roll(acc, B_VDIM, axis=1), 0.0))

    if diff:
        lam = (jnp.exp(jnp.sum(lq1[...] * lk1[...], axis=-1, keepdims=True))
               - jnp.exp(jnp.sum(lq2[...] * lk2[...], axis=-1, keepdims=True)) + lam_init)
        o = outs[0] - lam * outs[1]
        o_ref[0] = _bf(_rms(o, SUBLN_EPS) * g_ref[...] * (1.0 - lam_init))
    else:
        for h, o in enumerate(outs):
            o_ref[0, :, h * LANE:(h + 1) * LANE] = _bf(o)


def _attention(q, k, v, n_ctx, latent, diff_args=None):
    B, S, _ = k.shape
    tq = Q_TILE if latent else ROW_TILE
    if latent:
        q = q[:, n_ctx:]
    nq = q.shape[1] // tq if latent else n_ctx // tq
    n_keys = S if latent else n_ctx
    diff = diff_args is not None
    n_maps = 2
    bw = LANE if diff else 2 * LANE
    assert (n_keys - n_ctx) % LAT_BLOCK == 0
    n_lat = (n_keys - n_ctx) // LAT_BLOCK
    if diff:
        lq1, lk1, lq2, lk2, subln_g, lam_init = diff_args
        extra = [lq1, lk1, lq2, lk2, subln_g]
        vec = _full((1, A_DIM))
        extra_specs = [vec, vec, vec, vec, _full((1, LANE))]
    else:
        lam_init, extra, extra_specs = None, [], []
    kern = functools.partial(_attn_kernel, diff, lam_init, n_ctx, n_lat)
    qspec = pl.BlockSpec((1, tq, bw), lambda b, h, i: (b, i, h))
    kvspec = pl.BlockSpec((1, n_keys, bw), lambda b, h, i: (b, 0, h))
    return pl.pallas_call(
        kern,
        out_shape=jax.ShapeDtypeStruct((B, nq * tq, q.shape[-1]), BF16),
        grid=(B, q.shape[-1] // bw, nq),
        in_specs=[qspec, kvspec, kvspec] + extra_specs,
        out_specs=qspec,
        scratch_shapes=[pltpu.VMEM((n_maps, tq, n_ctx), F32),
                        pltpu.VMEM((max(n_maps * n_lat, 1), tq, LAT_BLOCK), F32)],
        compiler_params=_params(("parallel", "parallel", "parallel")),
        name=("diff_attn" if diff else "mla_attn") + ("" if latent else "_ctx"),
    )(q, k, v, *extra)


def _rwkv_chunk_index(rev, n_ctx_chunks, n_chunks, j):
    if not rev:
        return j
    return jnp.where(j < n_ctx_chunks, n_ctx_chunks - 1 - j, n_chunks + n_ctx_chunks - 1 - j)


def _rwkv_local_kernel(n_tiles,
                       cur_ref, prv_ref, nxt_ref, mup_ref, mun_ref, w0_ref, w2_ref, a0_ref, a2_ref, g2_ref,
                       kk_ref, ka_ref, rk_ref, bd_ref,
                       qf_ref, y0f_ref, mf_ref, nf_ref, qb_ref, y0b_ref, mb_ref, nb_ref, bonus_ref, gate_ref):
    C, W = CHUNK, C_WIDTH
    NB, R = cur_ref.shape[0], cur_ref.shape[1]
    i = pl.program_id(1)
    seq_first = i <= 1
    seq_last = (i == 0) | (i == n_tiles - 1)

    rr = _row_iota((R, R))
    cr = _lane_iota((R, R))
    same_chunk = (rr >> 6) == (cr >> 6)
    rows = _row_iota((R, C_COLS))
    rc = _row_iota((C, W))
    cc = _lane_iota((C, W)) & (C - 1)
    eye = jnp.where(rc == cc, 1.0, 0.0)
    pair = [((rc >> (lvl + 1)) == (cc >> (lvl + 1))) & ((rc >> lvl) != (cc >> lvl))
            for lvl in range(int(math.log2(C)))]
    same_head = (_row_iota((W, W)) >> 6) == (_lane_iota((W, W)) >> 6)
    bd = bd_ref[...]
    outs = ((qf_ref, y0f_ref, mf_ref, nf_ref), (qb_ref, y0b_ref, mb_ref, nb_ref))

    chunks = []
    for bi in range(NB):
        x = cur_ref[bi]
        prev_row = jnp.where(seq_first, 0.0, prv_ref[bi, SUBLANE - 1:SUBLANE, :])
        next_row = jnp.where(seq_last, 0.0, nxt_ref[bi, 0:1, :])
        xp = jnp.where(rows == 0, prev_row, pltpu.roll(x, 1, axis=0))
        xn = jnp.where(rows == R - 1, next_row, pltpu.roll(x, R - 1, axis=0))
        xs = x + mup_ref[...] * (xp - x) + mun_ref[...] * (xn - x)

        r, k, v = xs[:, 0:W], xs[:, W:2 * W], xs[:, 2 * W:3 * W]
        wl = jnp.tanh(xs[:, 3 * W:3 * W + 2 * C_LORA])
        al = xs[:, 3 * W + 2 * C_LORA:3 * W + 4 * C_LORA]
        gl = xs[:, 3 * W + 4 * C_LORA:]
        kkr = k * kk_ref[...]
        kk = kkr / jnp.maximum(jnp.sqrt(_mm_hl(kkr * kkr, bd)), 1e-12)
        gate_ref[bi] = _mm(_sigmoid(gl), g2_ref[...])

        bonus = jnp.zeros((R, W), F32)
        for d, rev in enumerate((False, True)):
            w = -_softplus(-(w0_ref[d:d + 1] + _mm3(wl, w2_ref[d]))) - 0.5
            lw = -jnp.exp(w)
            a_ic = _sigmoid(a0_ref[d:d + 1] + _mm3(al, a2_ref[d]))
            kd = k * (1.0 + (a_ic - 1.0) * ka_ref[...])
            avec = -kk
            bvec = kk * a_ic
            bonus = bonus + _mm_hl(r * kd * rk_ref[...], bd) * v

            tri = jnp.where(same_chunk & ((cr >= rr) if rev else (cr <= rr)), 1.0, 0.0).astype(BF16)
            l_hi = _bf(lw)
            l_md = _bf(lw - l_hi.astype(F32))
            l_lo = _bf(lw - l_hi.astype(F32) - l_md.astype(F32))
            cs = _dot(tri, l_hi) + (_dot(tri, l_md) + _dot(tri, l_lo))
            e_neg = jnp.exp(-cs)
            At_all = avec * jnp.exp(cs - lw)
            Rt_all = r * jnp.exp(cs)
            Bt_all = bvec * e_neg
            Kt_all = kd * e_neg
            for g in range(R // C):
                sl = slice(g * C, (g + 1) * C)
                csg = cs[sl]
                tot = csg[0:1, :] if rev else csg[C - 1:C, :]
                e_rem = jnp.exp(tot - csg)
                chunks.append(dict(bi=bi, d=d, g=g, rev=rev, sl=sl, At=At_all[sl], Rt=Rt_all[sl], v=v[sl], tot=tot,
                                   Bt=Bt_all[sl], Kt=Kt_all[sl], Bg=bvec[sl] * e_rem, Kg=kd[sl] * e_rem))
        bonus_ref[bi] = bonus

    def heads_bd(x):
        xb = _bf(x)
        return jnp.where(same_head, jnp.concatenate([xb] * C_HEADS, axis=0), jnp.zeros((W, W), BF16))

    def each(fn, *lists):
        return [fn(*args) for args in zip(*lists)]

    X = [jnp.concatenate([ch["At"], ch["Rt"]], axis=0) for ch in chunks]
    GB = each(lambda x, ch: _dot(_bf(x), heads_bd(ch["Bt"]), 1, 1), X, chunks)
    GK = each(lambda x, ch: _dot(_bf(x), heads_bd(ch["Kt"]), 1, 1), X, chunks)
    strict = {False: cc < rc, True: cc > rc}
    incl = {False: cc <= rc, True: cc >= rc}
    Aab = each(lambda g_, ch: jnp.where(strict[ch["rev"]], g_[0:C], 0.0), GB, chunks)
    Arb = each(lambda g_, ch: jnp.where(incl[ch["rev"]], g_[C:], 0.0), GB, chunks)
    Aak = each(lambda g_, ch: jnp.where(strict[ch["rev"]], g_[0:C], 0.0), GK, chunks)
    Ark = each(lambda g_, ch: jnp.where(incl[ch["rev"]], g_[C:], 0.0), GK, chunks)
    T = [eye + jnp.where(pair[0], a, 0.0) for a in Aab]
    for lvl in range(1, len(pair)):
        P = each(lambda t, a: _dot(_bf(t), heads_bd(jnp.where(pair[lvl], a, 0.0))), T, Aab)
        T = each(lambda t, p: t + _dot(_bf(p), heads_bd(t)), T, P)
    Vbd = [heads_bd(ch["v"]) for ch in chunks]
    AkV = each(lambda a, vb: _dot(_bf(a), vb), Aak, Vbd)
    Wt = each(lambda t, ch: _dot(_bf(t), heads_bd(ch["At"])), T, chunks)
    U0 = each(lambda t, akv: _dot(_bf(t), heads_bd(akv)), T, AkV)
    Q = each(lambda a, wt, ch: ch["Rt"] + _dot(_bf(a), heads_bd(wt)), Arb, Wt, chunks)
    Y0 = each(lambda arb, u0, ark, vb: _dot(_bf(arb), heads_bd(u0)) + _dot(_bf(ark), vb), Arb, U0, Ark, Vbd)

    for ch, wt, u0, q, y0 in zip(chunks, Wt, U0, Q, Y0):
        q_ref, y0_ref, m_ref, n_ref = outs[ch["d"]]
        q_ref[ch["bi"], ch["sl"], :] = _bf(q)
        y0_ref[ch["bi"], ch["sl"], :] = y0
        Mbd = jnp.where(same_head, _mm_tn(wt, ch["Bg"]), 0.0)
        Nbd = jnp.where(same_head, _mm_tn(u0, ch["Bg"]) + _mm_tn(ch["v"], ch["Kg"]), 0.0)
        Mc = Mbd[0:C] + Mbd[C:2 * C] + Mbd[2 * C:3 * C] + Mbd[3 * C:]
        m_ref[ch["bi"], ch["g"]] = _bf(Mc + jnp.where(rc == cc, jnp.exp(ch["tot"]), 0.0))
        n_ref[ch["bi"], ch["g"]] = Nbd[0:C] + Nbd[C:2 * C] + Nbd[2 * C:3 * C] + Nbd[3 * C:]


def _rwkv_local(pr, mu_prev, mu_next, w0, w2p, a0, a2p, g2, k_k, k_a, r_k, bd):
    B, S, _ = pr.shape
    R, C, W = ROW_TILE, CHUNK, C_WIDTH
    NB = 2 if B % 2 == 0 else 1
    nt, gpt = S // R, R // C
    bpt = R // SUBLANE
    nblk = S // SUBLANE
    rowspec = lambda w: pl.BlockSpec((NB, R, w), lambda b, i: (b, i, 0))
    mspec = pl.BlockSpec((NB, gpt, C, W), lambda b, i: (b, i, 0, 0))
    seq = lambda dt: jax.ShapeDtypeStruct((B, S, W), dt)
    mat = lambda dt: jax.ShapeDtypeStruct((B, S // C, C, W), dt)
    per_dir = [seq(BF16), seq(F32), mat(BF16), mat(F32)]
    vec = _full((1, W))
    return pl.pallas_call(
        functools.partial(_rwkv_local_kernel, nt),
        out_shape=per_dir * 2 + [seq(F32), seq(F32)],
        grid=(B // NB, nt),
        in_specs=[rowspec(C_COLS),
                  pl.BlockSpec((NB, SUBLANE, C_COLS), lambda b, i: (b, jnp.maximum(i * bpt - 1, 0), 0)),
                  pl.BlockSpec((NB, SUBLANE, C_COLS), lambda b, i: (b, jnp.minimum((i + 1) * bpt, nblk - 1), 0)),
                  _full((1, C_COLS)), _full((1, C_COLS)), _full(w0.shape), _full(w2p.shape), _full(a0.shape),
                  _full(a2p.shape), _full(g2.shape), vec, vec, vec, _full(bd.shape)],
        out_specs=[rowspec(W), rowspec(W), mspec, mspec] * 2 + [rowspec(W), rowspec(W)],
        compiler_params=_params(("parallel", "parallel")),
        name="rwkv_local",
    )(pr, pr, pr, mu_prev, mu_next, w0, w2p, a0, a2p, g2, k_k, k_a, r_k, bd)


def _rwkv_state_kernel(qf_ref, y0f_ref, mf_ref, nf_ref, qb_ref, y0b_ref, mb_ref, nb_ref, yf_ref, yb_ref, s_ref):
    C, W = CHUNK, C_WIDTH
    nb = qf_ref.shape[0]

    @pl.when(pl.program_id(0) == 0)
    def _():
        s_ref[...] = jnp.zeros(s_ref.shape, F32)

    head = _lane_iota((C, W)) >> 6

    def expand(mc):
        return jnp.concatenate([jnp.where(head == h, mc, jnp.zeros_like(mc)) for h in range(C_HEADS)], axis=0)

    dirs = ((qf_ref, y0f_ref, mf_ref, nf_ref, yf_ref), (qb_ref, y0b_ref, mb_ref, nb_ref, yb_ref))
    for d, (q_ref, y0_ref, m_ref, n_ref, y_ref) in enumerate(dirs):
        for b in range(nb):
            S = _bf(s_ref[d, b])
            y_ref[b] = _dot(q_ref[b], S, 1, 1) + y0_ref[b]
            s_ref[d, b] = _dot(S, expand(m_ref[b, 0])) + expand(n_ref[b, 0])


def _rwkv_state(loc, n_ctx):
    qf = loc[0]
    B, S, W = qf.shape
    C = CHUNK
    nch, ncc = S // C, n_ctx // C
    fwd = lambda j: j
    bwd = functools.partial(_rwkv_chunk_index, True, ncc, nch)
    seqspec = lambda idx: pl.BlockSpec((B, C, W), lambda j: (0, idx(j), 0))
    matspec = lambda idx: pl.BlockSpec((B, 1, C, W), lambda j: (0, idx(j), 0, 0))
    out = jax.ShapeDtypeStruct((B, S, W), F32)
    return pl.pallas_call(
        _rwkv_state_kernel,
        out_shape=[out, out],
        grid=(nch,),
        in_specs=[seqspec(fwd), seqspec(fwd), matspec(fwd), matspec(fwd),
                  seqspec(bwd), seqspec(bwd), matspec(bwd), matspec(bwd)],
        out_specs=[seqspec(fwd), seqspec(bwd)],
        scratch_shapes=[pltpu.VMEM((2, B, W, W), F32)],
        compiler_params=_params(("arbitrary",)),
        name="rwkv_state",
    )(*loc[:8])


def _outproj_kernel(x_ref, oa_ref, ob_ref, yf_ref, yb_ref, bonus_ref, gate_ref, gng_ref, gnb_ref, bdm_ref,
                    mod_ref, w_ref, gpost_ref, gpre_ref, x1_ref, h2_ref):
    D = x_ref.shape[-1]
    mod = mod_ref[0, 0]
    gt1 = mod[:, 2 * D:3 * D]
    sh2, sc2 = mod[:, 3 * D:4 * D], mod[:, 4 * D:5 * D]
    y = yf_ref[0] + yb_ref[0]
    bdm = bdm_ref[...]
    mu = _mm_hl(y, bdm)
    dev = y - mu
    var = _mm_hl(dev * dev, bdm)
    yn = dev * lax.rsqrt(var + C_GN_EPS) * gng_ref[...] + gnb_ref[...]
    oc = _bf((yn + bonus_ref[0]) * gate_ref[0])
    nb = ob_ref.shape[-1]
    o = (_dot(oa_ref[0], w_ref[0:A_WIDTH, :])
         + _dot(ob_ref[0], w_ref[A_WIDTH:A_WIDTH + nb, :])
         + _dot(oc, w_ref[A_WIDTH + nb:, :]))
    x1 = x_ref[0] + gt1 * (_rms(o, NORM_EPS) * gpost_ref[...])
    x1_ref[0] = x1
    h2_ref[0] = _bf(_rms(x1, NORM_EPS) * gpre_ref[...] * (1.0 + sc2) + sh2)


def _outproj(X, oa, ob, rwkv, gn_g, gn_b, bdm, modsel, w_out, g_post, g_pre, t_start):
    B, S, D = X.shape
    tm = ROW_TILE
    nt = S // tm - t_start
    row = lambda w: pl.BlockSpec((1, tm, w), lambda b, i: (b, i, 0))
    full_row = lambda w: pl.BlockSpec((1, tm, w), lambda b, i: (b, i + t_start, 0))
    W = C_WIDTH
    return pl.pallas_call(
        _outproj_kernel,
        out_shape=[jax.ShapeDtypeStruct((B, nt * tm, D), F32), jax.ShapeDtypeStruct((B, nt * tm, D), BF16)],
        grid=(B, nt),
        in_specs=[full_row(D), row(A_WIDTH), row(ob.shape[-1])] + [full_row(W)] * 4
                 + [_full((1, W)), _full((1, W)), _full((W, W)),
                    pl.BlockSpec((1, 1, 1, modsel.shape[-1]), lambda b, i: (b, jnp.minimum(i + t_start, 1), 0, 0)),
                    _full(w_out.shape), _full((1, D)), _full((1, D))],
        out_specs=[row(D), row(D)],
        compiler_params=_params(("parallel", "parallel")),
        name="out_proj",
    )(X, oa, ob, *rwkv, gn_g, gn_b, bdm, modsel, w_out, g_post, g_pre)


def _ffn_kernel(n_ctx_tiles, n_tiles, t_start, ff_tile,
                x_ref, h_ref, hp_ref, hn_ref, mod_ref, wup_ref, cw_ref, cb_ref, wdn_ref, gpost_ref, o_ref,
                act_ref):
    D = x_ref.shape[-1]
    tm = x_ref.shape[1]
    dff = wdn_ref.shape[0]
    i = pl.program_id(1) + t_start
    seq_first = (i == 0) | (i == n_ctx_tiles)
    seq_last = (i == n_ctx_tiles - 1) | (i == n_tiles - 1)
    hp = hp_ref[0]
    hn = hn_ref[0]
    hp = jnp.where(seq_first, jnp.zeros_like(hp), hp)
    hn = jnp.where(seq_last, jnp.zeros_like(hn), hn)
    hext = jnp.concatenate([hp, h_ref[0], hn], axis=0)
    halo = hp.shape[0]
    for f in range(dff // ff_tile):
        parts = []
        for base in (0, dff):
            lo = base + f * ff_tile
            u = _dot(hext, wup_ref[:, lo:lo + ff_tile])
            cw = cw_ref[:, lo:lo + ff_tile]
            n = u.shape[0]
            up = pltpu.roll(u, 1, axis=0)[halo:halo + tm]
            un = pltpu.roll(u, n - 1, axis=0)[halo:halo + tm]
            parts.append(up * cw[0:1] + u[halo:halo + tm] * cw[1:2] + un * cw[2:3]
                         + cb_ref[:, lo:lo + ff_tile])
        act_ref[:, f * ff_tile:(f + 1) * ff_tile] = _bf(_silu(parts[0]) * parts[1])
    acc = _dot(act_ref[...], wdn_ref[...])
    gt2 = mod_ref[0, 0][:, 5 * D:6 * D]
    o_ref[0] = x_ref[0] + gt2 * (_rms(acc, NORM_EPS) * gpost_ref[...])


def _ffn(X1, H2, modsel, w_up, conv_w, conv_b, w_dn, g_post, n_ctx, t_start):
    B, S, D = X1.shape
    tm = ROW_TILE
    halo = 16
    nt = S // tm
    hb = tm // halo
    nhb = S // halo
    kern = functools.partial(_ffn_kernel, n_ctx // tm, nt + t_start, t_start, 256)
    row = pl.BlockSpec((1, tm, D), lambda b, i: (b, i, 0))
    return pl.pallas_call(
        kern,
        out_shape=jax.ShapeDtypeStruct((B, S, D), F32),
        grid=(B, nt),
        in_specs=[row, row,
                  pl.BlockSpec((1, halo, D), lambda b, i: (b, jnp.maximum(i * hb - 1, 0), 0)),
                  pl.BlockSpec((1, halo, D), lambda b, i: (b, jnp.minimum((i + 1) * hb, nhb - 1), 0)),
                  pl.BlockSpec((1, 1, 1, modsel.shape[-1]), lambda b, i: (b, jnp.minimum(i + t_start, 1), 0, 0)),
                  _full(w_up.shape), _full(conv_w.shape), _full(conv_b.shape), _full(w_dn.shape), _full((1, D))],
        out_specs=row,
        scratch_shapes=[pltpu.VMEM((tm, w_dn.shape[0]), BF16)],
        compiler_params=_params(("parallel", "parallel")),
        name="conv_ffn",
    )(X1, H2, H2, H2, modsel, w_up, conv_w, conv_b, w_dn, g_post)


def _rope_tables(S, n_ctx, dim, lane_lo):
    nf = dim // 4
    t = jnp.arange(S - n_ctx, dtype=jnp.int32)
    rows = (t // GRID_W).astype(F32)
    cols = (t % GRID_W).astype(F32)
    inv = ROPE_BASE ** (-jnp.arange(nf, dtype=F32) / nf)
    ar, ac = rows[:, None] * inv, cols[:, None] * inv
    cos = jnp.concatenate([jnp.cos(ar), jnp.cos(ar), jnp.cos(ac), jnp.cos(ac)], axis=-1)
    sin = jnp.concatenate([-jnp.sin(ar), jnp.sin(ar), -jnp.sin(ac), jnp.sin(ac)], axis=-1)
    if lane_lo == 0:
        reps = LANE // dim
        cos, sin = jnp.tile(cos, (1, reps)), jnp.tile(sin, (1, reps))
    else:
        pad = ((0, 0), (lane_lo, LANE - lane_lo - dim))
        cos = jnp.pad(cos, pad, constant_values=1.0)
        sin = jnp.pad(sin, pad)
    cos = jnp.concatenate([jnp.ones((n_ctx, LANE), F32), cos], axis=0)
    sin = jnp.concatenate([jnp.zeros((n_ctx, LANE), F32), sin], axis=0)
    return cos, sin


def _layout_w_in(w):
    D = w.shape[0]
    a = w[:, :3 * A_WIDTH]
    o = 3 * A_WIDTH
    cq = w[:, o:o + B_Q_RANK]
    ckv = w[:, o + B_Q_RANK:o + B_Q_RANK + B_KV_RANK]
    kr = w[:, o + B_Q_RANK + B_KV_RANK:o + B_Q_RANK + B_KV_RANK + B_ROPE]
    c = w[:, o + B_Q_RANK + B_KV_RANK + B_ROPE:]
    z = jnp.zeros((D, 2 * LANE - B_Q_RANK - B_ROPE), w.dtype)
    return _bf(jnp.concatenate([a, cq, kr, z, ckv, c], axis=1))


def _layout_wq(w):
    hd = B_NOPE + B_ROPE
    w = w.reshape(B_Q_RANK, B_HEADS, hd)
    w = jnp.pad(w, ((0, 2 * LANE - B_Q_RANK), (0, 0), (0, LANE - hd)))
    return _bf(w.reshape(2 * LANE, B_HEADS * LANE))


def _layout_wkv(w):
    w = w.reshape(B_KV_RANK, B_HEADS, B_NOPE + B_VDIM)
    pad = lambda t: jnp.pad(t, ((0, 0), (0, 0), (0, LANE - t.shape[-1]))).reshape(B_KV_RANK, B_HEADS * LANE)
    return _bf(jnp.concatenate([pad(w[:, :, :B_NOPE]), pad(w[:, :, B_NOPE:])], axis=1))


def _layout_w_out(w):
    D = w.shape[1]
    wb = w[A_WIDTH:A_WIDTH + B_WIDTH].reshape(B_HEADS, B_VDIM, D)
    wb = jnp.pad(wb, ((0, 0), (0, LANE - B_VDIM), (0, 0))).reshape(B_HEADS * LANE, D)
    return _bf(jnp.concatenate([w[:A_WIDTH], wb, w[A_WIDTH + B_WIDTH:]], axis=0))


def _layout_lora(w, d):
    z = jnp.zeros_like(w)
    return jnp.concatenate([z, w] if d else [w, z], axis=0)


def _block_diag_ones(scale):
    i = np.arange(C_WIDTH) // C_DIM
    return jnp.asarray((i[:, None] == i[None, :]).astype(np.float32) * scale, dtype=BF16)


def kernel(x, c, ctx, c_ctx, ada_w, ada_b, mix_pre_g, mix_post_g, ffn_pre_g, ffn_post_g, w_in, w_out, lam_q1, lam_k1, lam_q2, lam_k2, a_subln_g, b_q_norm_g, b_w_q_up, b_kv_norm_g, b_w_kv_up, c_mu_prev, c_mu_next, c_w0, c_w2, c_a0, c_a2, c_g2, c_k_k, c_k_a, c_r_k, c_gn_g, c_gn_b, ffn_w_up, ffn_conv_w, ffn_conv_b, ffn_w_down):
    B, T, D = x.shape
    n_ctx = ctx.shape[1]
    S = n_ctx + T
    L = ada_w.shape[0]
    assert n_ctx == ROW_TILE and T % ROW_TILE == 0 and D % LANE == 0

    X = jnp.concatenate([ctx, x], axis=1)
    cond = jnp.concatenate([c, c_ctx[None, :], jnp.zeros((SUBLANE - B - 1, D), F32)], axis=0)
    mod = _modulation(cond, ada_w, ada_b)
    ropeA = _rope_tables(S, n_ctx, A_DIM, 0)
    ropeB = _rope_tables(S, n_ctx, B_ROPE, B_NOPE)
    bd1 = _block_diag_ones(1.0)
    bdm = _block_diag_ones(1.0 / C_DIM)
    row = lambda v: v.reshape(1, -1)

    for i in range(L):
        last = i == L - 1
        t0 = 1 if last else 0
        lam_init = 0.8 - 0.6 * math.exp(-0.3 * i)
        modsel = jnp.stack([jnp.broadcast_to(mod[i, B], (B, 6 * D)), mod[i, :B]], axis=1)[:, :, None, :]
        gq = jnp.pad(b_q_norm_g[i], (0, 2 * LANE - B_Q_RANK)).reshape(1, -1)
        qa, ka, va, qb, kb, vb, pr = _inproj(
            X, modsel, row(mix_pre_g[i]), _layout_w_in(w_in[i]), ropeA, ropeB,
            gq, row(b_kv_norm_g[i]), _layout_wq(b_w_q_up[i]), _layout_wkv(b_w_kv_up[i]))
        dargs = (row(lam_q1[i]), row(lam_k1[i]), row(lam_q2[i]), row(lam_k2[i]), row(a_subln_g[i]), lam_init)
        oa = _attention(qa, ka, va, n_ctx, True, dargs)
        ob = _attention(qb, kb, vb, n_ctx, True)
        if not last:
            oa = jnp.concatenate([_attention(qa, ka, va, n_ctx, False, dargs), oa], axis=1)
            ob = jnp.concatenate([_attention(qb, kb, vb, n_ctx, False), ob], axis=1)
        loc = _rwkv_local(pr, row(c_mu_prev[i]), row(c_mu_next[i]),
                          c_w0[i], jnp.stack([_layout_lora(c_w2[i, d], d) for d in (0, 1)]),
                          c_a0[i], jnp.stack([_layout_lora(c_a2[i, d], d) for d in (0, 1)]),
                          c_g2[i], row(c_k_k[i]), row(c_k_a[i]), row(c_r_k[i]), bd1)
        yf, yb = _rwkv_state(loc, n_ctx)
        X1, H2 = _outproj(X, oa, ob, (yf, yb, loc[8], loc[9]), row(c_gn_g[i]), row(c_gn_b[i]), bdm, modsel,
                          _layout_w_out(w_out[i]), row(mix_post_g[i]), row(ffn_pre_g[i]), t0)
        X = _ffn(X1, H2, modsel, _bf(ffn_w_up[i]), ffn_conv_w[i], row(ffn_conv_b[i]), _bf(ffn_w_down[i]),
                 row(ffn_post_g[i]), n_ctx, t0)
    return X
```

```python
import functools
import math

import jax
import jax.numpy as jnp
import numpy as np
from jax import lax
from jax.experimental import pallas as pl
from jax.experimental.pallas import tpu as pltpu

F32 = jnp.float32
BF16 = jnp.bfloat16

GRID_W = 64
ROPE_BASE = 10000.0
NORM_EPS = 1e-6
SUBLN_EPS = 1e-5
A_HEADS, A_DIM = 4, 64
A_WIDTH = A_HEADS * 2 * A_DIM
B_HEADS, B_NOPE, B_ROPE, B_VDIM = 4, 64, 32, 64
B_Q_RANK, B_KV_RANK = 192, 128
B_WIDTH = B_HEADS * B_VDIM
C_HEADS, C_DIM = 4, 64
C_WIDTH = C_HEADS * C_DIM
C_LORA = 64
C_GATE_LORA = 128
C_GN_EPS = 64e-5
C_COLS = 3 * C_WIDTH + 4 * C_LORA + C_GATE_LORA

LANE = 128
SUBLANE = 8
ROW_TILE = 256
Q_TILE = 512
LAT_BLOCK = 512
LOG2E = 1.4426950408889634
CHUNK = 64
VMEM_LIMIT = 56 * 1024 * 1024

P_AQ, P_AK, P_AV = 0, 512, 1024
P_BQ = 1536
P_BKV = 1792
P_C = 1920
P_TOTAL = 3072


def _bf(x):
    return x.astype(BF16)


def _dot(a, b, ca=1, cb=0):
    return lax.dot_general(a, b, (((ca,), (cb,)), ((), ())), preferred_element_type=F32)


def _mm(a, b):
    return _dot(_bf(a), _bf(b))


def _mm_nt(a, b):
    return _dot(_bf(a), _bf(b), 1, 1)


def _mm_tn(a, b):
    return _dot(_bf(a.T), _bf(b))


def _split2(x):
    hi = _bf(x)
    lo = _bf(x - hi.astype(F32))
    return hi, lo


def _mm_hl(a, b):
    hi, lo = _split2(a)
    return _dot(hi, b) + _dot(lo, b)


def _mm3(a, b):
    ah, al = _split2(a)
    bh, bl = _split2(b)
    return _dot(ah, bh) + (_dot(ah, bl) + _dot(al, bh))


def _rms(x, eps):
    return x * lax.rsqrt(jnp.mean(x * x, axis=-1, keepdims=True) + eps)


def _sigmoid(x):
    return 1.0 / (1.0 + jnp.exp(-x))


def _silu(x):
    return x * _sigmoid(x)


def _softplus(x):
    return jnp.maximum(x, 0.0) + jnp.log(1.0 + jnp.exp(-jnp.abs(x)))


def _lane_iota(shape):
    return lax.broadcasted_iota(jnp.int32, shape, len(shape) - 1)


def _row_iota(shape):
    return lax.broadcasted_iota(jnp.int32, shape, len(shape) - 2)


def _rope(x, cos, sin, half):
    n = x.shape[-1]
    up = pltpu.roll(x, n - half, axis=1)
    dn = pltpu.roll(x, half, axis=1)
    first = (_lane_iota(x.shape) & half) == 0
    return x * cos + jnp.where(first, up, dn) * sin


def _params(sem):
    return pltpu.CompilerParams(dimension_semantics=sem, vmem_limit_bytes=VMEM_LIMIT)


def _full(shape):
    nd = len(shape)
    return pl.BlockSpec(shape, lambda *_: (0,) * nd)


def _batch_rows(B):
    return 2 if B % 2 == 0 else 1


def _per_batch_row(tile_fn, is_row):
    def kern(*refs):
        n = next(r for r, m in zip(refs, is_row) if m).shape[0]
        for bi in range(n):
            tile_fn(*[r.at[pl.ds(bi, 1)] if m else r for r, m in zip(refs, is_row)])
    return kern


def _mod_kernel(c_ref, w_ref, b_ref, o_ref):
    act = _silu(c_ref[...])
    o_ref[0] = _mm3(act, w_ref[0]) + b_ref[0]


def _modulation(cond, ada_w, ada_b):
    L, D, N = ada_w.shape
    R = cond.shape[0]
    tn = 1024
    return pl.pallas_call(
        _mod_kernel,
        out_shape=jax.ShapeDtypeStruct((L, R, N), F32),
        grid=(L, N // tn),
        in_specs=[pl.BlockSpec((R, D), lambda l, j: (0, 0)),
                  pl.BlockSpec((1, D, tn), lambda l, j: (l, 0, j)),
                  pl.BlockSpec((1, 1, tn), lambda l, j: (l, 0, j))],
        out_specs=pl.BlockSpec((1, R, tn), lambda l, j: (l, 0, j)),
        compiler_params=_params(("parallel", "parallel")),
        name="adaln_mod",
    )(cond, ada_w, ada_b.reshape(L, 1, N))


def _inproj_kernel(xc_ref, xl_ref, mod_ref, g_ref, w_ref, ca_ref, sa_ref, cb_ref, sb_ref,
                   gq_ref, gkv_ref, wq_ref, wkv_ref,
                   qa_ref, ka_ref, va_ref, qb_ref, kb_ref, vb_ref, pr_ref):
    D = xc_ref.shape[-1]
    x = jnp.where(pl.program_id(1) == 0, xc_ref[0], xl_ref[0])
    mod = mod_ref[0, 0]
    sh1, sc1 = mod[:, 0:D], mod[:, D:2 * D]
    h = _bf(_rms(x, NORM_EPS) * g_ref[...] * (1.0 + sc1) + sh1)

    def proj(lo, hi):
        return _dot(h, w_ref[:, lo:hi])

    ca, sa = ca_ref[...], sa_ref[...]
    cb, sb = cb_ref[...], sb_ref[...]

    pq = proj(P_AQ, P_AK)
    pk = proj(P_AK, P_AV)
    for j in range(A_WIDTH // LANE):
        sl = slice(j * LANE, (j + 1) * LANE)
        qa_ref[0, :, sl] = _bf(_rope(pq[:, sl], ca, sa, A_DIM // 4) * (A_DIM ** -0.5 * LOG2E))
        ka_ref[0, :, sl] = _bf(_rope(pk[:, sl], ca, sa, A_DIM // 4))
    va_ref[0] = _bf(proj(P_AV, P_BQ))

    pbq = proj(P_BQ, P_BKV)
    lane = _lane_iota(pbq.shape)
    cq = jnp.where(lane < B_Q_RANK, pbq, 0.0)
    cqn = cq * lax.rsqrt(jnp.sum(cq * cq, axis=-1, keepdims=True) * (1.0 / B_Q_RANK) + NORM_EPS) * gq_ref[...]
    qb = _mm(cqn, wq_ref[...]) * ((B_NOPE + B_ROPE) ** -0.5 * LOG2E)
    ckv = proj(P_BKV, P_C)
    ckvn = _rms(ckv, NORM_EPS) * gkv_ref[...]
    kv = _mm(ckvn, wkv_ref[...])
    krb = pbq[:, LANE:2 * LANE]
    l1 = _lane_iota(krb.shape)
    kr = _rope(jnp.where((l1 >= B_NOPE) & (l1 < B_NOPE + B_ROPE), krb, 0.0), cb, sb, B_ROPE // 4)
    for j in range(B_HEADS):
        sl = slice(j * LANE, (j + 1) * LANE)
        qb_ref[0, :, sl] = _bf(_rope(qb[:, sl], cb, sb, B_ROPE // 4))
        kb_ref[0, :, sl] = _bf(kv[:, sl] + kr)
        vh = kv[:, B_HEADS * LANE + j * LANE:B_HEADS * LANE + (j + 1) * LANE]
        vb_ref[0, :, sl] = _bf(jnp.where(l1 < B_VDIM, vh, 1.0))

    pr_ref[0] = proj(P_C, P_TOTAL)


def _stream_specs(nb, tm, D, first, t_start=0):
    return [pl.BlockSpec((nb, tm, D), lambda b, i: (b, 0, 0)),
            pl.BlockSpec((nb, tm, D), lambda b, i: (b, jnp.maximum(i + t_start - first, 0), 0))]


def _inproj(Xc, Xl, modsel, g, w_in, ropeA, ropeB, gq, gkv, wq, wkv):
    B, _, D = Xc.shape
    tm = ROW_TILE
    first = 0 if Xl is Xc else Xc.shape[1] // tm
    S = Xl.shape[1] + first * tm
    nb = _batch_rows(B)
    row = lambda w: pl.BlockSpec((nb, tm, w), lambda b, i: (b, i, 0))
    tab = pl.BlockSpec((tm, LANE), lambda b, i: (i, 0))
    outs = [jax.ShapeDtypeStruct((B, S, 512), BF16)] * 6 + [jax.ShapeDtypeStruct((B, S, C_COLS), F32)]
    return pl.pallas_call(
        _per_batch_row(_inproj_kernel, [True, True, True] + [False] * 10 + [True] * 7),
        out_shape=outs,
        grid=(B // nb, S // tm),
        in_specs=_stream_specs(nb, tm, D, first) + [
                  pl.BlockSpec((nb, 1, 1, modsel.shape[-1]), lambda b, i: (b, jnp.minimum(i, 1), 0, 0)),
                  _full((1, D)), _full(w_in.shape), tab, tab, tab, tab,
                  _full(gq.shape), _full(gkv.shape), _full(wq.shape), _full(wkv.shape)],
        out_specs=[row(512)] * 6 + [row(C_COLS)],
        compiler_params=_params(("parallel", "parallel")),
        name="in_proj",
    )(Xc, Xl, modsel, g, w_in, ropeA[0], ropeA[1], ropeB[0], ropeB[1], gq, gkv, wq, wkv)


def _attn_kernel(diff, lam_init, n_ctx, n_lat, n_q, *refs):
    q_refs, (k_ref, v_ref), rest = refs[:n_q], refs[n_q:n_q + 2], refs[n_q + 2:]
    if diff:
        lq1, lk1, lq2, lk2, g_ref, o_ref, sc_ref, sl_ref = rest
    else:
        o_ref, sc_ref, sl_ref = rest
    q = jnp.concatenate([r[0] for r in q_refs], axis=0) if n_q > 1 else q_refs[0][0]
    tq = q.shape[0]
    lane = _lane_iota((tq, LANE))
    if diff:
        zero = jnp.zeros_like(q)
        maps = [(jnp.where(lane < A_DIM, q, zero), slice(0, LANE)), (jnp.where(lane >= A_DIM, q, zero), slice(0, LANE))]
    else:
        maps = [(q[:, h * LANE:(h + 1) * LANE], slice(h * LANE, (h + 1) * LANE)) for h in range(q.shape[1] // LANE)]
    blocks = [(0, n_ctx, lambda mi: sc_ref.at[mi])]
    blocks += [(n_ctx + j * LAT_BLOCK, LAT_BLOCK, lambda mi, j=j: sl_ref.at[mi * n_lat + j]) for j in range(n_lat)]

    ms = []
    for mi, (qm, ksl) in enumerate(maps):
        mrun = jnp.full((tq, LANE), -jnp.inf, F32)
        for off, size, buf in blocks:
            s = _dot(qm, k_ref[0, off:off + size, ksl], 1, 1)
            buf(mi)[...] = s
            for c in range(size // LANE):
                mrun = jnp.maximum(mrun, s[:, c * LANE:(c + 1) * LANE])
        ms.append(jnp.max(mrun, axis=-1, keepdims=True))

    outs = []
    for mi, (qm, ksl) in enumerate(maps):
        acc = None
        for off, size, buf in blocks:
            v = v_ref[0, off:off + size, ksl]
            if diff:
                v = jnp.concatenate([v, jnp.ones_like(v)], axis=1)
            pv = _dot(_bf(jnp.exp2(buf(mi)[...] - ms[mi])), v)
            acc = pv if acc is None else acc + pv
        if diff:
            outs.append(acc[:, :LANE] / acc[:, LANE:])
        else:
            outs.append(jnp.where(lane < B_VDIM, acc / pltpu.roll(acc, B_VDIM, axis=1), 0.0))

    if diff:
        lam = (jnp.exp(jnp.sum(lq1[...] * lk1[...], axis=-1, keepdims=True))
               - jnp.exp(jnp.sum(lq2[...] * lk2[...], axis=-1, keepdims=True)) + lam_init)
        o = outs[0] - lam * outs[1]
        o_ref[0] = _bf(_rms(o, SUBLN_EPS) * g_ref[...] * (1.0 - lam_init))
    else:
        for h, o in enumerate(outs):
            o_ref[0, :, h * LANE:(h + 1) * LANE] = _bf(o)


def _attention(q, k, v, n_ctx, latent, diff_args=None):
    B, S, _ = k.shape
    tq = Q_TILE if latent else ROW_TILE
    tb = ROW_TILE
    n_q = tq // tb
    t0 = n_ctx // tb if latent else 0
    nq = (S - n_ctx) // tq if latent else n_ctx // tq
    n_keys = S if latent else n_ctx
    diff = diff_args is not None
    bw = LANE if diff else 2 * LANE
    assert (n_keys - n_ctx) % LAT_BLOCK == 0
    n_lat = (n_keys - n_ctx) // LAT_BLOCK
    if diff:
        lq1, lk1, lq2, lk2, subln_g, lam_init = diff_args
        extra = [lq1, lk1, lq2, lk2, subln_g]
        vec = _full((1, A_DIM))
        extra_specs = [vec, vec, vec, vec, _full((1, LANE))]
    else:
        lam_init, extra, extra_specs = None, [], []
    kern = functools.partial(_attn_kernel, diff, lam_init, n_ctx, n_lat, n_q)
    qspecs = [pl.BlockSpec((1, tb, bw), lambda b, h, i, j=j: (b, t0 + i * n_q + j, h)) for j in range(n_q)]
    kvspec = pl.BlockSpec((1, n_keys, bw), lambda b, h, i: (b, 0, h))
    return pl.pallas_call(
        kern,
        out_shape=jax.ShapeDtypeStruct((B, nq * tq, q.shape[-1]), BF16),
        grid=(B, q.shape[-1] // bw, nq),
        in_specs=qspecs + [kvspec, kvspec] + extra_specs,
        out_specs=pl.BlockSpec((1, tq, bw), lambda b, h, i: (b, i, h)),
        scratch_shapes=[pltpu.VMEM((2, tq, n_ctx), F32),
                        pltpu.VMEM((max(2 * n_lat, 1), tq, LAT_BLOCK), F32)],
        compiler_params=_params(("parallel", "parallel", "parallel")),
        name=("diff_attn" if diff else "mla_attn") + ("" if latent else "_ctx"),
    )(*([q] * n_q), k, v, *extra)


def _rwkv_chunk_index(rev, n_ctx_chunks, n_chunks, j):
    if not rev:
        return j
    return jnp.where(j < n_ctx_chunks, n_ctx_chunks - 1 - j, n_chunks + n_ctx_chunks - 1 - j)


def _rwkv_local_kernel(n_tiles,
                       cur_ref, prv_ref, nxt_ref, mup_ref, mun_ref, w0_ref, w2_ref, a0_ref, a2_ref, g2_ref,
                       kk_ref, ka_ref, rk_ref, bd_ref,
                       qf_ref, y0f_ref, mf_ref, nf_ref, qb_ref, y0b_ref, mb_ref, nb_ref, bonus_ref, gate_ref):
    C, W = CHUNK, C_WIDTH
    NB, R = cur_ref.shape[0], cur_ref.shape[1]
    i = pl.program_id(1)
    seq_first = i <= 1
    seq_last = (i == 0) | (i == n_tiles - 1)

    rr = _row_iota((R, R))
    cr = _lane_iota((R, R))
    same_chunk = (rr >> 6) == (cr >> 6)
    rows = _row_iota((R, C_COLS))
    rc = _row_iota((C, W))
    cc = _lane_iota((C, W)) & (C - 1)
    eye = jnp.where(rc == cc, 1.0, 0.0)
    pair = [((rc >> (lvl + 1)) == (cc >> (lvl + 1))) & ((rc >> lvl) != (cc >> lvl))
            for lvl in range(int(math.log2(C)))]
    same_head = (_row_iota((W, W)) >> 6) == (_lane_iota((W, W)) >> 6)
    bd = bd_ref[...]
    outs = ((qf_ref, y0f_ref, mf_ref, nf_ref), (qb_ref, y0b_ref, mb_ref, nb_ref))

    chunks = []
    for bi in range(NB):
        x = cur_ref[bi]
        prev_row = jnp.where(seq_first, 0.0, prv_ref[bi, SUBLANE - 1:SUBLANE, :])
        next_row = jnp.where(seq_last, 0.0, nxt_ref[bi, 0:1, :])
        xp = jnp.where(rows == 0, prev_row, pltpu.roll(x, 1, axis=0))
        xn = jnp.where(rows == R - 1, next_row, pltpu.roll(x, R - 1, axis=0))
        xs = x + mup_ref[...] * (xp - x) + mun_ref[...] * (xn - x)

        r, k, v = xs[:, 0:W], xs[:, W:2 * W], xs[:, 2 * W:3 * W]
        wl = jnp.tanh(xs[:, 3 * W:3 * W + 2 * C_LORA])
        al = xs[:, 3 * W + 2 * C_LORA:3 * W + 4 * C_LORA]
        gl = xs[:, 3 * W + 4 * C_LORA:]
        kkr = k * kk_ref[...]
        kk = kkr / jnp.maximum(jnp.sqrt(_mm_hl(kkr * kkr, bd)), 1e-12)
        gate_ref[bi] = _mm(_sigmoid(gl), g2_ref[...])

        bonus = jnp.zeros((R, W), F32)
        for d, rev in enumerate((False, True)):
            w = -_softplus(-(w0_ref[d:d + 1] + _mm3(wl, w2_ref[d]))) - 0.5
            lw = -jnp.exp(w)
            a_ic = _sigmoid(a0_ref[d:d + 1] + _mm3(al, a2_ref[d]))
            kd = k * (1.0 + (a_ic - 1.0) * ka_ref[...])
            avec = -kk
            bvec = kk * a_ic
            bonus = bonus + _mm_hl(r * kd * rk_ref[...], bd) * v

            tri = jnp.where(same_chunk & ((cr >= rr) if rev else (cr <= rr)), 1.0, 0.0).astype(BF16)
            l_hi = _bf(lw)
            l_md = _bf(lw - l_hi.astype(F32))
            l_lo = _bf(lw - l_hi.astype(F32) - l_md.astype(F32))
            cs = _dot(tri, l_hi) + (_dot(tri, l_md) + _dot(tri, l_lo))
            e_neg = jnp.exp(-cs)
            At_all = avec * jnp.exp(cs - lw)
            Rt_all = r * jnp.exp(cs)
            Bt_all = bvec * e_neg
            Kt_all = kd * e_neg
            for g in range(R // C):
                sl = slice(g * C, (g + 1) * C)
                csg = cs[sl]
                tot = csg[0:1, :] if rev else csg[C - 1:C, :]
                e_rem = jnp.exp(tot - csg)
                chunks.append(dict(bi=bi, d=d, g=g, rev=rev, sl=sl, At=At_all[sl], Rt=Rt_all[sl], v=v[sl], tot=tot,
                                   Bt=Bt_all[sl], Kt=Kt_all[sl], Bg=bvec[sl] * e_rem, Kg=kd[sl] * e_rem))
        bonus_ref[bi] = bonus

    def heads_bd(x):
        xb = _bf(x)
        return jnp.where(same_head, jnp.concatenate([xb] * C_HEADS, axis=0), jnp.zeros((W, W), BF16))

    def each(fn, *lists):
        return [fn(*args) for args in zip(*lists)]

    X = [jnp.concatenate([ch["At"], ch["Rt"]], axis=0) for ch in chunks]
    GB = each(lambda x, ch: _dot(_bf(x), heads_bd(ch["Bt"]), 1, 1), X, chunks)
    GK = each(lambda x, ch: _dot(_bf(x), heads_bd(ch["Kt"]), 1, 1), X, chunks)
    strict = {False: cc < rc, True: cc > rc}
    incl = {False: cc <= rc, True: cc >= rc}
    Aab = each(lambda g_, ch: jnp.where(strict[ch["rev"]], g_[0:C], 0.0), GB, chunks)
    Arb = each(lambda g_, ch: jnp.where(incl[ch["rev"]], g_[C:], 0.0), GB, chunks)
    Aak = each(lambda g_, ch: jnp.where(strict[ch["rev"]], g_[0:C], 0.0), GK, chunks)
    Ark = each(lambda g_, ch: jnp.where(incl[ch["rev"]], g_[C:], 0.0), GK, chunks)
    T = [eye + jnp.where(pair[0], a, 0.0) for a in Aab]
    for lvl in range(1, len(pair)):
        P = each(lambda t, a: _dot(_bf(t), heads_bd(jnp.where(pair[lvl], a, 0.0))), T, Aab)
        T = each(lambda t, p: t + _dot(_bf(p), heads_bd(t)), T, P)
    Vbd = [heads_bd(ch["v"]) for ch in chunks]
    AkV = each(lambda a, vb: _dot(_bf(a), vb), Aak, Vbd)
    Wt = each(lambda t, ch: _dot(_bf(t), heads_bd(ch["At"])), T, chunks)
    U0 = each(lambda t, akv: _dot(_bf(t), heads_bd(akv)), T, AkV)
    Q = each(lambda a, wt, ch: ch["Rt"] + _dot(_bf(a), heads_bd(wt)), Arb, Wt, chunks)
    Y0 = each(lambda arb, u0, ark, vb: _dot(_bf(arb), heads_bd(u0)) + _dot(_bf(ark), vb), Arb, U0, Ark, Vbd)

    for ch, wt, u0, q, y0 in zip(chunks, Wt, U0, Q, Y0):
        q_ref, y0_ref, m_ref, n_ref = outs[ch["d"]]
        q_ref[ch["bi"], ch["sl"], :] = _bf(q)
        y0_ref[ch["bi"], ch["sl"], :] = y0
        Mbd = jnp.where(same_head, _mm_tn(wt, ch["Bg"]), 0.0)
        Nbd = jnp.where(same_head, _mm_tn(u0, ch["Bg"]) + _mm_tn(ch["v"], ch["Kg"]), 0.0)
        Mc = Mbd[0:C] + Mbd[C:2 * C] + Mbd[2 * C:3 * C] + Mbd[3 * C:]
        m_ref[ch["bi"], ch["g"]] = _bf(Mc + jnp.where(rc == cc, jnp.exp(ch["tot"]), 0.0))
        n_ref[ch["bi"], ch["g"]] = Nbd[0:C] + Nbd[C:2 * C] + Nbd[2 * C:3 * C] + Nbd[3 * C:]


def _rwkv_local(pr, mu_prev, mu_next, w0, w2p, a0, a2p, g2, k_k, k_a, r_k, bd):
    B, S, _ = pr.shape
    R, C, W = ROW_TILE, CHUNK, C_WIDTH
    NB = 2 if B % 2 == 0 else 1
    nt, gpt = S // R, R // C
    bpt = R // SUBLANE
    nblk = S // SUBLANE
    rowspec = lambda w: pl.BlockSpec((NB, R, w), lambda b, i: (b, i, 0))
    mspec = pl.BlockSpec((NB, gpt, C, W), lambda b, i: (b, i, 0, 0))
    seq = lambda dt: jax.ShapeDtypeStruct((B, S, W), dt)
    mat = lambda dt: jax.ShapeDtypeStruct((B, S // C, C, W), dt)
    per_dir = [seq(BF16), seq(F32), mat(BF16), mat(F32)]
    vec = _full((1, W))
    return pl.pallas_call(
        functools.partial(_rwkv_local_kernel, nt),
        out_shape=per_dir * 2 + [seq(F32), seq(F32)],
        grid=(B // NB, nt),
        in_specs=[rowspec(C_COLS),
                  pl.BlockSpec((NB, SUBLANE, C_COLS), lambda b, i: (b, jnp.maximum(i * bpt - 1, 0), 0)),
                  pl.BlockSpec((NB, SUBLANE, C_COLS), lambda b, i: (b, jnp.minimum((i + 1) * bpt, nblk - 1), 0)),
                  _full((1, C_COLS)), _full((1, C_COLS)), _full(w0.shape), _full(w2p.shape), _full(a0.shape),
                  _full(a2p.shape), _full(g2.shape), vec, vec, vec, _full(bd.shape)],
        out_specs=[rowspec(W), rowspec(W), mspec, mspec] * 2 + [rowspec(W), rowspec(W)],
        compiler_params=_params(("parallel", "parallel")),
        name="rwkv_local",
    )(pr, pr, pr, mu_prev, mu_next, w0, w2p, a0, a2p, g2, k_k, k_a, r_k, bd)


def _rwkv_state_kernel(qf_ref, y0f_ref, mf_ref, nf_ref, qb_ref, y0b_ref, mb_ref, nb_ref, yf_ref, yb_ref, s_ref):
    C, W = CHUNK, C_WIDTH
    nb = qf_ref.shape[0]

    @pl.when(pl.program_id(0) == 0)
    def _():
        s_ref[...] = jnp.zeros(s_ref.shape, F32)

    head = _lane_iota((C, W)) >> 6

    def expand(mc):
        return jnp.concatenate([jnp.where(head == h, mc, jnp.zeros_like(mc)) for h in range(C_HEADS)], axis=0)

    dirs = ((qf_ref, y0f_ref, mf_ref, nf_ref, yf_ref), (qb_ref, y0b_ref, mb_ref, nb_ref, yb_ref))
    for d, (q_ref, y0_ref, m_ref, n_ref, y_ref) in enumerate(dirs):
        for b in range(nb):
            S = _bf(s_ref[d, b])
            y_ref[b] = _dot(q_ref[b], S, 1, 1) + y0_ref[b]
            s_ref[d, b] = _dot(S, expand(m_ref[b, 0])) + expand(n_ref[b, 0])


def _rwkv_state(loc, n_ctx):
    qf = loc[0]
    B, S, W = qf.shape
    C = CHUNK
    nch, ncc = S // C, n_ctx // C
    fwd = lambda j: j
    bwd = functools.partial(_rwkv_chunk_index, True, ncc, nch)
    seqspec = lambda idx: pl.BlockSpec((B, C, W), lambda j: (0, idx(j), 0))
    matspec = lambda idx: pl.BlockSpec((B, 1, C, W), lambda j: (0, idx(j), 0, 0))
    out = jax.ShapeDtypeStruct((B, S, W), F32)
    return pl.pallas_call(
        _rwkv_state_kernel,
        out_shape=[out, out],
        grid=(nch,),
        in_specs=[seqspec(fwd), seqspec(fwd), matspec(fwd), matspec(fwd),
                  seqspec(bwd), seqspec(bwd), matspec(bwd), matspec(bwd)],
        out_specs=[seqspec(fwd), seqspec(bwd)],
        scratch_shapes=[pltpu.VMEM((2, B, W, W), F32)],
        compiler_params=_params(("arbitrary",)),
        name="rwkv_state",
    )(*loc[:8])


def _outproj_kernel(t_start, xc_ref, xl_ref, oac_ref, oal_ref, obc_ref, obl_ref,
                    yf_ref, yb_ref, bonus_ref, gate_ref, gng_ref, gnb_ref, bdm_ref,
                    mod_ref, w_ref, gpost_ref, gpre_ref, x1_ref, h2_ref):
    D = xc_ref.shape[-1]
    is_ctx = pl.program_id(1) + t_start == 0
    x = jnp.where(is_ctx, xc_ref[0], xl_ref[0])
    oa = jnp.where(is_ctx, oac_ref[0], oal_ref[0])
    ob = jnp.where(is_ctx, obc_ref[0], obl_ref[0])
    mod = mod_ref[0, 0]
    gt1 = mod[:, 2 * D:3 * D]
    sh2, sc2 = mod[:, 3 * D:4 * D], mod[:, 4 * D:5 * D]
    y = yf_ref[0] + yb_ref[0]
    bdm = bdm_ref[...]
    mu = _mm_hl(y, bdm)
    dev = y - mu
    var = _mm_hl(dev * dev, bdm)
    yn = dev * lax.rsqrt(var + C_GN_EPS) * gng_ref[...] + gnb_ref[...]
    oc = _bf((yn + bonus_ref[0]) * gate_ref[0])
    nb = ob.shape[-1]
    o = (_dot(oa, w_ref[0:A_WIDTH, :])
         + _dot(ob, w_ref[A_WIDTH:A_WIDTH + nb, :])
         + _dot(oc, w_ref[A_WIDTH + nb:, :]))
    x1 = x + gt1 * (_rms(o, NORM_EPS) * gpost_ref[...])
    x1_ref[0] = x1
    h2_ref[0] = _bf(_rms(x1, NORM_EPS) * gpre_ref[...] * (1.0 + sc2) + sh2)


def _outproj(Xc, Xl, oa, ob, rwkv, gn_g, gn_b, bdm, modsel, w_out, g_post, g_pre, t_start):
    B, _, D = Xc.shape
    tm = ROW_TILE
    first = 0 if Xl is Xc else Xc.shape[1] // tm
    S = rwkv[0].shape[1]
    nt = S // tm - t_start
    nb = 1
    row = lambda w: pl.BlockSpec((nb, tm, w), lambda b, i: (b, i, 0))
    full_row = lambda w: pl.BlockSpec((nb, tm, w), lambda b, i: (b, i + t_start, 0))
    pair = lambda w: _stream_specs(nb, tm, w, 1, t_start)
    W = C_WIDTH
    return pl.pallas_call(
        _per_batch_row(functools.partial(_outproj_kernel, t_start),
                       [True] * 10 + [False] * 3 + [True] + [False] * 3 + [True] * 2),
        out_shape=[jax.ShapeDtypeStruct((B, nt * tm, D), F32), jax.ShapeDtypeStruct((B, nt * tm, D), BF16)],
        grid=(B // nb, nt),
        in_specs=_stream_specs(nb, tm, D, first, t_start) + pair(A_WIDTH) + pair(ob[1].shape[-1])
                 + [full_row(W)] * 4
                 + [_full((1, W)), _full((1, W)), _full((W, W)),
                    pl.BlockSpec((nb, 1, 1, modsel.shape[-1]), lambda b, i: (b, jnp.minimum(i + t_start, 1), 0, 0)),
                    _full(w_out.shape), _full((1, D)), _full((1, D))],
        out_specs=[row(D), row(D)],
        compiler_params=_params(("parallel", "parallel")),
        name="out_proj",
    )(Xc, Xl, *oa, *ob, *rwkv, gn_g, gn_b, bdm, modsel, w_out, g_post, g_pre)


def _ffn_kernel(n_ctx_tiles, n_tiles, t_start, ff_tile,
                x_ref, h_ref, hp_ref, hn_ref, mod_ref, wup_ref, cw_ref, cb_ref, wdn_ref, gpost_ref, o_ref,
                act_ref):
    D = x_ref.shape[-1]
    tm = x_ref.shape[1]
    dff = wdn_ref.shape[0]
    i = pl.program_id(1) + t_start
    seq_first = (i == 0) | (i == n_ctx_tiles)
    seq_last = (i == n_ctx_tiles - 1) | (i == n_tiles - 1)
    hp = hp_ref[0]
    hn = hn_ref[0]
    hp = jnp.where(seq_first, jnp.zeros_like(hp), hp)
    hn = jnp.where(seq_last, jnp.zeros_like(hn), hn)
    hext = jnp.concatenate([hp, h_ref[0], hn], axis=0)
    halo = hp.shape[0]
    for f in range(dff // ff_tile):
        parts = []
        for base in (0, dff):
            lo = base + f * ff_tile
            u = _dot(hext, wup_ref[:, lo:lo + ff_tile])
            cw = cw_ref[:, lo:lo + ff_tile]
            n = u.shape[0]
            up = pltpu.roll(u, 1, axis=0)[halo:halo + tm]
            un = pltpu.roll(u, n - 1, axis=0)[halo:halo + tm]
            parts.append(up * cw[0:1] + u[halo:halo + tm] * cw[1:2] + un * cw[2:3]
                         + cb_ref[:, lo:lo + ff_tile])
        act_ref[0, :, f * ff_tile:(f + 1) * ff_tile] = _bf(_silu(parts[0]) * parts[1])
    acc = _dot(act_ref[0], wdn_ref[...])
    gt2 = mod_ref[0, 0][:, 5 * D:6 * D]
    o_ref[0] = x_ref[0] + gt2 * (_rms(acc, NORM_EPS) * gpost_ref[...])


def _ffn(X1, H2, modsel, w_up, conv_w, conv_b, w_dn, g_post, n_ctx, t_start):
    B, S, D = X1.shape
    tm = ROW_TILE
    halo = 16
    nt = S // tm
    hb = tm // halo
    nhb = S // halo
    nb = _batch_rows(B)
    kern = _per_batch_row(functools.partial(_ffn_kernel, n_ctx // tm, nt + t_start, t_start, 256),
                          [True] * 5 + [False] * 5 + [True] * 2)
    row = pl.BlockSpec((nb, tm, D), lambda b, i: (b, i, 0))
    once = lambda shape: pl.BlockSpec(shape, lambda b, i: (0, 0), pipeline_mode=pl.Buffered(1))
    return pl.pallas_call(
        kern,
        out_shape=jax.ShapeDtypeStruct((B, S, D), F32),
        grid=(B // nb, nt),
        in_specs=[row, row,
                  pl.BlockSpec((nb, halo, D), lambda b, i: (b, jnp.maximum(i * hb - 1, 0), 0)),
                  pl.BlockSpec((nb, halo, D), lambda b, i: (b, jnp.minimum((i + 1) * hb, nhb - 1), 0)),
                  pl.BlockSpec((nb, 1, 1, modsel.shape[-1]), lambda b, i: (b, jnp.minimum(i + t_start, 1), 0, 0)),
                  once(w_up.shape), _full(conv_w.shape), _full(conv_b.shape), once(w_dn.shape), _full((1, D))],
        out_specs=row,
        scratch_shapes=[pltpu.VMEM((nb, tm, w_dn.shape[0]), BF16)],
        compiler_params=_params(("parallel", "parallel")),
        name="conv_ffn",
    )(X1, H2, H2, H2, modsel, w_up, conv_w, conv_b, w_dn, g_post)


def _rope_tables(S, n_ctx, dim, lane_lo):
    nf = dim // 4
    t = jnp.arange(S - n_ctx, dtype=jnp.int32)
    rows = (t // GRID_W).astype(F32)
    cols = (t % GRID_W).astype(F32)
    inv = ROPE_BASE ** (-jnp.arange(nf, dtype=F32) / nf)
    ar, ac = rows[:, None] * inv, cols[:, None] * inv
    cos = jnp.concatenate([jnp.cos(ar), jnp.cos(ar), jnp.cos(ac), jnp.cos(ac)], axis=-1)
    sin = jnp.concatenate([-jnp.sin(ar), jnp.sin(ar), -jnp.sin(ac), jnp.sin(ac)], axis=-1)
    if lane_lo == 0:
        reps = LANE // dim
        cos, sin = jnp.tile(cos, (1, reps)), jnp.tile(sin, (1, reps))
    else:
        pad = ((0, 0), (lane_lo, LANE - lane_lo - dim))
        cos = jnp.pad(cos, pad, constant_values=1.0)
        sin = jnp.pad(sin, pad)
    cos = jnp.concatenate([jnp.ones((n_ctx, LANE), F32), cos], axis=0)
    sin = jnp.concatenate([jnp.zeros((n_ctx, LANE), F32), sin], axis=0)
    return cos, sin


def _layout_w_in(w):
    D = w.shape[0]
    a = w[:, :3 * A_WIDTH]
    o = 3 * A_WIDTH
    cq = w[:, o:o + B_Q_RANK]
    ckv = w[:, o + B_Q_RANK:o + B_Q_RANK + B_KV_RANK]
    kr = w[:, o + B_Q_RANK + B_KV_RANK:o + B_Q_RANK + B_KV_RANK + B_ROPE]
    c = w[:, o + B_Q_RANK + B_KV_RANK + B_ROPE:]
    z = jnp.zeros((D, 2 * LANE - B_Q_RANK - B_ROPE), w.dtype)
    return _bf(jnp.concatenate([a, cq, kr, z, ckv, c], axis=1))


def _layout_wq(w):
    hd = B_NOPE + B_ROPE
    w = w.reshape(B_Q_RANK, B_HEADS, hd)
    w = jnp.pad(w, ((0, 2 * LANE - B_Q_RANK), (0, 0), (0, LANE - hd)))
    return _bf(w.reshape(2 * LANE, B_HEADS * LANE))


def _layout_wkv(w):
    w = w.reshape(B_KV_RANK, B_HEADS, B_NOPE + B_VDIM)
    pad = lambda t: jnp.pad(t, ((0, 0), (0, 0), (0, LANE - t.shape[-1]))).reshape(B_KV_RANK, B_HEADS * LANE)
    return _bf(jnp.concatenate([pad(w[:, :, :B_NOPE]), pad(w[:, :, B_NOPE:])], axis=1))


def _layout_w_out(w):
    D = w.shape[1]
    wb = w[A_WIDTH:A_WIDTH + B_WIDTH].reshape(B_HEADS, B_VDIM, D)
    wb = jnp.pad(wb, ((0, 0), (0, LANE - B_VDIM), (0, 0))).reshape(B_HEADS * LANE, D)
    return _bf(jnp.concatenate([w[:A_WIDTH], wb, w[A_WIDTH + B_WIDTH:]], axis=0))


def _layout_lora(w, d):
    z = jnp.zeros_like(w)
    return jnp.concatenate([z, w] if d else [w, z], axis=0)


def _block_diag_ones(scale):
    i = np.arange(C_WIDTH) // C_DIM
    return jnp.asarray((i[:, None] == i[None, :]).astype(np.float32) * scale, dtype=BF16)


def kernel(x, c, ctx, c_ctx, ada_w, ada_b, mix_pre_g, mix_post_g, ffn_pre_g, ffn_post_g, w_in, w_out, lam_q1, lam_k1, lam_q2, lam_k2, a_subln_g, b_q_norm_g, b_w_q_up, b_kv_norm_g, b_w_kv_up, c_mu_prev, c_mu_next, c_w0, c_w2, c_a0, c_a2, c_g2, c_k_k, c_k_a, c_r_k, c_gn_g, c_gn_b, ffn_w_up, ffn_conv_w, ffn_conv_b, ffn_w_down):
    B, T, D = x.shape
    n_ctx = ctx.shape[1]
    S = n_ctx + T
    L = ada_w.shape[0]
    assert n_ctx == ROW_TILE and T % ROW_TILE == 0 and D % LANE == 0

    Xc, Xl = ctx, x
    cond = jnp.concatenate([c, c_ctx[None, :], jnp.zeros((SUBLANE - B - 1, D), F32)], axis=0)
    mod = _modulation(cond, ada_w, ada_b)
    ropeA = _rope_tables(S, n_ctx, A_DIM, 0)
    ropeB = _rope_tables(S, n_ctx, B_ROPE, B_NOPE)
    bd1 = _block_diag_ones(1.0)
    bdm = _block_diag_ones(1.0 / C_DIM)
    row = lambda v: v.reshape(1, -1)

    for i in range(L):
        last = i == L - 1
        t0 = 1 if last else 0
        lam_init = 0.8 - 0.6 * math.exp(-0.3 * i)
        modsel = jnp.stack([jnp.broadcast_to(mod[i, B], (B, 6 * D)), mod[i, :B]], axis=1)[:, :, None, :]
        gq = jnp.pad(b_q_norm_g[i], (0, 2 * LANE - B_Q_RANK)).reshape(1, -1)
        qa, ka, va, qb, kb, vb, pr = _inproj(
            Xc, Xl, modsel, row(mix_pre_g[i]), _layout_w_in(w_in[i]), ropeA, ropeB,
            gq, row(b_kv_norm_g[i]), _layout_wq(b_w_q_up[i]), _layout_wkv(b_w_kv_up[i]))
        dargs = (row(lam_q1[i]), row(lam_k1[i]), row(lam_q2[i]), row(lam_k2[i]), row(a_subln_g[i]), lam_init)
        oa = _attention(qa, ka, va, n_ctx, True, dargs)
        ob = _attention(qb, kb, vb, n_ctx, True)
        oa = (oa if last else _attention(qa, ka, va, n_ctx, False, dargs), oa)
        ob = (ob if last else _attention(qb, kb, vb, n_ctx, False), ob)
        loc = _rwkv_local(pr, row(c_mu_prev[i]), row(c_mu_next[i]),
                          c_w0[i], jnp.stack([_layout_lora(c_w2[i, d], d) for d in (0, 1)]),
                          c_a0[i], jnp.stack([_layout_lora(c_a2[i, d], d) for d in (0, 1)]),
                          c_g2[i], row(c_k_k[i]), row(c_k_a[i]), row(c_r_k[i]), bd1)
        yf, yb = _rwkv_state(loc, n_ctx)
        X1, H2 = _outproj(Xc, Xl, oa, ob, (yf, yb, loc[8], loc[9]), row(c_gn_g[i]), row(c_gn_b[i]), bdm, modsel,
                          _layout_w_out(w_out[i]), row(mix_post_g[i]), row(ffn_pre_g[i]), t0)
        Xc = Xl = _ffn(X1, H2, modsel, _bf(ffn_w_up[i]), ffn_conv_w[i], row(ffn_conv_b[i]), _bf(ffn_w_down[i]),
                       row(ffn_post_g[i]), n_ctx, t0)
    return Xl
```

```python
import functools
import math

import jax
import jax.numpy as jnp
import numpy as np
from jax import lax
from jax.experimental import pallas as pl
from jax.experimental.pallas import tpu as pltpu

F32 = jnp.float32
BF16 = jnp.bfloat16

GRID_W = 64
ROPE_BASE = 10000.0
NORM_EPS = 1e-6
SUBLN_EPS = 1e-5
A_HEADS, A_DIM = 4, 64
A_WIDTH = A_HEADS * 2 * A_DIM
B_HEADS, B_NOPE, B_ROPE, B_VDIM = 4, 64, 32, 64
B_Q_RANK, B_KV_RANK = 192, 128
B_WIDTH = B_HEADS * B_VDIM
C_HEADS, C_DIM = 4, 64
C_WIDTH = C_HEADS * C_DIM
C_LORA = 64
C_GATE_LORA = 128
C_GN_EPS = 64e-5
C_COLS = 3 * C_WIDTH + 4 * C_LORA + C_GATE_LORA

LANE = 128
SUBLANE = 8
ROW_TILE = 256
Q_TILE = 512
LAT_BLOCK = 512
LOG2E = 1.4426950408889634
CHUNK = 64
VMEM_LIMIT = 56 * 1024 * 1024

P_AQ, P_AK, P_AV = 0, 512, 1024
P_BQ = 1536
P_BKV = 1792
P_C = 1920
P_TOTAL = 3072


def _bf(x):
    return x.astype(BF16)


def _dot(a, b, ca=1, cb=0):
    return lax.dot_general(a, b, (((ca,), (cb,)), ((), ())), preferred_element_type=F32)


def _mm(a, b):
    return _dot(_bf(a), _bf(b))


def _mm_nt(a, b):
    return _dot(_bf(a), _bf(b), 1, 1)


def _mm_tn(a, b):
    return _dot(_bf(a.T), _bf(b))


def _split2(x):
    hi = _bf(x)
    lo = _bf(x - hi.astype(F32))
    return hi, lo


def _mm_hl(a, b):
    hi, lo = _split2(a)
    return _dot(hi, b) + _dot(lo, b)


def _mm3(a, b):
    ah, al = _split2(a)
    bh, bl = _split2(b)
    return _dot(ah, bh) + (_dot(ah, bl) + _dot(al, bh))


def _rms(x, eps):
    return x * lax.rsqrt(jnp.mean(x * x, axis=-1, keepdims=True) + eps)


def _sigmoid(x):
    return 1.0 / (1.0 + jnp.exp(-x))


def _silu(x):
    return x * _sigmoid(x)


def _softplus(x):
    return jnp.maximum(x, 0.0) + jnp.log(1.0 + jnp.exp(-jnp.abs(x)))


def _lane_iota(shape):
    return lax.broadcasted_iota(jnp.int32, shape, len(shape) - 1)


def _row_iota(shape):
    return lax.broadcasted_iota(jnp.int32, shape, len(shape) - 2)


def _rope(x, cos, sin, half):
    n = x.shape[-1]
    up = pltpu.roll(x, n - half, axis=1)
    dn = pltpu.roll(x, half, axis=1)
    first = (_lane_iota(x.shape) & half) == 0
    return x * cos + jnp.where(first, up, dn) * sin


def _params(sem):
    return pltpu.CompilerParams(dimension_semantics=sem, vmem_limit_bytes=VMEM_LIMIT)


def _full(shape):
    nd = len(shape)
    return pl.BlockSpec(shape, lambda *_: (0,) * nd)


def _batch_rows(B):
    return 2 if B % 2 == 0 else 1


def _per_batch_row(tile_fn, is_row):
    def kern(*refs):
        n = next(r for r, m in zip(refs, is_row) if m).shape[0]
        for bi in range(n):
            tile_fn(*[r.at[pl.ds(bi, 1)] if m else r for r, m in zip(refs, is_row)])
    return kern


def _mod_kernel(c_ref, w_ref, b_ref, o_ref):
    act = _silu(c_ref[...])
    o_ref[0] = _mm3(act, w_ref[0]) + b_ref[0]


def _modulation(cond, ada_w, ada_b):
    L, D, N = ada_w.shape
    R = cond.shape[0]
    tn = 1024
    return pl.pallas_call(
        _mod_kernel,
        out_shape=jax.ShapeDtypeStruct((L, R, N), F32),
        grid=(L, N // tn),
        in_specs=[pl.BlockSpec((R, D), lambda l, j: (0, 0)),
                  pl.BlockSpec((1, D, tn), lambda l, j: (l, 0, j)),
                  pl.BlockSpec((1, 1, tn), lambda l, j: (l, 0, j))],
        out_specs=pl.BlockSpec((1, R, tn), lambda l, j: (l, 0, j)),
        compiler_params=_params(("parallel", "parallel")),
        name="adaln_mod",
    )(cond, ada_w, ada_b.reshape(L, 1, N))


def _inproj_kernel(xc_ref, xl_ref, mod_ref, g_ref, w_ref, ca_ref, sa_ref, cb_ref, sb_ref,
                   gq_ref, gkv_ref, wq_ref, wkv_ref,
                   qa_ref, ka_ref, va_ref, qb_ref, kb_ref, vb_ref, pr_ref):
    D = xc_ref.shape[-1]
    x = jnp.where(pl.program_id(1) == 0, xc_ref[0], xl_ref[0])
    mod = mod_ref[0, 0]
    sh1, sc1 = mod[:, 0:D], mod[:, D:2 * D]
    h = _bf(_rms(x, NORM_EPS) * g_ref[...] * (1.0 + sc1) + sh1)

    def proj(lo, hi):
        return _dot(h, w_ref[:, lo:hi])

    ca, sa = ca_ref[...], sa_ref[...]
    cb, sb = cb_ref[...], sb_ref[...]

    pbq = proj(P_BQ, P_BKV)
    ckv = proj(P_BKV, P_C)

    pq = proj(P_AQ, P_AK)
    pk = proj(P_AK, P_AV)
    for j in range(A_WIDTH // LANE):
        sl = slice(j * LANE, (j + 1) * LANE)
        qa_ref[0, :, sl] = _bf(_rope(pq[:, sl], ca, sa, A_DIM // 4) * (A_DIM ** -0.5 * LOG2E))
        ka_ref[0, :, sl] = _bf(_rope(pk[:, sl], ca, sa, A_DIM // 4))
    va_ref[0] = _bf(proj(P_AV, P_BQ))

    pr_ref[0] = proj(P_C, P_TOTAL)

    lane = _lane_iota(pbq.shape)
    cq = jnp.where(lane < B_Q_RANK, pbq, 0.0)
    cqn = cq * lax.rsqrt(jnp.sum(cq * cq, axis=-1, keepdims=True) * (1.0 / B_Q_RANK) + NORM_EPS) * gq_ref[...]
    qb = _mm(cqn, wq_ref[...]) * ((B_NOPE + B_ROPE) ** -0.5 * LOG2E)
    ckvn = _rms(ckv, NORM_EPS) * gkv_ref[...]
    kv = _mm(ckvn, wkv_ref[...])
    krb = pbq[:, LANE:2 * LANE]
    l1 = _lane_iota(krb.shape)
    kr = _rope(jnp.where((l1 >= B_NOPE) & (l1 < B_NOPE + B_ROPE), krb, 0.0), cb, sb, B_ROPE // 4)
    for j in range(B_HEADS):
        sl = slice(j * LANE, (j + 1) * LANE)
        qb_ref[0, :, sl] = _bf(_rope(qb[:, sl], cb, sb, B_ROPE // 4))
        kb_ref[0, :, sl] = _bf(kv[:, sl] + kr)
        vh = kv[:, B_HEADS * LANE + j * LANE:B_HEADS * LANE + (j + 1) * LANE]
        vb_ref[0, :, sl] = _bf(jnp.where(l1 < B_VDIM, vh, 1.0))


def _stream_specs(nb, tm, D, first, t_start=0):
    return [pl.BlockSpec((nb, tm, D), lambda b, i: (b, 0, 0)),
            pl.BlockSpec((nb, tm, D), lambda b, i: (b, jnp.maximum(i + t_start - first, 0), 0))]


def _inproj(Xc, Xl, modsel, g, w_in, ropeA, ropeB, gq, gkv, wq, wkv):
    B, _, D = Xc.shape
    tm = ROW_TILE
    first = 0 if Xl is Xc else Xc.shape[1] // tm
    S = Xl.shape[1] + first * tm
    nb = _batch_rows(B)
    row = lambda w: pl.BlockSpec((nb, tm, w), lambda b, i: (b, i, 0))
    tab = pl.BlockSpec((tm, LANE), lambda b, i: (i, 0))
    outs = [jax.ShapeDtypeStruct((B, S, 512), BF16)] * 6 + [jax.ShapeDtypeStruct((B, S, C_COLS), F32)]
    return pl.pallas_call(
        _per_batch_row(_inproj_kernel, [True, True, True] + [False] * 10 + [True] * 7),
        out_shape=outs,
        grid=(B // nb, S // tm),
        in_specs=_stream_specs(nb, tm, D, first) + [
                  pl.BlockSpec((nb, 1, 1, modsel.shape[-1]), lambda b, i: (b, jnp.minimum(i, 1), 0, 0)),
                  _full((1, D)), _full(w_in.shape), tab, tab, tab, tab,
                  _full(gq.shape), _full(gkv.shape), _full(wq.shape), _full(wkv.shape)],
        out_specs=[row(512)] * 6 + [row(C_COLS)],
        compiler_params=_params(("parallel", "parallel")),
        name="in_proj",
    )(Xc, Xl, modsel, g, w_in, ropeA[0], ropeA[1], ropeB[0], ropeB[1], gq, gkv, wq, wkv)


def _attn_kernel(diff, lam_init, n_ctx, n_lat, n_q, *refs):
    q_refs, (k_ref, v_ref), rest = refs[:n_q], refs[n_q:n_q + 2], refs[n_q + 2:]
    if diff:
        lq1, lk1, lq2, lk2, g_ref, o_ref, sc_ref, sl_ref = rest
    else:
        o_ref, sc_ref, sl_ref = rest
    q = jnp.concatenate([r[0] for r in q_refs], axis=0) if n_q > 1 else q_refs[0][0]
    tq = q.shape[0]
    lane = _lane_iota((tq, LANE))
    if diff:
        zero = jnp.zeros_like(q)
        maps = [(jnp.where(lane < A_DIM, q, zero), slice(0, LANE)), (jnp.where(lane >= A_DIM, q, zero), slice(0, LANE))]
    else:
        maps = [(q[:, h * LANE:(h + 1) * LANE], slice(h * LANE, (h + 1) * LANE)) for h in range(q.shape[1] // LANE)]
    blocks = [(0, n_ctx, lambda mi: sc_ref.at[mi])]
    blocks += [(n_ctx + j * LAT_BLOCK, LAT_BLOCK, lambda mi, j=j: sl_ref.at[mi * n_lat + j]) for j in range(n_lat)]

    ms = []
    for mi, (qm, ksl) in enumerate(maps):
        mrun = jnp.full((tq, LANE), -jnp.inf, F32)
        for off, size, buf in blocks:
            s = _dot(qm, k_ref[0, off:off + size, ksl], 1, 1)
            buf(mi)[...] = s
            for c in range(size // LANE):
                mrun = jnp.maximum(mrun, s[:, c * LANE:(c + 1) * LANE])
        ms.append(jnp.max(mrun, axis=-1, keepdims=True))

    outs = []
    for mi, (qm, ksl) in enumerate(maps):
        acc = None
        for off, size, buf in blocks:
            v = v_ref[0, off:off + size, ksl]
            if diff:
                v = jnp.concatenate([v, jnp.ones_like(v)], axis=1)
            pv = _dot(_bf(jnp.exp2(buf(mi)[...] - ms[mi])), v)
            acc = pv if acc is None else acc + pv
        if diff:
            outs.append(acc[:, :LANE] / acc[:, LANE:])
        else:
            outs.append(jnp.where(lane < B_VDIM, acc / pltpu.roll(acc, B_VDIM, axis=1), 0.0))

    if diff:
        lam = (jnp.exp(jnp.sum(lq1[...] * lk1[...], axis=-1, keepdims=True))
               - jnp.exp(jnp.sum(lq2[...] * lk2[...], axis=-1, keepdims=True)) + lam_init)
        o = outs[0] - lam * outs[1]
        o_ref[0] = _bf(_rms(o, SUBLN_EPS) * g_ref[...] * (1.0 - lam_init))
    else:
        for h, o in enumerate(outs):
            o_ref[0, :, h * LANE:(h + 1) * LANE] = _bf(o)


def _attention(q, k, v, n_ctx, latent, diff_args=None):
    B, S, _ = k.shape
    tq = Q_TILE if latent else ROW_TILE
    tb = ROW_TILE
    n_q = tq // tb
    t0 = n_ctx // tb if latent else 0
    nq = (S - n_ctx) // tq if latent else n_ctx // tq
    n_keys = S if latent else n_ctx
    diff = diff_args is not None
    bw = LANE if diff else 2 * LANE
    assert (n_keys - n_ctx) % LAT_BLOCK == 0
    n_lat = (n_keys - n_ctx) // LAT_BLOCK
    if diff:
        lq1, lk1, lq2, lk2, subln_g, lam_init = diff_args
        extra = [lq1, lk1, lq2, lk2, subln_g]
        vec = _full((1, A_DIM))
        extra_specs = [vec, vec, vec, vec, _full((1, LANE))]
    else:
        lam_init, extra, extra_specs = None, [], []
    kern = functools.partial(_attn_kernel, diff, lam_init, n_ctx, n_lat, n_q)
    qspecs = [pl.BlockSpec((1, tb, bw), lambda b, h, i, j=j: (b, t0 + i * n_q + j, h)) for j in range(n_q)]
    kvspec = pl.BlockSpec((1, n_keys, bw), lambda b, h, i: (b, 0, h))
    return pl.pallas_call(
        kern,
        out_shape=jax.ShapeDtypeStruct((B, nq * tq, q.shape[-1]), BF16),
        grid=(B, q.shape[-1] // bw, nq),
        in_specs=qspecs + [kvspec, kvspec] + extra_specs,
        out_specs=pl.BlockSpec((1, tq, bw), lambda b, h, i: (b, i, h)),
        scratch_shapes=[pltpu.VMEM((2, tq, n_ctx), F32),
                        pltpu.VMEM((max(2 * n_lat, 1), tq, LAT_BLOCK), F32)],
        compiler_params=_params(("parallel", "parallel", "parallel")),
        name=("diff_attn" if diff else "mla_attn") + ("" if latent else "_ctx"),
    )(*([q] * n_q), k, v, *extra)


def _rwkv_chunk_index(rev, n_ctx_chunks, n_chunks, j):
    if not rev:
        return j
    return jnp.where(j < n_ctx_chunks, n_ctx_chunks - 1 - j, n_chunks + n_ctx_chunks - 1 - j)


def _rwkv_local_kernel(n_tiles,
                       cur_ref, prv_ref, nxt_ref, mup_ref, mun_ref, w0_ref, w2_ref, a0_ref, a2_ref, g2_ref,
                       kk_ref, ka_ref, rk_ref, bd_ref,
                       qf_ref, y0f_ref, mf_ref, nf_ref, qb_ref, y0b_ref, mb_ref, nb_ref, bonus_ref, gate_ref):
    C, W = CHUNK, C_WIDTH
    NB, R = cur_ref.shape[0], cur_ref.shape[1]
    i = pl.program_id(1)
    seq_first = i <= 1
    seq_last = (i == 0) | (i == n_tiles - 1)

    rr = _row_iota((R, R))
    cr = _lane_iota((R, R))
    same_chunk = (rr >> 6) == (cr >> 6)
    rows = _row_iota((R, C_COLS))
    rc = _row_iota((C, W))
    cc = _lane_iota((C, W)) & (C - 1)
    eye = jnp.where(rc == cc, 1.0, 0.0)
    pair = [((rc >> (lvl + 1)) == (cc >> (lvl + 1))) & ((rc >> lvl) != (cc >> lvl))
            for lvl in range(int(math.log2(C)))]
    same_head = (_row_iota((W, W)) >> 6) == (_lane_iota((W, W)) >> 6)
    bd = bd_ref[...]
    outs = ((qf_ref, y0f_ref, mf_ref, nf_ref), (qb_ref, y0b_ref, mb_ref, nb_ref))

    chunks = []
    for bi in range(NB):
        x = cur_ref[bi]
        prev_row = jnp.where(seq_first, 0.0, prv_ref[bi, SUBLANE - 1:SUBLANE, :])
        next_row = jnp.where(seq_last, 0.0, nxt_ref[bi, 0:1, :])
        xp = jnp.where(rows == 0, prev_row, pltpu.roll(x, 1, axis=0))
        xn = jnp.where(rows == R - 1, next_row, pltpu.roll(x, R - 1, axis=0))
        xs = x + mup_ref[...] * (xp - x) + mun_ref[...] * (xn - x)

        r, k, v = xs[:, 0:W], xs[:, W:2 * W], xs[:, 2 * W:3 * W]
        wl = jnp.tanh(xs[:, 3 * W:3 * W + 2 * C_LORA])
        al = xs[:, 3 * W + 2 * C_LORA:3 * W + 4 * C_LORA]
        gl = xs[:, 3 * W + 4 * C_LORA:]
        kkr = k * kk_ref[...]
        kk = kkr / jnp.maximum(jnp.sqrt(_mm_hl(kkr * kkr, bd)), 1e-12)
        gate_ref[bi] = _mm(_sigmoid(gl), g2_ref[...])

        bonus = jnp.zeros((R, W), F32)
        for d, rev in enumerate((False, True)):
            w = -_softplus(-(w0_ref[d:d + 1] + _mm3(wl, w2_ref[d]))) - 0.5
            lw = -jnp.exp(w)
            a_ic = _sigmoid(a0_ref[d:d + 1] + _mm3(al, a2_ref[d]))
            kd = k * (1.0 + (a_ic - 1.0) * ka_ref[...])
            avec = -kk
            bvec = kk * a_ic
            bonus = bonus + _mm_hl(r * kd * rk_ref[...], bd) * v

            tri = jnp.where(same_chunk & ((cr >= rr) if rev else (cr <= rr)), 1.0, 0.0).astype(BF16)
            l_hi = _bf(lw)
            l_md = _bf(lw - l_hi.astype(F32))
            l_lo = _bf(lw - l_hi.astype(F32) - l_md.astype(F32))
            cs = _dot(tri, l_hi) + (_dot(tri, l_md) + _dot(tri, l_lo))
            e_neg = jnp.exp(-cs)
            At_all = avec * jnp.exp(cs - lw)
            Rt_all = r * jnp.exp(cs)
            Bt_all = bvec * e_neg
            Kt_all = kd * e_neg
            for g in range(R // C):
                sl = slice(g * C, (g + 1) * C)
                csg = cs[sl]
                tot = csg[0:1, :] if rev else csg[C - 1:C, :]
                e_rem = jnp.exp(tot - csg)
                chunks.append(dict(bi=bi, d=d, g=g, rev=rev, sl=sl, At=At_all[sl], Rt=Rt_all[sl], v=v[sl], tot=tot,
                                   Bt=Bt_all[sl], Kt=Kt_all[sl], Bg=bvec[sl] * e_rem, Kg=kd[sl] * e_rem))
        bonus_ref[bi] = bonus

    def heads_bd(x):
        xb = _bf(x)
        return jnp.where(same_head, jnp.concatenate([xb] * C_HEADS, axis=0), jnp.zeros((W, W), BF16))

    def each(fn, *lists):
        return [fn(*args) for args in zip(*lists)]

    X = [jnp.concatenate([ch["At"], ch["Rt"]], axis=0) for ch in chunks]
    GB = each(lambda x, ch: _dot(_bf(x), heads_bd(ch["Bt"]), 1, 1), X, chunks)
    GK = each(lambda x, ch: _dot(_bf(x), heads_bd(ch["Kt"]), 1, 1), X, chunks)
    strict = {False: cc < rc, True: cc > rc}
    incl = {False: cc <= rc, True: cc >= rc}
    Aab = each(lambda g_, ch: jnp.where(strict[ch["rev"]], g_[0:C], 0.0), GB, chunks)
    Arb = each(lambda g_, ch: jnp.where(incl[ch["rev"]], g_[C:], 0.0), GB, chunks)
    Aak = each(lambda g_, ch: jnp.where(strict[ch["rev"]], g_[0:C], 0.0), GK, chunks)
    Ark = each(lambda g_, ch: jnp.where(incl[ch["rev"]], g_[C:], 0.0), GK, chunks)
    T = [eye + jnp.where(pair[0], a, 0.0) for a in Aab]
    for lvl in range(1, len(pair)):
        P = each(lambda t, a: _dot(_bf(t), heads_bd(jnp.where(pair[lvl], a, 0.0))), T, Aab)
        T = each(lambda t, p: t + _dot(_bf(p), heads_bd(t)), T, P)
    Vbd = [heads_bd(ch["v"]) for ch in chunks]
    AkV = each(lambda a, vb: _dot(_bf(a), vb), Aak, Vbd)
    Wt = each(lambda t, ch: _dot(_bf(t), heads_bd(ch["At"])), T, chunks)
    U0 = each(lambda t, akv: _dot(_bf(t), heads_bd(akv)), T, AkV)
    Q = each(lambda a, wt, ch: ch["Rt"] + _dot(_bf(a), heads_bd(wt)), Arb, Wt, chunks)
    Y0 = each(lambda arb, u0, ark, vb: _dot(_bf(arb), heads_bd(u0)) + _dot(_bf(ark), vb), Arb, U0, Ark, Vbd)

    for ch, wt, u0, q, y0 in zip(chunks, Wt, U0, Q, Y0):
        q_ref, y0_ref, m_ref, n_ref = outs[ch["d"]]
        q_ref[ch["bi"], ch["sl"], :] = _bf(q)
        y0_ref[ch["bi"], ch["sl"], :] = y0
        Mbd = jnp.where(same_head, _mm_tn(wt, ch["Bg"]), 0.0)
        Nbd = jnp.where(same_head, _mm_tn(u0, ch["Bg"]) + _mm_tn(ch["v"], ch["Kg"]), 0.0)
        Mc = Mbd[0:C] + Mbd[C:2 * C] + Mbd[2 * C:3 * C] + Mbd[3 * C:]
        m_ref[ch["bi"], ch["g"]] = _bf(Mc + jnp.where(rc == cc, jnp.exp(ch["tot"]), 0.0))
        n_ref[ch["bi"], ch["g"]] = Nbd[0:C] + Nbd[C:2 * C] + Nbd[2 * C:3 * C] + Nbd[3 * C:]


def _rwkv_local(pr, mu_prev, mu_next, w0, w2p, a0, a2p, g2, k_k, k_a, r_k, bd):
    B, S, _ = pr.shape
    R, C, W = ROW_TILE, CHUNK, C_WIDTH
    NB = 2 if B % 2 == 0 else 1
    nt, gpt = S // R, R // C
    bpt = R // SUBLANE
    nblk = S // SUBLANE
    rowspec = lambda w: pl.BlockSpec((NB, R, w), lambda b, i: (b, i, 0))
    mspec = pl.BlockSpec((NB, gpt, C, W), lambda b, i: (b, i, 0, 0))
    seq = lambda dt: jax.ShapeDtypeStruct((B, S, W), dt)
    mat = lambda dt: jax.ShapeDtypeStruct((B, S // C, C, W), dt)
    per_dir = [seq(BF16), seq(F32), mat(BF16), mat(F32)]
    vec = _full((1, W))
    return pl.pallas_call(
        functools.partial(_rwkv_local_kernel, nt),
        out_shape=per_dir * 2 + [seq(F32), seq(F32)],
        grid=(B // NB, nt),
        in_specs=[rowspec(C_COLS),
                  pl.BlockSpec((NB, SUBLANE, C_COLS), lambda b, i: (b, jnp.maximum(i * bpt - 1, 0), 0)),
                  pl.BlockSpec((NB, SUBLANE, C_COLS), lambda b, i: (b, jnp.minimum((i + 1) * bpt, nblk - 1), 0)),
                  _full((1, C_COLS)), _full((1, C_COLS)), _full(w0.shape), _full(w2p.shape), _full(a0.shape),
                  _full(a2p.shape), _full(g2.shape), vec, vec, vec, _full(bd.shape)],
        out_specs=[rowspec(W), rowspec(W), mspec, mspec] * 2 + [rowspec(W), rowspec(W)],
        compiler_params=_params(("parallel", "parallel")),
        name="rwkv_local",
    )(pr, pr, pr, mu_prev, mu_next, w0, w2p, a0, a2p, g2, k_k, k_a, r_k, bd)


def _rwkv_state_kernel(qf_ref, y0f_ref, mf_ref, nf_ref, qb_ref, y0b_ref, mb_ref, nb_ref, yf_ref, yb_ref, s_ref):
    C, W = CHUNK, C_WIDTH
    nb, G = mf_ref.shape[0], mf_ref.shape[1]

    @pl.when(pl.program_id(0) == 0)
    def _():
        s_ref[...] = jnp.zeros(s_ref.shape, F32)

    head = _lane_iota((C, W)) >> 6

    def expand(mc):
        return jnp.concatenate([jnp.where(head == h, mc, jnp.zeros_like(mc)) for h in range(C_HEADS)], axis=0)

    dirs = ((qf_ref, y0f_ref, mf_ref, nf_ref, yf_ref, range(G)),
            (qb_ref, y0b_ref, mb_ref, nb_ref, yb_ref, range(G - 1, -1, -1)))
    for step in range(G):
        for d, (q_ref, y0_ref, m_ref, n_ref, y_ref, order) in enumerate(dirs):
            g = order[step]
            sl = slice(g * C, (g + 1) * C)
            for b in range(nb):
                S = _bf(s_ref[d, b])
                y_ref[b, sl, :] = _dot(q_ref[b, sl, :], S, 1, 1) + y0_ref[b, sl, :]
                s_ref[d, b] = _dot(S, expand(m_ref[b, g])) + expand(n_ref[b, g])


def _rwkv_state(loc, n_ctx):
    qf = loc[0]
    B, S, W = qf.shape
    C = CHUNK
    G = 2
    nblk, ncb = S // (G * C), n_ctx // (G * C)
    fwd = lambda j: j
    bwd = functools.partial(_rwkv_chunk_index, True, ncb, nblk)
    seqspec = lambda idx: pl.BlockSpec((B, G * C, W), lambda j: (0, idx(j), 0))
    matspec = lambda idx: pl.BlockSpec((B, G, C, W), lambda j: (0, idx(j), 0, 0))
    out = jax.ShapeDtypeStruct((B, S, W), F32)
    return pl.pallas_call(
        _rwkv_state_kernel,
        out_shape=[out, out],
        grid=(nblk,),
        in_specs=[seqspec(fwd), seqspec(fwd), matspec(fwd), matspec(fwd),
                  seqspec(bwd), seqspec(bwd), matspec(bwd), matspec(bwd)],
        out_specs=[seqspec(fwd), seqspec(bwd)],
        scratch_shapes=[pltpu.VMEM((2, B, W, W), F32)],
        compiler_params=_params(("arbitrary",)),
        name="rwkv_state",
    )(*loc[:8])


def _outproj_kernel(t_start, xc_ref, xl_ref, oac_ref, oal_ref, obc_ref, obl_ref,
                    yf_ref, yb_ref, bonus_ref, gate_ref, gng_ref, gnb_ref, bdm_ref,
                    mod_ref, w_ref, gpost_ref, gpre_ref, x1_ref, h2_ref):
    D = xc_ref.shape[-1]
    is_ctx = pl.program_id(1) + t_start == 0
    x = jnp.where(is_ctx, xc_ref[0], xl_ref[0])
    oa = jnp.where(is_ctx, oac_ref[0], oal_ref[0])
    ob = jnp.where(is_ctx, obc_ref[0], obl_ref[0])
    mod = mod_ref[0, 0]
    gt1 = mod[:, 2 * D:3 * D]
    sh2, sc2 = mod[:, 3 * D:4 * D], mod[:, 4 * D:5 * D]
    y = yf_ref[0] + yb_ref[0]
    bdm = bdm_ref[...]
    mu = _mm_hl(y, bdm)
    dev = y - mu
    var = _mm_hl(dev * dev, bdm)
    yn = dev * lax.rsqrt(var + C_GN_EPS) * gng_ref[...] + gnb_ref[...]
    oc = _bf((yn + bonus_ref[0]) * gate_ref[0])
    nb = ob.shape[-1]
    o = (_dot(oa, w_ref[0:A_WIDTH, :])
         + _dot(ob, w_ref[A_WIDTH:A_WIDTH + nb, :])
         + _dot(oc, w_ref[A_WIDTH + nb:, :]))
    x1 = x + gt1 * (_rms(o, NORM_EPS) * gpost_ref[...])
    x1_ref[0] = x1
    h2_ref[0] = _bf(_rms(x1, NORM_EPS) * gpre_ref[...] * (1.0 + sc2) + sh2)


def _outproj(Xc, Xl, oa, ob, rwkv, gn_g, gn_b, bdm, modsel, w_out, g_post, g_pre, t_start):
    B, _, D = Xc.shape
    tm = ROW_TILE
    first = 0 if Xl is Xc else Xc.shape[1] // tm
    S = rwkv[0].shape[1]
    nt = S // tm - t_start
    nb = 1
    row = lambda w: pl.BlockSpec((nb, tm, w), lambda b, i: (b, i, 0))
    full_row = lambda w: pl.BlockSpec((nb, tm, w), lambda b, i: (b, i + t_start, 0))
    pair = lambda w: _stream_specs(nb, tm, w, 1, t_start)
    W = C_WIDTH
    return pl.pallas_call(
        _per_batch_row(functools.partial(_outproj_kernel, t_start),
                       [True] * 10 + [False] * 3 + [True] + [False] * 3 + [True] * 2),
        out_shape=[jax.ShapeDtypeStruct((B, nt * tm, D), F32), jax.ShapeDtypeStruct((B, nt * tm, D), BF16)],
        grid=(B // nb, nt),
        in_specs=_stream_specs(nb, tm, D, first, t_start) + pair(A_WIDTH) + pair(ob[1].shape[-1])
                 + [full_row(W)] * 4
                 + [_full((1, W)), _full((1, W)), _full((W, W)),
                    pl.BlockSpec((nb, 1, 1, modsel.shape[-1]), lambda b, i: (b, jnp.minimum(i + t_start, 1), 0, 0)),
                    _full(w_out.shape), _full((1, D)), _full((1, D))],
        out_specs=[row(D), row(D)],
        compiler_params=_params(("parallel", "parallel")),
        name="out_proj",
    )(Xc, Xl, *oa, *ob, *rwkv, gn_g, gn_b, bdm, modsel, w_out, g_post, g_pre)


def _ffn_kernel(n_ctx_tiles, n_tiles, t_start, ff_tile,
                x_ref, h_ref, hp_ref, hn_ref, mod_ref, wup_ref, cw_ref, cb_ref, wdn_ref, gpost_ref, o_ref,
                act_ref):
    D = x_ref.shape[-1]
    tm = x_ref.shape[1]
    dff = wdn_ref.shape[0]
    i = pl.program_id(1) + t_start
    seq_first = (i == 0) | (i == n_ctx_tiles)
    seq_last = (i == n_ctx_tiles - 1) | (i == n_tiles - 1)
    hp = hp_ref[0]
    hn = hn_ref[0]
    hp = jnp.where(seq_first, jnp.zeros_like(hp), hp)
    hn = jnp.where(seq_last, jnp.zeros_like(hn), hn)
    hext = jnp.concatenate([hp, h_ref[0], hn], axis=0)
    halo = hp.shape[0]
    for f in range(dff // ff_tile):
        parts = []
        for base in (0, dff):
            lo = base + f * ff_tile
            u = _dot(hext, wup_ref[:, lo:lo + ff_tile])
            cw = cw_ref[:, lo:lo + ff_tile]
            n = u.shape[0]
            up = pltpu.roll(u, 1, axis=0)[halo:halo + tm]
            un = pltpu.roll(u, n - 1, axis=0)[halo:halo + tm]
            parts.append(up * cw[0:1] + u[halo:halo + tm] * cw[1:2] + un * cw[2:3]
                         + cb_ref[:, lo:lo + ff_tile])
        act_ref[0, :, f * ff_tile:(f + 1) * ff_tile] = _bf(_silu(parts[0]) * parts[1])
    acc = _dot(act_ref[0], wdn_ref[...])
    gt2 = mod_ref[0, 0][:, 5 * D:6 * D]
    o_ref[0] = x_ref[0] + gt2 * (_rms(acc, NORM_EPS) * gpost_ref[...])


def _ffn(X1, H2, modsel, w_up, conv_w, conv_b, w_dn, g_post, n_ctx, t_start):
    B, S, D = X1.shape
    tm = ROW_TILE
    halo = 16
    nt = S // tm
    hb = tm // halo
    nhb = S // halo
    nb = _batch_rows(B)
    kern = _per_batch_row(functools.partial(_ffn_kernel, n_ctx // tm, nt + t_start, t_start, 256),
                          [True] * 5 + [False] * 5 + [True] * 2)
    row = pl.BlockSpec((nb, tm, D), lambda b, i: (b, i, 0))
    once = lambda shape: pl.BlockSpec(shape, lambda b, i: (0, 0), pipeline_mode=pl.Buffered(1))
    return pl.pallas_call(
        kern,
        out_shape=jax.ShapeDtypeStruct((B, S, D), F32),
        grid=(B // nb, nt),
        in_specs=[row, row,
                  pl.BlockSpec((nb, halo, D), lambda b, i: (b, jnp.maximum(i * hb - 1, 0), 0)),
                  pl.BlockSpec((nb, halo, D), lambda b, i: (b, jnp.minimum((i + 1) * hb, nhb - 1), 0)),
                  pl.BlockSpec((nb, 1, 1, modsel.shape[-1]), lambda b, i: (b, jnp.minimum(i + t_start, 1), 0, 0)),
                  once(w_up.shape), _full(conv_w.shape), _full(conv_b.shape), once(w_dn.shape), _full((1, D))],
        out_specs=row,
        scratch_shapes=[pltpu.VMEM((nb, tm, w_dn.shape[0]), BF16)],
        compiler_params=_params(("parallel", "parallel")),
        name="conv_ffn",
    )(X1, H2, H2, H2, modsel, w_up, conv_w, conv_b, w_dn, g_post)


def _rope_tables(S, n_ctx, dim, lane_lo):
    nf = dim // 4
    t = jnp.arange(S - n_ctx, dtype=jnp.int32)
    rows = (t // GRID_W).astype(F32)
    cols = (t % GRID_W).astype(F32)
    inv = ROPE_BASE ** (-jnp.arange(nf, dtype=F32) / nf)
    ar, ac = rows[:, None] * inv, cols[:, None] * inv
    cos = jnp.concatenate([jnp.cos(ar), jnp.cos(ar), jnp.cos(ac), jnp.cos(ac)], axis=-1)
    sin = jnp.concatenate([-jnp.sin(ar), jnp.sin(ar), -jnp.sin(ac), jnp.sin(ac)], axis=-1)
    if lane_lo == 0:
        reps = LANE // dim
        cos, sin = jnp.tile(cos, (1, reps)), jnp.tile(sin, (1, reps))
    else:
        pad = ((0, 0), (lane_lo, LANE - lane_lo - dim))
        cos = jnp.pad(cos, pad, constant_values=1.0)
        sin = jnp.pad(sin, pad)
    cos = jnp.concatenate([jnp.ones((n_ctx, LANE), F32), cos], axis=0)
    sin = jnp.concatenate([jnp.zeros((n_ctx, LANE), F32), sin], axis=0)
    return cos, sin


def _layout_w_in(w):
    D = w.shape[0]
    a = w[:, :3 * A_WIDTH]
    o = 3 * A_WIDTH
    cq = w[:, o:o + B_Q_RANK]
    ckv = w[:, o + B_Q_RANK:o + B_Q_RANK + B_KV_RANK]
    kr = w[:, o + B_Q_RANK + B_KV_RANK:o + B_Q_RANK + B_KV_RANK + B_ROPE]
    c = w[:, o + B_Q_RANK + B_KV_RANK + B_ROPE:]
    z = jnp.zeros((D, 2 * LANE - B_Q_RANK - B_ROPE), w.dtype)
    return _bf(jnp.concatenate([a, cq, kr, z, ckv, c], axis=1))


def _layout_wq(w):
    hd = B_NOPE + B_ROPE
    w = w.reshape(B_Q_RANK, B_HEADS, hd)
    w = jnp.pad(w, ((0, 2 * LANE - B_Q_RANK), (0, 0), (0, LANE - hd)))
    return _bf(w.reshape(2 * LANE, B_HEADS * LANE))


def _layout_wkv(w):
    w = w.reshape(B_KV_RANK, B_HEADS, B_NOPE + B_VDIM)
    pad = lambda t: jnp.pad(t, ((0, 0), (0, 0), (0, LANE - t.shape[-1]))).reshape(B_KV_RANK, B_HEADS * LANE)
    return _bf(jnp.concatenate([pad(w[:, :, :B_NOPE]), pad(w[:, :, B_NOPE:])], axis=1))


def _layout_w_out(w):
    D = w.shape[1]
    wb = w[A_WIDTH:A_WIDTH + B_WIDTH].reshape(B_HEADS, B_VDIM, D)
    wb = jnp.pad(wb, ((0, 0), (0, LANE - B_VDIM), (0, 0))).reshape(B_HEADS * LANE, D)
    return _bf(jnp.concatenate([w[:A_WIDTH], wb, w[A_WIDTH + B_WIDTH:]], axis=0))


def _layout_lora(w, d):
    z = jnp.zeros_like(w)
    return jnp.concatenate([z, w] if d else [w, z], axis=0)


def _block_diag_ones(scale):
    i = np.arange(C_WIDTH) // C_DIM
    return jnp.asarray((i[:, None] == i[None, :]).astype(np.float32) * scale, dtype=BF16)


def kernel(x, c, ctx, c_ctx, ada_w, ada_b, mix_pre_g, mix_post_g, ffn_pre_g, ffn_post_g, w_in, w_out, lam_q1, lam_k1, lam_q2, lam_k2, a_subln_g, b_q_norm_g, b_w_q_up, b_kv_norm_g, b_w_kv_up, c_mu_prev, c_mu_next, c_w0, c_w2, c_a0, c_a2, c_g2, c_k_k, c_k_a, c_r_k, c_gn_g, c_gn_b, ffn_w_up, ffn_conv_w, ffn_conv_b, ffn_w_down):
    B, T, D = x.shape
    n_ctx = ctx.shape[1]
    S = n_ctx + T
    L = ada_w.shape[0]
    assert n_ctx == ROW_TILE and T % ROW_TILE == 0 and D % LANE == 0

    Xc, Xl = ctx, x
    cond = jnp.concatenate([c, c_ctx[None, :], jnp.zeros((SUBLANE - B - 1, D), F32)], axis=0)
    mod = _modulation(cond, ada_w, ada_b)
    ropeA = _rope_tables(S, n_ctx, A_DIM, 0)
    ropeB = _rope_tables(S, n_ctx, B_ROPE, B_NOPE)
    bd1 = _block_diag_ones(1.0)
    bdm = _block_diag_ones(1.0 / C_DIM)
    row = lambda v: v.reshape(1, -1)

    for i in range(L):
        last = i == L - 1
        t0 = 1 if last else 0
        lam_init = 0.8 - 0.6 * math.exp(-0.3 * i)
        modsel = jnp.stack([jnp.broadcast_to(mod[i, B], (B, 6 * D)), mod[i, :B]], axis=1)[:, :, None, :]
        gq = jnp.pad(b_q_norm_g[i], (0, 2 * LANE - B_Q_RANK)).reshape(1, -1)
        qa, ka, va, qb, kb, vb, pr = _inproj(
            Xc, Xl, modsel, row(mix_pre_g[i]), _layout_w_in(w_in[i]), ropeA, ropeB,
            gq, row(b_kv_norm_g[i]), _layout_wq(b_w_q_up[i]), _layout_wkv(b_w_kv_up[i]))
        dargs = (row(lam_q1[i]), row(lam_k1[i]), row(lam_q2[i]), row(lam_k2[i]), row(a_subln_g[i]), lam_init)
        oa = _attention(qa, ka, va, n_ctx, True, dargs)
        ob = _attention(qb, kb, vb, n_ctx, True)
        oa = (oa if last else _attention(qa, ka, va, n_ctx, False, dargs), oa)
        ob = (ob if last else _attention(qb, kb, vb, n_ctx, False), ob)
        loc = _rwkv_local(pr, row(c_mu_prev[i]), row(c_mu_next[i]),
                          c_w0[i], jnp.stack([_layout_lora(c_w2[i, d], d) for d in (0, 1)]),
                          c_a0[i], jnp.stack([_layout_lora(c_a2[i, d], d) for d in (0, 1)]),
                          c_g2[i], row(c_k_k[i]), row(c_k_a[i]), row(c_r_k[i]), bd1)
        yf, yb = _rwkv_state(loc, n_ctx)
        X1, H2 = _outproj(Xc, Xl, oa, ob, (yf, yb, loc[8], loc[9]), row(c_gn_g[i]), row(c_gn_b[i]), bdm, modsel,
                          _layout_w_out(w_out[i]), row(mix_post_g[i]), row(ffn_pre_g[i]), t0)
        Xc = Xl = _ffn(X1, H2, modsel, _bf(ffn_w_up[i]), ffn_conv_w[i], row(ffn_conv_b[i]), _bf(ffn_w_down[i]),
                       row(ffn_post_g[i]), n_ctx, t0)
    return Xl
```

```python
import functools
import math

import jax
import jax.numpy as jnp
import numpy as np
from jax import lax
from jax.experimental import pallas as pl
from jax.experimental.pallas import tpu as pltpu

F32 = jnp.float32
BF16 = jnp.bfloat16

GRID_W = 64
ROPE_BASE = 10000.0
NORM_EPS = 1e-6
SUBLN_EPS = 1e-5
A_HEADS, A_DIM = 4, 64
A_WIDTH = A_HEADS * 2 * A_DIM
B_HEADS, B_NOPE, B_ROPE, B_VDIM = 4, 64, 32, 64
B_Q_RANK, B_KV_RANK = 192, 128
B_WIDTH = B_HEADS * B_VDIM
C_HEADS, C_DIM = 4, 64
C_WIDTH = C_HEADS * C_DIM
C_LORA = 64
C_GATE_LORA = 128
C_GN_EPS = 64e-5
C_COLS = 3 * C_WIDTH + 4 * C_LORA + C_GATE_LORA

LANE = 128
SUBLANE = 8
ROW_TILE = 256
Q_TILE = 1024
LAT_BLOCK = 512
LOG2E = 1.4426950408889634
CHUNK = 64
VMEM_LIMIT = 56 * 1024 * 1024

P_AQ, P_AK, P_AV = 0, 512, 1024
P_BQ = 1536
P_BKV = 1792
P_C = 1920
P_TOTAL = 3072


def _bf(x):
    return x.astype(BF16)


def _dot(a, b, ca=1, cb=0):
    return lax.dot_general(a, b, (((ca,), (cb,)), ((), ())), preferred_element_type=F32)


def _mm(a, b):
    return _dot(_bf(a), _bf(b))


def _mm_nt(a, b):
    return _dot(_bf(a), _bf(b), 1, 1)


def _mm_tn(a, b):
    return _dot(_bf(a.T), _bf(b))


def _split2(x):
    hi = _bf(x)
    lo = _bf(x - hi.astype(F32))
    return hi, lo


def _mm_hl(a, b):
    hi, lo = _split2(a)
    return _dot(hi, b) + _dot(lo, b)


def _mm3(a, b):
    ah, al = _split2(a)
    bh, bl = _split2(b)
    return _dot(ah, bh) + (_dot(ah, bl) + _dot(al, bh))


def _rms(x, eps):
    return x * lax.rsqrt(jnp.mean(x * x, axis=-1, keepdims=True) + eps)


def _sigmoid(x):
    return 1.0 / (1.0 + jnp.exp(-x))


def _silu(x):
    return x * _sigmoid(x)


def _softplus(x):
    return jnp.maximum(x, 0.0) + jnp.log(1.0 + jnp.exp(-jnp.abs(x)))


def _lane_iota(shape):
    return lax.broadcasted_iota(jnp.int32, shape, len(shape) - 1)


def _row_iota(shape):
    return lax.broadcasted_iota(jnp.int32, shape, len(shape) - 2)


def _rope(x, cos, sin, half):
    n = x.shape[-1]
    up = pltpu.roll(x, n - half, axis=1)
    dn = pltpu.roll(x, half, axis=1)
    first = (_lane_iota(x.shape) & half) == 0
    return x * cos + jnp.where(first, up, dn) * sin


def _params(sem):
    return pltpu.CompilerParams(dimension_semantics=sem, vmem_limit_bytes=VMEM_LIMIT)


def _full(shape):
    nd = len(shape)
    return pl.BlockSpec(shape, lambda *_: (0,) * nd)


def _batch_rows(B, most=2):
    return max(n for n in (1, 2, 4) if n <= most and B % n == 0)


def _per_batch_row(tile_fn, is_row):
    def kern(*refs):
        n = next(r for r, m in zip(refs, is_row) if m).shape[0]
        for bi in range(n):
            tile_fn(*[r.at[pl.ds(bi, 1)] if m else r for r, m in zip(refs, is_row)])
    return kern


def _mod_kernel(c_ref, w_ref, b_ref, o_ref):
    act = _silu(c_ref[...])
    o_ref[0] = _mm3(act, w_ref[0]) + b_ref[0]


def _modulation(cond, ada_w, ada_b):
    L, D, N = ada_w.shape
    R = cond.shape[0]
    tn = 1024
    return pl.pallas_call(
        _mod_kernel,
        out_shape=jax.ShapeDtypeStruct((L, R, N), F32),
        grid=(L, N // tn),
        in_specs=[pl.BlockSpec((R, D), lambda l, j: (0, 0)),
                  pl.BlockSpec((1, D, tn), lambda l, j: (l, 0, j)),
                  pl.BlockSpec((1, 1, tn), lambda l, j: (l, 0, j))],
        out_specs=pl.BlockSpec((1, R, tn), lambda l, j: (l, 0, j)),
        compiler_params=_params(("parallel", "parallel")),
        name="adaln_mod",
    )(cond, ada_w, ada_b.reshape(L, 1, N))


def _inproj_kernel(xc_ref, xl_ref, mod_ref, g_ref, w_ref, ca_ref, sa_ref, cb_ref, sb_ref,
                   gq_ref, gkv_ref, wq_ref, wkv_ref,
                   qa_ref, ka_ref, va_ref, qb_ref, kb_ref, vb_ref, pr_ref):
    D = xc_ref.shape[-1]
    x = jnp.where(pl.program_id(1) == 0, xc_ref[0], xl_ref[0])
    mod = mod_ref[0, 0]
    sh1, sc1 = mod[:, 0:D], mod[:, D:2 * D]
    h = _bf(_rms(x, NORM_EPS) * g_ref[...] * (1.0 + sc1) + sh1)

    def proj(lo, hi):
        return _dot(h, w_ref[:, lo:hi])

    ca, sa = ca_ref[...], sa_ref[...]
    cb, sb = cb_ref[...], sb_ref[...]

    pbq = proj(P_BQ, P_BKV)
    ckv = proj(P_BKV, P_C)

    pq = proj(P_AQ, P_AK)
    pk = proj(P_AK, P_AV)
    for j in range(A_WIDTH // LANE):
        sl = slice(j * LANE, (j + 1) * LANE)
        qa_ref[0, :, sl] = _bf(_rope(pq[:, sl], ca, sa, A_DIM // 4) * (A_DIM ** -0.5 * LOG2E))
        ka_ref[0, :, sl] = _bf(_rope(pk[:, sl], ca, sa, A_DIM // 4))
    va_ref[0] = _bf(proj(P_AV, P_BQ))

    pr_ref[0] = proj(P_C, P_TOTAL)

    lane = _lane_iota(pbq.shape)
    cq = jnp.where(lane < B_Q_RANK, pbq, 0.0)
    cqn = cq * lax.rsqrt(jnp.sum(cq * cq, axis=-1, keepdims=True) * (1.0 / B_Q_RANK) + NORM_EPS) * gq_ref[...]
    qb = _mm(cqn, wq_ref[...]) * ((B_NOPE + B_ROPE) ** -0.5 * LOG2E)
    ckvn = _rms(ckv, NORM_EPS) * gkv_ref[...]
    kv = _mm(ckvn, wkv_ref[...])
    krb = pbq[:, LANE:2 * LANE]
    l1 = _lane_iota(krb.shape)
    kr = _rope(jnp.where((l1 >= B_NOPE) & (l1 < B_NOPE + B_ROPE), krb, 0.0), cb, sb, B_ROPE // 4)
    for j in range(B_HEADS):
        sl = slice(j * LANE, (j + 1) * LANE)
        qb_ref[0, :, sl] = _bf(_rope(qb[:, sl], cb, sb, B_ROPE // 4))
        kb_ref[0, :, sl] = _bf(kv[:, sl] + kr)
        vh = kv[:, B_HEADS * LANE + j * LANE:B_HEADS * LANE + (j + 1) * LANE]
        vb_ref[0, :, sl] = _bf(jnp.where(l1 < B_VDIM, vh, 1.0))


def _stream_specs(nb, tm, D, first, t_start=0):
    return [pl.BlockSpec((nb, tm, D), lambda b, i: (b, 0, 0)),
            pl.BlockSpec((nb, tm, D), lambda b, i: (b, jnp.maximum(i + t_start - first, 0), 0))]


def _inproj(Xc, Xl, modsel, g, w_in, ropeA, ropeB, gq, gkv, wq, wkv):
    B, _, D = Xc.shape
    tm = ROW_TILE
    first = 0 if Xl is Xc else Xc.shape[1] // tm
    S = Xl.shape[1] + first * tm
    nb = _batch_rows(B, 4)
    row = lambda w: pl.BlockSpec((nb, tm, w), lambda b, i: (b, i, 0))
    tab = pl.BlockSpec((tm, LANE), lambda b, i: (i, 0))
    outs = [jax.ShapeDtypeStruct((B, S, 512), BF16)] * 6 + [jax.ShapeDtypeStruct((B, S, C_COLS), F32)]
    return pl.pallas_call(
        _per_batch_row(_inproj_kernel, [True, True, True] + [False] * 10 + [True] * 7),
        out_shape=outs,
        grid=(B // nb, S // tm),
        in_specs=_stream_specs(nb, tm, D, first) + [
                  pl.BlockSpec((nb, 1, 1, modsel.shape[-1]), lambda b, i: (b, jnp.minimum(i, 1), 0, 0)),
                  _full((1, D)), _full(w_in.shape), tab, tab, tab, tab,
                  _full(gq.shape), _full(gkv.shape), _full(wq.shape), _full(wkv.shape)],
        out_specs=[row(512)] * 6 + [row(C_COLS)],
        compiler_params=_params(("parallel", "parallel")),
        name="in_proj",
    )(Xc, Xl, modsel, g, w_in, ropeA[0], ropeA[1], ropeB[0], ropeB[1], gq, gkv, wq, wkv)


def _attn_kernel(diff, lam_init, n_ctx, n_lat, n_q, *refs):
    q_refs, (k_ref, v_ref), rest = refs[:n_q], refs[n_q:n_q + 2], refs[n_q + 2:]
    if diff:
        lq1, lk1, lq2, lk2, g_ref, o_ref, sc_ref, sl_ref = rest
    else:
        o_ref, sc_ref, sl_ref = rest
    q = jnp.concatenate([r[0] for r in q_refs], axis=0) if n_q > 1 else q_refs[0][0]
    tq = q.shape[0]
    lane = _lane_iota((tq, LANE))
    if diff:
        zero = jnp.zeros_like(q)
        maps = [(jnp.where(lane < A_DIM, q, zero), slice(0, LANE)), (jnp.where(lane >= A_DIM, q, zero), slice(0, LANE))]
    else:
        maps = [(q[:, h * LANE:(h + 1) * LANE], slice(h * LANE, (h + 1) * LANE)) for h in range(q.shape[1] // LANE)]
    blocks = [(0, n_ctx, lambda mi: sc_ref.at[mi])]
    blocks += [(n_ctx + j * LAT_BLOCK, LAT_BLOCK, lambda mi, j=j: sl_ref.at[mi * n_lat + j]) for j in range(n_lat)]

    ms = []
    for mi, (qm, ksl) in enumerate(maps):
        mrun = jnp.full((tq, LANE), -jnp.inf, F32)
        for off, size, buf in blocks:
            s = _dot(qm, k_ref[0, off:off + size, ksl], 1, 1)
            buf(mi)[...] = s
            for c in range(size // LANE):
                mrun = jnp.maximum(mrun, s[:, c * LANE:(c + 1) * LANE])
        ms.append(jnp.max(mrun, axis=-1, keepdims=True))

    outs = []
    for mi, (qm, ksl) in enumerate(maps):
        acc = None
        for off, size, buf in blocks:
            v = v_ref[0, off:off + size, ksl]
            if diff:
                v = jnp.concatenate([v, jnp.ones_like(v)], axis=1)
            pv = _dot(_bf(jnp.exp2(buf(mi)[...] - ms[mi])), v)
            acc = pv if acc is None else acc + pv
        if diff:
            outs.append(acc[:, :LANE] / acc[:, LANE:])
        else:
            outs.append(jnp.where(lane < B_VDIM, acc / pltpu.roll(acc, B_VDIM, axis=1), 0.0))

    if diff:
        lam = (jnp.exp(jnp.sum(lq1[...] * lk1[...], axis=-1, keepdims=True))
               - jnp.exp(jnp.sum(lq2[...] * lk2[...], axis=-1, keepdims=True)) + lam_init)
        o = outs[0] - lam * outs[1]
        o_ref[0] = _bf(_rms(o, SUBLN_EPS) * g_ref[...] * (1.0 - lam_init))
    else:
        for h, o in enumerate(outs):
            o_ref[0, :, h * LANE:(h + 1) * LANE] = _bf(o)


def _attention(q, k, v, n_ctx, latent, diff_args=None):
    B, S, _ = k.shape
    tq = math.gcd(Q_TILE, S - n_ctx) if latent else ROW_TILE
    tb = ROW_TILE
    n_q = tq // tb
    t0 = n_ctx // tb if latent else 0
    nq = (S - n_ctx) // tq if latent else n_ctx // tq
    n_keys = S if latent else n_ctx
    diff = diff_args is not None
    bw = LANE if diff else 2 * LANE
    assert (n_keys - n_ctx) % LAT_BLOCK == 0
    n_lat = (n_keys - n_ctx) // LAT_BLOCK
    if diff:
        lq1, lk1, lq2, lk2, subln_g, lam_init = diff_args
        extra = [lq1, lk1, lq2, lk2, subln_g]
        vec = _full((1, A_DIM))
        extra_specs = [vec, vec, vec, vec, _full((1, LANE))]
    else:
        lam_init, extra, extra_specs = None, [], []
    kern = functools.partial(_attn_kernel, diff, lam_init, n_ctx, n_lat, n_q)
    qspecs = [pl.BlockSpec((1, tb, bw), lambda b, h, i, j=j: (b, t0 + i * n_q + j, h)) for j in range(n_q)]
    kvspec = pl.BlockSpec((1, n_keys, bw), lambda b, h, i: (b, 0, h))
    return pl.pallas_call(
        kern,
        out_shape=jax.ShapeDtypeStruct((B, nq * tq, q.shape[-1]), BF16),
        grid=(B, q.shape[-1] // bw, nq),
        in_specs=qspecs + [kvspec, kvspec] + extra_specs,
        out_specs=pl.BlockSpec((1, tq, bw), lambda b, h, i: (b, i, h)),
        scratch_shapes=[pltpu.VMEM((2, tq, n_ctx), F32),
                        pltpu.VMEM((max(2 * n_lat, 1), tq, LAT_BLOCK), F32)],
        compiler_params=_params(("parallel", "parallel", "parallel")),
        name=("diff_attn" if diff else "mla_attn") + ("" if latent else "_ctx"),
    )(*([q] * n_q), k, v, *extra)


def _rwkv_chunk_index(rev, n_ctx_chunks, n_chunks, j):
    if not rev:
        return j
    return jnp.where(j < n_ctx_chunks, n_ctx_chunks - 1 - j, n_chunks + n_ctx_chunks - 1 - j)


def _rwkv_local_kernel(n_tiles,
                       cur_ref, prv_ref, nxt_ref, mup_ref, mun_ref, w0_ref, w2_ref, a0_ref, a2_ref, g2_ref,
                       kk_ref, ka_ref, rk_ref, bd_ref,
                       qf_ref, y0f_ref, mf_ref, nf_ref, qb_ref, y0b_ref, mb_ref, nb_ref, bonus_ref, gate_ref):
    C, W = CHUNK, C_WIDTH
    NB, R = cur_ref.shape[0], cur_ref.shape[1]
    i = pl.program_id(1)
    seq_first = i <= 1
    seq_last = (i == 0) | (i == n_tiles - 1)

    rr = _row_iota((R, R))
    cr = _lane_iota((R, R))
    same_chunk = (rr >> 6) == (cr >> 6)
    rows = _row_iota((R, C_COLS))
    rc = _row_iota((C, W))
    cc = _lane_iota((C, W)) & (C - 1)
    eye = jnp.where(rc == cc, 1.0, 0.0)
    pair = [((rc >> (lvl + 1)) == (cc >> (lvl + 1))) & ((rc >> lvl) != (cc >> lvl))
            for lvl in range(int(math.log2(C)))]
    same_head = (_row_iota((W, W)) >> 6) == (_lane_iota((W, W)) >> 6)
    bd = bd_ref[...]
    outs = ((qf_ref, y0f_ref, mf_ref, nf_ref), (qb_ref, y0b_ref, mb_ref, nb_ref))

    chunks = []
    for bi in range(NB):
        x = cur_ref[bi]
        prev_row = jnp.where(seq_first, 0.0, prv_ref[bi, SUBLANE - 1:SUBLANE, :])
        next_row = jnp.where(seq_last, 0.0, nxt_ref[bi, 0:1, :])
        xp = jnp.where(rows == 0, prev_row, pltpu.roll(x, 1, axis=0))
        xn = jnp.where(rows == R - 1, next_row, pltpu.roll(x, R - 1, axis=0))
        xs = x + mup_ref[...] * (xp - x) + mun_ref[...] * (xn - x)

        r, k, v = xs[:, 0:W], xs[:, W:2 * W], xs[:, 2 * W:3 * W]
        wl = jnp.tanh(xs[:, 3 * W:3 * W + 2 * C_LORA])
        al = xs[:, 3 * W + 2 * C_LORA:3 * W + 4 * C_LORA]
        gl = xs[:, 3 * W + 4 * C_LORA:]
        kkr = k * kk_ref[...]
        kk = kkr / jnp.maximum(jnp.sqrt(_mm_hl(kkr * kkr, bd)), 1e-12)
        gate_ref[bi] = _mm(_sigmoid(gl), g2_ref[...])

        bonus = jnp.zeros((R, W), F32)
        for d, rev in enumerate((False, True)):
            w = -_softplus(-(w0_ref[d:d + 1] + _mm3(wl, w2_ref[d]))) - 0.5
            lw = -jnp.exp(w)
            a_ic = _sigmoid(a0_ref[d:d + 1] + _mm3(al, a2_ref[d]))
            kd = k * (1.0 + (a_ic - 1.0) * ka_ref[...])
            avec = -kk
            bvec = kk * a_ic
            bonus = bonus + _mm_hl(r * kd * rk_ref[...], bd) * v

            tri = jnp.where(same_chunk & ((cr >= rr) if rev else (cr <= rr)), 1.0, 0.0).astype(BF16)
            l_hi = _bf(lw)
            l_md = _bf(lw - l_hi.astype(F32))
            l_lo = _bf(lw - l_hi.astype(F32) - l_md.astype(F32))
            cs = _dot(tri, l_hi) + (_dot(tri, l_md) + _dot(tri, l_lo))
            e_neg = jnp.exp(-cs)
            At_all = avec * jnp.exp(cs - lw)
            Rt_all = r * jnp.exp(cs)
            Bt_all = bvec * e_neg
            Kt_all = kd * e_neg
            for g in range(R // C):
                sl = slice(g * C, (g + 1) * C)
                csg = cs[sl]
                tot = csg[0:1, :] if rev else csg[C - 1:C, :]
                e_rem = jnp.exp(tot - csg)
                chunks.append(dict(bi=bi, d=d, g=g, rev=rev, sl=sl, At=At_all[sl], Rt=Rt_all[sl], v=v[sl], tot=tot,
                                   Bt=Bt_all[sl], Kt=Kt_all[sl], Bg=bvec[sl] * e_rem, Kg=kd[sl] * e_rem))
        bonus_ref[bi] = bonus

    def heads_bd(x):
        xb = _bf(x)
        return jnp.where(same_head, jnp.concatenate([xb] * C_HEADS, axis=0), jnp.zeros((W, W), BF16))

    def each(fn, *lists):
        return [fn(*args) for args in zip(*lists)]

    X = [jnp.concatenate([ch["At"], ch["Rt"]], axis=0) for ch in chunks]
    GB = each(lambda x, ch: _dot(_bf(x), heads_bd(ch["Bt"]), 1, 1), X, chunks)
    GK = each(lambda x, ch: _dot(_bf(x), heads_bd(ch["Kt"]), 1, 1), X, chunks)
    strict = {False: cc < rc, True: cc > rc}
    incl = {False: cc <= rc, True: cc >= rc}
    Aab = each(lambda g_, ch: jnp.where(strict[ch["rev"]], g_[0:C], 0.0), GB, chunks)
    Arb = each(lambda g_, ch: jnp.where(incl[ch["rev"]], g_[C:], 0.0), GB, chunks)
    Aak = each(lambda g_, ch: jnp.where(strict[ch["rev"]], g_[0:C], 0.0), GK, chunks)
    Ark = each(lambda g_, ch: jnp.where(incl[ch["rev"]], g_[C:], 0.0), GK, chunks)
    T = [eye + jnp.where(pair[0], a, 0.0) for a in Aab]
    for lvl in range(1, len(pair)):
        P = each(lambda t, a: _dot(_bf(t), heads_bd(jnp.where(pair[lvl], a, 0.0))), T, Aab)
        T = each(lambda t, p: t + _dot(_bf(p), heads_bd(t)), T, P)
    Vbd = [heads_bd(ch["v"]) for ch in chunks]
    AkV = each(lambda a, vb: _dot(_bf(a), vb), Aak, Vbd)
    Wt = each(lambda t, ch: _dot(_bf(t), heads_bd(ch["At"])), T, chunks)
    U0 = each(lambda t, akv: _dot(_bf(t), heads_bd(akv)), T, AkV)
    Q = each(lambda a, wt, ch: ch["Rt"] + _dot(_bf(a), heads_bd(wt)), Arb, Wt, chunks)
    Y0 = each(lambda arb, u0, ark, vb: _dot(_bf(arb), heads_bd(u0)) + _dot(_bf(ark), vb), Arb, U0, Ark, Vbd)

    for ch, wt, u0, q, y0 in zip(chunks, Wt, U0, Q, Y0):
        q_ref, y0_ref, m_ref, n_ref = outs[ch["d"]]
        q_ref[ch["bi"], ch["sl"], :] = _bf(q)
        y0_ref[ch["bi"], ch["sl"], :] = y0
        Mbd = jnp.where(same_head, _mm_tn(wt, ch["Bg"]), 0.0)
        Nbd = jnp.where(same_head, _mm_tn(u0, ch["Bg"]) + _mm_tn(ch["v"], ch["Kg"]), 0.0)
        Mc = Mbd[0:C] + Mbd[C:2 * C] + Mbd[2 * C:3 * C] + Mbd[3 * C:]
        m_ref[ch["bi"], ch["g"]] = _bf(Mc + jnp.where(rc == cc, jnp.exp(ch["tot"]), 0.0))
        n_ref[ch["bi"], ch["g"]] = Nbd[0:C] + Nbd[C:2 * C] + Nbd[2 * C:3 * C] + Nbd[3 * C:]


def _rwkv_local(pr, mu_prev, mu_next, w0, w2p, a0, a2p, g2, k_k, k_a, r_k, bd):
    B, S, _ = pr.shape
    R, C, W = ROW_TILE, CHUNK, C_WIDTH
    NB = 2 if B % 2 == 0 else 1
    nt, gpt = S // R, R // C
    bpt = R // SUBLANE
    nblk = S // SUBLANE
    rowspec = lambda w: pl.BlockSpec((NB, R, w), lambda b, i: (b, i, 0))
    mspec = pl.BlockSpec((NB, gpt, C, W), lambda b, i: (b, i, 0, 0))
    seq = lambda dt: jax.ShapeDtypeStruct((B, S, W), dt)
    mat = lambda dt: jax.ShapeDtypeStruct((B, S // C, C, W), dt)
    per_dir = [seq(BF16), seq(F32), mat(BF16), mat(F32)]
    vec = _full((1, W))
    return pl.pallas_call(
        functools.partial(_rwkv_local_kernel, nt),
        out_shape=per_dir * 2 + [seq(F32), seq(F32)],
        grid=(B // NB, nt),
        in_specs=[rowspec(C_COLS),
                  pl.BlockSpec((NB, SUBLANE, C_COLS), lambda b, i: (b, jnp.maximum(i * bpt - 1, 0), 0)),
                  pl.BlockSpec((NB, SUBLANE, C_COLS), lambda b, i: (b, jnp.minimum((i + 1) * bpt, nblk - 1), 0)),
                  _full((1, C_COLS)), _full((1, C_COLS)), _full(w0.shape), _full(w2p.shape), _full(a0.shape),
                  _full(a2p.shape), _full(g2.shape), vec, vec, vec, _full(bd.shape)],
        out_specs=[rowspec(W), rowspec(W), mspec, mspec] * 2 + [rowspec(W), rowspec(W)],
        compiler_params=_params(("parallel", "parallel")),
        name="rwkv_local",
    )(pr, pr, pr, mu_prev, mu_next, w0, w2p, a0, a2p, g2, k_k, k_a, r_k, bd)


def _rwkv_state_kernel(qf_ref, y0f_ref, mf_ref, nf_ref, qb_ref, y0b_ref, mb_ref, nb_ref, yf_ref, yb_ref, s_ref):
    C, W = CHUNK, C_WIDTH
    nb, G = mf_ref.shape[0], mf_ref.shape[1]

    @pl.when(pl.program_id(0) == 0)
    def _():
        s_ref[...] = jnp.zeros(s_ref.shape, F32)

    head = _lane_iota((C, W)) >> 6

    def expand(mc):
        return jnp.concatenate([jnp.where(head == h, mc, jnp.zeros_like(mc)) for h in range(C_HEADS)], axis=0)

    dirs = ((qf_ref, y0f_ref, mf_ref, nf_ref, yf_ref, range(G)),
            (qb_ref, y0b_ref, mb_ref, nb_ref, yb_ref, range(G - 1, -1, -1)))
    for step in range(G):
        for d, (q_ref, y0_ref, m_ref, n_ref, y_ref, order) in enumerate(dirs):
            g = order[step]
            sl = slice(g * C, (g + 1) * C)
            for b in range(nb):
                S = _bf(s_ref[d, b])
                y_ref[b, sl, :] = _dot(q_ref[b, sl, :], S, 1, 1) + y0_ref[b, sl, :]
                s_ref[d, b] = _dot(S, expand(m_ref[b, g])) + expand(n_ref[b, g])


def _rwkv_state(loc, n_ctx):
    qf = loc[0]
    B, S, W = qf.shape
    C = CHUNK
    G = 4
    nblk, ncb = S // (G * C), n_ctx // (G * C)
    fwd = lambda j: j
    bwd = functools.partial(_rwkv_chunk_index, True, ncb, nblk)
    seqspec = lambda idx: pl.BlockSpec((B, G * C, W), lambda j: (0, idx(j), 0))
    matspec = lambda idx: pl.BlockSpec((B, G, C, W), lambda j: (0, idx(j), 0, 0))
    out = jax.ShapeDtypeStruct((B, S, W), F32)
    return pl.pallas_call(
        _rwkv_state_kernel,
        out_shape=[out, out],
        grid=(nblk,),
        in_specs=[seqspec(fwd), seqspec(fwd), matspec(fwd), matspec(fwd),
                  seqspec(bwd), seqspec(bwd), matspec(bwd), matspec(bwd)],
        out_specs=[seqspec(fwd), seqspec(bwd)],
        scratch_shapes=[pltpu.VMEM((2, B, W, W), F32)],
        compiler_params=_params(("arbitrary",)),
        name="rwkv_state",
    )(*loc[:8])


def _outproj_kernel(t_start, xc_ref, xl_ref, oac_ref, oal_ref, obc_ref, obl_ref,
                    yf_ref, yb_ref, bonus_ref, gate_ref, gng_ref, gnb_ref, bdm_ref,
                    mod_ref, w_ref, gpost_ref, gpre_ref, x1_ref, h2_ref):
    D = xc_ref.shape[-1]
    is_ctx = pl.program_id(1) + t_start == 0
    x = jnp.where(is_ctx, xc_ref[0], xl_ref[0])
    oa = jnp.where(is_ctx, oac_ref[0], oal_ref[0])
    ob = jnp.where(is_ctx, obc_ref[0], obl_ref[0])
    mod = mod_ref[0, 0]
    gt1 = mod[:, 2 * D:3 * D]
    sh2, sc2 = mod[:, 3 * D:4 * D], mod[:, 4 * D:5 * D]
    y = yf_ref[0] + yb_ref[0]
    bdm = bdm_ref[...]
    mu = _mm_hl(y, bdm)
    dev = y - mu
    var = _mm_hl(dev * dev, bdm)
    yn = dev * lax.rsqrt(var + C_GN_EPS) * gng_ref[...] + gnb_ref[...]
    oc = _bf((yn + bonus_ref[0]) * gate_ref[0])
    nb = ob.shape[-1]
    o = (_dot(oa, w_ref[0:A_WIDTH, :])
         + _dot(ob, w_ref[A_WIDTH:A_WIDTH + nb, :])
         + _dot(oc, w_ref[A_WIDTH + nb:, :]))
    x1 = x + gt1 * (_rms(o, NORM_EPS) * gpost_ref[...])
    x1_ref[0] = x1
    h2_ref[0] = _bf(_rms(x1, NORM_EPS) * gpre_ref[...] * (1.0 + sc2) + sh2)


def _outproj(Xc, Xl, oa, ob, rwkv, gn_g, gn_b, bdm, modsel, w_out, g_post, g_pre, t_start):
    B, _, D = Xc.shape
    tm = ROW_TILE
    first = 0 if Xl is Xc else Xc.shape[1] // tm
    S = rwkv[0].shape[1]
    nt = S // tm - t_start
    nb = _batch_rows(B, 4)
    row = lambda w: pl.BlockSpec((nb, tm, w), lambda b, i: (b, i, 0))
    full_row = lambda w: pl.BlockSpec((nb, tm, w), lambda b, i: (b, i + t_start, 0))
    pair = lambda w: _stream_specs(nb, tm, w, 1, t_start)
    W = C_WIDTH
    return pl.pallas_call(
        _per_batch_row(functools.partial(_outproj_kernel, t_start),
                       [True] * 10 + [False] * 3 + [True] + [False] * 3 + [True] * 2),
        out_shape=[jax.ShapeDtypeStruct((B, nt * tm, D), F32), jax.ShapeDtypeStruct((B, nt * tm, D), BF16)],
        grid=(B // nb, nt),
        in_specs=_stream_specs(nb, tm, D, first, t_start) + pair(A_WIDTH) + pair(ob[1].shape[-1])
                 + [full_row(W)] * 4
                 + [_full((1, W)), _full((1, W)), _full((W, W)),
                    pl.BlockSpec((nb, 1, 1, modsel.shape[-1]), lambda b, i: (b, jnp.minimum(i + t_start, 1), 0, 0)),
                    _full(w_out.shape), _full((1, D)), _full((1, D))],
        out_specs=[row(D), row(D)],
        compiler_params=_params(("parallel", "parallel")),
        name="out_proj",
    )(Xc, Xl, *oa, *ob, *rwkv, gn_g, gn_b, bdm, modsel, w_out, g_post, g_pre)


def _ffn_kernel(n_ctx_tiles, n_tiles, t_start, ff_tile,
                x_ref, h_ref, hp_ref, hn_ref, mod_ref, wup_ref, cw_ref, cb_ref, wdn_ref, gpost_ref, o_ref,
                act_ref):
    D = x_ref.shape[-1]
    tm = x_ref.shape[1]
    dff = wdn_ref.shape[0]
    i = pl.program_id(1) + t_start
    seq_first = (i == 0) | (i == n_ctx_tiles)
    seq_last = (i == n_ctx_tiles - 1) | (i == n_tiles - 1)
    hp = hp_ref[0]
    hn = hn_ref[0]
    hp = jnp.where(seq_first, jnp.zeros_like(hp), hp)
    hn = jnp.where(seq_last, jnp.zeros_like(hn), hn)
    hext = jnp.concatenate([hp, h_ref[0], hn], axis=0)
    halo = hp.shape[0]
    for f in range(dff // ff_tile):
        parts = []
        for base in (0, dff):
            lo = base + f * ff_tile
            u = _dot(hext, wup_ref[:, lo:lo + ff_tile])
            cw = cw_ref[:, lo:lo + ff_tile]
            n = u.shape[0]
            up = pltpu.roll(u, 1, axis=0)[halo:halo + tm]
            un = pltpu.roll(u, n - 1, axis=0)[halo:halo + tm]
            parts.append(up * cw[0:1] + u[halo:halo + tm] * cw[1:2] + un * cw[2:3]
                         + cb_ref[:, lo:lo + ff_tile])
        act_ref[0, :, f * ff_tile:(f + 1) * ff_tile] = _bf(_silu(parts[0]) * parts[1])
    acc = _dot(act_ref[0], wdn_ref[...])
    gt2 = mod_ref[0, 0][:, 5 * D:6 * D]
    o_ref[0] = x_ref[0] + gt2 * (_rms(acc, NORM_EPS) * gpost_ref[...])


def _ffn(X1, H2, modsel, w_up, conv_w, conv_b, w_dn, g_post, n_ctx, t_start):
    B, S, D = X1.shape
    tm = ROW_TILE
    halo = 16
    nt = S // tm
    hb = tm // halo
    nhb = S // halo
    nb = _batch_rows(B, 4)
    kern = _per_batch_row(functools.partial(_ffn_kernel, n_ctx // tm, nt + t_start, t_start, 256),
                          [True] * 5 + [False] * 5 + [True] * 2)
    row = pl.BlockSpec((nb, tm, D), lambda b, i: (b, i, 0))
    once = lambda shape: pl.BlockSpec(shape, lambda b, i: (0, 0), pipeline_mode=pl.Buffered(1))
    return pl.pallas_call(
        kern,
        out_shape=jax.ShapeDtypeStruct((B, S, D), F32),
        grid=(B // nb, nt),
        in_specs=[row, row,
                  pl.BlockSpec((nb, halo, D), lambda b, i: (b, jnp.maximum(i * hb - 1, 0), 0)),
                  pl.BlockSpec((nb, halo, D), lambda b, i: (b, jnp.minimum((i + 1) * hb, nhb - 1), 0)),
                  pl.BlockSpec((nb, 1, 1, modsel.shape[-1]), lambda b, i: (b, jnp.minimum(i + t_start, 1), 0, 0)),
                  once(w_up.shape), _full(conv_w.shape), _full(conv_b.shape), once(w_dn.shape), _full((1, D))],
        out_specs=row,
        scratch_shapes=[pltpu.VMEM((nb, tm, w_dn.shape[0]), BF16)],
        compiler_params=_params(("parallel", "parallel")),
        name="conv_ffn",
    )(X1, H2, H2, H2, modsel, w_up, conv_w, conv_b, w_dn, g_post)


def _rope_tables(S, n_ctx, dim, lane_lo):
    nf = dim // 4
    t = jnp.arange(S - n_ctx, dtype=jnp.int32)
    rows = (t // GRID_W).astype(F32)
    cols = (t % GRID_W).astype(F32)
    inv = ROPE_BASE ** (-jnp.arange(nf, dtype=F32) / nf)
    ar, ac = rows[:, None] * inv, cols[:, None] * inv
    cos = jnp.concatenate([jnp.cos(ar), jnp.cos(ar), jnp.cos(ac), jnp.cos(ac)], axis=-1)
    sin = jnp.concatenate([-jnp.sin(ar), jnp.sin(ar), -jnp.sin(ac), jnp.sin(ac)], axis=-1)
    if lane_lo == 0:
        reps = LANE // dim
        cos, sin = jnp.tile(cos, (1, reps)), jnp.tile(sin, (1, reps))
    else:
        pad = ((0, 0), (lane_lo, LANE - lane_lo - dim))
        cos = jnp.pad(cos, pad, constant_values=1.0)
        sin = jnp.pad(sin, pad)
    cos = jnp.concatenate([jnp.ones((n_ctx, LANE), F32), cos], axis=0)
    sin = jnp.concatenate([jnp.zeros((n_ctx, LANE), F32), sin], axis=0)
    return cos, sin


def _layout_w_in(w):
    D = w.shape[0]
    a = w[:, :3 * A_WIDTH]
    o = 3 * A_WIDTH
    cq = w[:, o:o + B_Q_RANK]
    ckv = w[:, o + B_Q_RANK:o + B_Q_RANK + B_KV_RANK]
    kr = w[:, o + B_Q_RANK + B_KV_RANK:o + B_Q_RANK + B_KV_RANK + B_ROPE]
    c = w[:, o + B_Q_RANK + B_KV_RANK + B_ROPE:]
    z = jnp.zeros((D, 2 * LANE - B_Q_RANK - B_ROPE), w.dtype)
    return _bf(jnp.concatenate([a, cq, kr, z, ckv, c], axis=1))


def _layout_wq(w):
    hd = B_NOPE + B_ROPE
    w = w.reshape(B_Q_RANK, B_HEADS, hd)
    w = jnp.pad(w, ((0, 2 * LANE - B_Q_RANK), (0, 0), (0, LANE - hd)))
    return _bf(w.reshape(2 * LANE, B_HEADS * LANE))


def _layout_wkv(w):
    w = w.reshape(B_KV_RANK, B_HEADS, B_NOPE + B_VDIM)
    pad = lambda t: jnp.pad(t, ((0, 0), (0, 0), (0, LANE - t.shape[-1]))).reshape(B_KV_RANK, B_HEADS * LANE)
    return _bf(jnp.concatenate([pad(w[:, :, :B_NOPE]), pad(w[:, :, B_NOPE:])], axis=1))


def _layout_w_out(w):
    D = w.shape[1]
    wb = w[A_WIDTH:A_WIDTH + B_WIDTH].reshape(B_HEADS, B_VDIM, D)
    wb = jnp.pad(wb, ((0, 0), (0, LANE - B_VDIM), (0, 0))).reshape(B_HEADS * LANE, D)
    return _bf(jnp.concatenate([w[:A_WIDTH], wb, w[A_WIDTH + B_WIDTH:]], axis=0))


def _layout_lora(w, d):
    z = jnp.zeros_like(w)
    return jnp.concatenate([z, w] if d else [w, z], axis=0)


def _block_diag_ones(scale):
    i = np.arange(C_WIDTH) // C_DIM
    return jnp.asarray((i[:, None] == i[None, :]).astype(np.float32) * scale, dtype=BF16)


def kernel(x, c, ctx, c_ctx, ada_w, ada_b, mix_pre_g, mix_post_g, ffn_pre_g, ffn_post_g, w_in, w_out, lam_q1, lam_k1, lam_q2, lam_k2, a_subln_g, b_q_norm_g, b_w_q_up, b_kv_norm_g, b_w_kv_up, c_mu_prev, c_mu_next, c_w0, c_w2, c_a0, c_a2, c_g2, c_k_k, c_k_a, c_r_k, c_gn_g, c_gn_b, ffn_w_up, ffn_conv_w, ffn_conv_b, ffn_w_down):
    B, T, D = x.shape
    n_ctx = ctx.shape[1]
    S = n_ctx + T
    L = ada_w.shape[0]
    assert n_ctx == ROW_TILE and T % ROW_TILE == 0 and D % LANE == 0

    Xc, Xl = ctx, x
    cond = jnp.concatenate([c, c_ctx[None, :], jnp.zeros((SUBLANE - B - 1, D), F32)], axis=0)
    mod = _modulation(cond, ada_w, ada_b)
    ropeA = _rope_tables(S, n_ctx, A_DIM, 0)
    ropeB = _rope_tables(S, n_ctx, B_ROPE, B_NOPE)
    bd1 = _block_diag_ones(1.0)
    bdm = _block_diag_ones(1.0 / C_DIM)
    row = lambda v: v.reshape(1, -1)

    for i in range(L):
        last = i == L - 1
        t0 = 1 if last else 0
        lam_init = 0.8 - 0.6 * math.exp(-0.3 * i)
        modsel = jnp.stack([jnp.broadcast_to(mod[i, B], (B, 6 * D)), mod[i, :B]], axis=1)[:, :, None, :]
        gq = jnp.pad(b_q_norm_g[i], (0, 2 * LANE - B_Q_RANK)).reshape(1, -1)
        qa, ka, va, qb, kb, vb, pr = _inproj(
            Xc, Xl, modsel, row(mix_pre_g[i]), _layout_w_in(w_in[i]), ropeA, ropeB,
            gq, row(b_kv_norm_g[i]), _layout_wq(b_w_q_up[i]), _layout_wkv(b_w_kv_up[i]))
        dargs = (row(lam_q1[i]), row(lam_k1[i]), row(lam_q2[i]), row(lam_k2[i]), row(a_subln_g[i]), lam_init)
        oa = _attention(qa, ka, va, n_ctx, True, dargs)
        ob = _attention(qb, kb, vb, n_ctx, True)
        oa = (oa if last else _attention(qa, ka, va, n_ctx, False, dargs), oa)
        ob = (ob if last else _attention(qb, kb, vb, n_ctx, False), ob)
        loc = _rwkv_local(pr, row(c_mu_prev[i]), row(c_mu_next[i]),
                          c_w0[i], jnp.stack([_layout_lora(c_w2[i, d], d) for d in (0, 1)]),
                          c_a0[i], jnp.stack([_layout_lora(c_a2[i, d], d) for d in (0, 1)]),
                          c_g2[i], row(c_k_k[i]), row(c_k_a[i]), row(c_r_k[i]), bd1)
        yf, yb = _rwkv_state(loc, n_ctx)
        X1, H2 = _outproj(Xc, Xl, oa, ob, (yf, yb, loc[8], loc[9]), row(c_gn_g[i]), row(c_gn_b[i]), bdm, modsel,
                          _layout_w_out(w_out[i]), row(mix_post_g[i]), row(ffn_pre_g[i]), t0)
        Xc = Xl = _ffn(X1, H2, modsel, _bf(ffn_w_up[i]), ffn_conv_w[i], row(ffn_conv_b[i]), _bf(ffn_w_down[i]),
                       row(ffn_post_g[i]), n_ctx, t0)
    return Xl
```

```python
import functools
import math

import jax
import jax.numpy as jnp
import numpy as np
from jax import lax
from jax.experimental import pallas as pl
from jax.experimental.pallas import tpu as pltpu

F32 = jnp.float32
BF16 = jnp.bfloat16

GRID_W = 64
ROPE_BASE = 10000.0
NORM_EPS = 1e-6
SUBLN_EPS = 1e-5
A_HEADS, A_DIM = 4, 64
A_WIDTH = A_HEADS * 2 * A_DIM
B_HEADS, B_NOPE, B_ROPE, B_VDIM = 4, 64, 32, 64
B_Q_RANK, B_KV_RANK = 192, 128
B_WIDTH = B_HEADS * B_VDIM
C_HEADS, C_DIM = 4, 64
C_WIDTH = C_HEADS * C_DIM
C_LORA = 64
C_GATE_LORA = 128
C_GN_EPS = 64e-5
C_COLS = 3 * C_WIDTH + 4 * C_LORA + C_GATE_LORA

LANE = 128
SUBLANE = 8
ROW_TILE = 256
Q_TILE = 1024
LAT_BLOCK = 512
LOG2E = 1.4426950408889634
CHUNK = 64
VMEM_LIMIT = 56 * 1024 * 1024

P_AQ, P_AK, P_AV = 0, 512, 1024
P_BQ = 1536
P_BKV = 1792
P_C = 1920
P_TOTAL = 3072


def _bf(x):
    return x.astype(BF16)


def _dot(a, b, ca=1, cb=0):
    return lax.dot_general(a, b, (((ca,), (cb,)), ((), ())), preferred_element_type=F32)


def _mm(a, b):
    return _dot(_bf(a), _bf(b))


def _mm_nt(a, b):
    return _dot(_bf(a), _bf(b), 1, 1)


def _mm_tn(a, b):
    return _dot(_bf(a.T), _bf(b))


def _split2(x):
    hi = _bf(x)
    lo = _bf(x - hi.astype(F32))
    return hi, lo


def _mm_hl(a, b):
    hi, lo = _split2(a)
    return _dot(hi, b) + _dot(lo, b)


def _mm3(a, b):
    ah, al = _split2(a)
    bh, bl = _split2(b)
    return _dot(ah, bh) + (_dot(ah, bl) + _dot(al, bh))


def _rms(x, eps):
    return x * lax.rsqrt(jnp.mean(x * x, axis=-1, keepdims=True) + eps)


def _sigmoid(x):
    return 1.0 / (1.0 + jnp.exp(-x))


def _silu(x):
    return x * _sigmoid(x)


def _softplus(x):
    return jnp.maximum(x, 0.0) + jnp.log(1.0 + jnp.exp(-jnp.abs(x)))


def _lane_iota(shape):
    return lax.broadcasted_iota(jnp.int32, shape, len(shape) - 1)


def _row_iota(shape):
    return lax.broadcasted_iota(jnp.int32, shape, len(shape) - 2)


def _rope(x, cos, sin, half):
    n = x.shape[-1]
    up = pltpu.roll(x, n - half, axis=1)
    dn = pltpu.roll(x, half, axis=1)
    first = (_lane_iota(x.shape) & half) == 0
    return x * cos + jnp.where(first, up, dn) * sin


def _params(sem):
    return pltpu.CompilerParams(dimension_semantics=sem, vmem_limit_bytes=VMEM_LIMIT)


def _full(shape):
    nd = len(shape)
    return pl.BlockSpec(shape, lambda *_: (0,) * nd)


def _batch_rows(B, most=2):
    return max(n for n in (1, 2, 4) if n <= most and B % n == 0)


def _per_batch_row(tile_fn, is_row):
    def kern(*refs):
        n = next(r for r, m in zip(refs, is_row) if m).shape[0]
        for bi in range(n):
            tile_fn(*[r.at[pl.ds(bi, 1)] if m else r for r, m in zip(refs, is_row)])
    return kern


def _mod_kernel(c_ref, w_ref, b_ref, o_ref):
    act = _silu(c_ref[...])
    o_ref[0] = _mm3(act, w_ref[0]) + b_ref[0]


def _modulation(cond, ada_w, ada_b):
    L, D, N = ada_w.shape
    R = cond.shape[0]
    tn = 1024
    return pl.pallas_call(
        _mod_kernel,
        out_shape=jax.ShapeDtypeStruct((L, R, N), F32),
        grid=(L, N // tn),
        in_specs=[pl.BlockSpec((R, D), lambda l, j: (0, 0)),
                  pl.BlockSpec((1, D, tn), lambda l, j: (l, 0, j)),
                  pl.BlockSpec((1, 1, tn), lambda l, j: (l, 0, j))],
        out_specs=pl.BlockSpec((1, R, tn), lambda l, j: (l, 0, j)),
        compiler_params=_params(("parallel", "parallel")),
        name="adaln_mod",
    )(cond, ada_w, ada_b.reshape(L, 1, N))


def _inproj_kernel(xc_ref, xl_ref, mod_ref, g_ref, w_ref, ca_ref, sa_ref, cb_ref, sb_ref,
                   gq_ref, gkv_ref, wq_ref, wkv_ref,
                   qa_ref, ka_ref, va_ref, qb_ref, kb_ref, vb_ref, pr_ref):
    D = xc_ref.shape[-1]
    x = jnp.where(pl.program_id(1) == 0, xc_ref[0], xl_ref[0])
    mod = mod_ref[0, 0]
    sh1, sc1 = mod[:, 0:D], mod[:, D:2 * D]
    h = _bf(_rms(x, NORM_EPS) * g_ref[...] * (1.0 + sc1) + sh1)

    def proj(lo, hi):
        return _dot(h, w_ref[:, lo:hi])

    ca, sa = ca_ref[...], sa_ref[...]
    cb, sb = cb_ref[...], sb_ref[...]

    pbq = proj(P_BQ, P_BKV)
    ckv = proj(P_BKV, P_C)

    pq = proj(P_AQ, P_AK)
    pk = proj(P_AK, P_AV)
    for j in range(A_WIDTH // LANE):
        sl = slice(j * LANE, (j + 1) * LANE)
        qa_ref[0, :, sl] = _bf(_rope(pq[:, sl], ca, sa, A_DIM // 4) * (A_DIM ** -0.5 * LOG2E))
        ka_ref[0, :, sl] = _bf(_rope(pk[:, sl], ca, sa, A_DIM // 4))
    va_ref[0] = _bf(proj(P_AV, P_BQ))

    pr_ref[0] = proj(P_C, P_TOTAL)

    lane = _lane_iota(pbq.shape)
    cq = jnp.where(lane < B_Q_RANK, pbq, 0.0)
    cqn = cq * lax.rsqrt(jnp.sum(cq * cq, axis=-1, keepdims=True) * (1.0 / B_Q_RANK) + NORM_EPS) * gq_ref[...]
    qb = _mm(cqn, wq_ref[...]) * ((B_NOPE + B_ROPE) ** -0.5 * LOG2E)
    ckvn = _rms(ckv, NORM_EPS) * gkv_ref[...]
    kv = _mm(ckvn, wkv_ref[...])
    krb = pbq[:, LANE:2 * LANE]
    l1 = _lane_iota(krb.shape)
    kr = _rope(jnp.where((l1 >= B_NOPE) & (l1 < B_NOPE + B_ROPE), krb, 0.0), cb, sb, B_ROPE // 4)
    for j in range(B_HEADS):
        sl = slice(j * LANE, (j + 1) * LANE)
        qb_ref[0, :, sl] = _bf(_rope(qb[:, sl], cb, sb, B_ROPE // 4))
        kb_ref[0, :, sl] = _bf(kv[:, sl] + kr)
        vh = kv[:, B_HEADS * LANE + j * LANE:B_HEADS * LANE + (j + 1) * LANE]
        vb_ref[0, :, sl] = _bf(jnp.where(l1 < B_VDIM, vh, 1.0))


def _stream_specs(nb, tm, D, first, t_start=0):
    return [pl.BlockSpec((nb, tm, D), lambda b, i: (b, 0, 0)),
            pl.BlockSpec((nb, tm, D), lambda b, i: (b, jnp.maximum(i + t_start - first, 0), 0))]


def _inproj(Xc, Xl, modsel, g, w_in, ropeA, ropeB, gq, gkv, wq, wkv):
    B, _, D = Xc.shape
    tm = ROW_TILE
    first = 0 if Xl is Xc else Xc.shape[1] // tm
    S = Xl.shape[1] + first * tm
    nb = _batch_rows(B, 4)
    row = lambda w: pl.BlockSpec((nb, tm, w), lambda b, i: (b, i, 0))
    tab = pl.BlockSpec((tm, LANE), lambda b, i: (i, 0))
    outs = [jax.ShapeDtypeStruct((B, S, 512), BF16)] * 6 + [jax.ShapeDtypeStruct((B, S, C_COLS), F32)]
    return pl.pallas_call(
        _per_batch_row(_inproj_kernel, [True, True, True] + [False] * 10 + [True] * 7),
        out_shape=outs,
        grid=(B // nb, S // tm),
        in_specs=_stream_specs(nb, tm, D, first) + [
                  pl.BlockSpec((nb, 1, 1, modsel.shape[-1]), lambda b, i: (b, jnp.minimum(i, 1), 0, 0)),
                  _full((1, D)), _full(w_in.shape), tab, tab, tab, tab,
                  _full(gq.shape), _full(gkv.shape), _full(wq.shape), _full(wkv.shape)],
        out_specs=[row(512)] * 6 + [row(C_COLS)],
        compiler_params=_params(("parallel", "parallel")),
        name="in_proj",
    )(Xc, Xl, modsel, g, w_in, ropeA[0], ropeA[1], ropeB[0], ropeB[1], gq, gkv, wq, wkv)


def _attn_kernel(diff, lam_init, n_ctx, n_lat, n_q, *refs):
    q_refs, (k_ref, v_ref), rest = refs[:n_q], refs[n_q:n_q + 2], refs[n_q + 2:]
    if diff:
        lq1, lk1, lq2, lk2, g_ref, o_ref, sc_ref, sl_ref = rest
    else:
        o_ref, sc_ref, sl_ref = rest
    q = jnp.concatenate([r[0] for r in q_refs], axis=0) if n_q > 1 else q_refs[0][0]
    tq = q.shape[0]
    lane = _lane_iota((tq, LANE))
    if diff:
        zero = jnp.zeros_like(q)
        maps = [(jnp.where(lane < A_DIM, q, zero), slice(0, LANE)), (jnp.where(lane >= A_DIM, q, zero), slice(0, LANE))]
    else:
        maps = [(q[:, h * LANE:(h + 1) * LANE], slice(h * LANE, (h + 1) * LANE)) for h in range(q.shape[1] // LANE)]
    blocks = [(0, n_ctx, lambda mi: sc_ref.at[mi])]
    blocks += [(n_ctx + j * LAT_BLOCK, LAT_BLOCK, lambda mi, j=j: sl_ref.at[mi * n_lat + j]) for j in range(n_lat)]

    ms = []
    for mi, (qm, ksl) in enumerate(maps):
        mrun = jnp.full((tq, LANE), -jnp.inf, F32)
        for off, size, buf in blocks:
            s = _dot(qm, k_ref[0, off:off + size, ksl], 1, 1)
            buf(mi)[...] = s
            for c in range(size // LANE):
                mrun = jnp.maximum(mrun, s[:, c * LANE:(c + 1) * LANE])
        ms.append(jnp.max(mrun, axis=-1, keepdims=True))

    outs = []
    for mi, (qm, ksl) in enumerate(maps):
        acc = None
        for off, size, buf in blocks:
            v = v_ref[0, off:off + size, ksl]
            if diff:
                v = jnp.concatenate([v, jnp.ones_like(v)], axis=1)
            pv = _dot(_bf(jnp.exp2(buf(mi)[...] - ms[mi])), v)
            acc = pv if acc is None else acc + pv
        if diff:
            outs.append(acc[:, :LANE] / acc[:, LANE:])
        else:
            outs.append(jnp.where(lane < B_VDIM, acc / pltpu.roll(acc, B_VDIM, axis=1), 0.0))

    if diff:
        lam = (jnp.exp(jnp.sum(lq1[...] * lk1[...], axis=-1, keepdims=True))
               - jnp.exp(jnp.sum(lq2[...] * lk2[...], axis=-1, keepdims=True)) + lam_init)
        o = outs[0] - lam * outs[1]
        o_ref[0] = _bf(_rms(o, SUBLN_EPS) * g_ref[...] * (1.0 - lam_init))
    else:
        for h, o in enumerate(outs):
            o_ref[0, :, h * LANE:(h + 1) * LANE] = _bf(o)


def _attention(q, k, v, n_ctx, latent, diff_args=None):
    B, S, _ = k.shape
    tq = math.gcd(Q_TILE, S - n_ctx) if latent else ROW_TILE
    tb = ROW_TILE
    n_q = tq // tb
    t0 = n_ctx // tb if latent else 0
    nq = (S - n_ctx) // tq if latent else n_ctx // tq
    n_keys = S if latent else n_ctx
    diff = diff_args is not None
    bw = LANE if diff else 2 * LANE
    assert (n_keys - n_ctx) % LAT_BLOCK == 0
    n_lat = (n_keys - n_ctx) // LAT_BLOCK
    if diff:
        lq1, lk1, lq2, lk2, subln_g, lam_init = diff_args
        extra = [lq1, lk1, lq2, lk2, subln_g]
        vec = _full((1, A_DIM))
        extra_specs = [vec, vec, vec, vec, _full((1, LANE))]
    else:
        lam_init, extra, extra_specs = None, [], []
    kern = functools.partial(_attn_kernel, diff, lam_init, n_ctx, n_lat, n_q)
    qspecs = [pl.BlockSpec((1, tb, bw), lambda b, h, i, j=j: (b, t0 + i * n_q + j, h)) for j in range(n_q)]
    kvspec = pl.BlockSpec((1, n_keys, bw), lambda b, h, i: (b, 0, h))
    return pl.pallas_call(
        kern,
        out_shape=jax.ShapeDtypeStruct((B, nq * tq, q.shape[-1]), BF16),
        grid=(B, q.shape[-1] // bw, nq),
        in_specs=qspecs + [kvspec, kvspec] + extra_specs,
        out_specs=pl.BlockSpec((1, tq, bw), lambda b, h, i: (b, i, h)),
        scratch_shapes=[pltpu.VMEM((2, tq, n_ctx), F32),
                        pltpu.VMEM((max(2 * n_lat, 1), tq, LAT_BLOCK), F32)],
        compiler_params=_params(("parallel", "parallel", "parallel")),
        name=("diff_attn" if diff else "mla_attn") + ("" if latent else "_ctx"),
    )(*([q] * n_q), k, v, *extra)


def _rwkv_chunk_index(rev, n_ctx_chunks, n_chunks, j):
    if not rev:
        return j
    return jnp.where(j < n_ctx_chunks, n_ctx_chunks - 1 - j, n_chunks + n_ctx_chunks - 1 - j)


def _rwkv_local_kernel(n_tiles,
                       cur_ref, prv_ref, nxt_ref, mup_ref, mun_ref, w0_ref, w2_ref, a0_ref, a2_ref, g2_ref,
                       kk_ref, ka_ref, rk_ref, bd_ref,
                       qf_ref, y0f_ref, mf_ref, nf_ref, qb_ref, y0b_ref, mb_ref, nb_ref, bonus_ref, gate_ref):
    C, W = CHUNK, C_WIDTH
    NB, R = cur_ref.shape[0], cur_ref.shape[1]
    i = pl.program_id(1)
    seq_first = i <= 1
    seq_last = (i == 0) | (i == n_tiles - 1)

    rr = _row_iota((R, R))
    cr = _lane_iota((R, R))
    same_chunk = (rr >> 6) == (cr >> 6)
    rows = _row_iota((R, C_COLS))
    rc = _row_iota((C, W))
    cc = _lane_iota((C, W)) & (C - 1)
    eye = jnp.where(rc == cc, 1.0, 0.0)
    pair = [((rc >> (lvl + 1)) == (cc >> (lvl + 1))) & ((rc >> lvl) != (cc >> lvl))
            for lvl in range(int(math.log2(C)))]
    same_head = (_row_iota((W, W)) >> 6) == (_lane_iota((W, W)) >> 6)
    bd = bd_ref[...]
    outs = ((qf_ref, y0f_ref, mf_ref, nf_ref), (qb_ref, y0b_ref, mb_ref, nb_ref))

    chunks = []
    for bi in range(NB):
        x = cur_ref[bi]
        prev_row = jnp.where(seq_first, 0.0, prv_ref[bi, SUBLANE - 1:SUBLANE, :])
        next_row = jnp.where(seq_last, 0.0, nxt_ref[bi, 0:1, :])
        xp = jnp.where(rows == 0, prev_row, pltpu.roll(x, 1, axis=0))
        xn = jnp.where(rows == R - 1, next_row, pltpu.roll(x, R - 1, axis=0))
        xs = x + mup_ref[...] * (xp - x) + mun_ref[...] * (xn - x)

        r, k, v = xs[:, 0:W], xs[:, W:2 * W], xs[:, 2 * W:3 * W]
        wl = jnp.tanh(xs[:, 3 * W:3 * W + 2 * C_LORA])
        al = xs[:, 3 * W + 2 * C_LORA:3 * W + 4 * C_LORA]
        gl = xs[:, 3 * W + 4 * C_LORA:]
        kkr = k * kk_ref[...]
        kk = kkr / jnp.maximum(jnp.sqrt(_mm_hl(kkr * kkr, bd)), 1e-12)
        gate_ref[bi] = _mm(_sigmoid(gl), g2_ref[...])

        bonus = jnp.zeros((R, W), F32)
        for d, rev in enumerate((False, True)):
            w = -_softplus(-(w0_ref[d:d + 1] + _mm(wl, w2_ref[d]))) - 0.5
            lw = -jnp.exp(w)
            a_ic = _sigmoid(a0_ref[d:d + 1] + _mm(al, a2_ref[d]))
            kd = k * (1.0 + (a_ic - 1.0) * ka_ref[...])
            avec = -kk
            bvec = kk * a_ic
            bonus = bonus + _mm_hl(r * kd * rk_ref[...], bd) * v

            tri = jnp.where(same_chunk & ((cr >= rr) if rev else (cr <= rr)), 1.0, 0.0).astype(BF16)
            l_hi = _bf(lw)
            l_md = _bf(lw - l_hi.astype(F32))
            l_lo = _bf(lw - l_hi.astype(F32) - l_md.astype(F32))
            cs = _dot(tri, l_hi) + (_dot(tri, l_md) + _dot(tri, l_lo))
            e_neg = jnp.exp(-cs)
            At_all = avec * jnp.exp(cs - lw)
            Rt_all = r * jnp.exp(cs)
            Bt_all = bvec * e_neg
            Kt_all = kd * e_neg
            for g in range(R // C):
                sl = slice(g * C, (g + 1) * C)
                csg = cs[sl]
                tot = csg[0:1, :] if rev else csg[C - 1:C, :]
                e_rem = jnp.exp(tot - csg)
                chunks.append(dict(bi=bi, d=d, g=g, rev=rev, sl=sl, At=At_all[sl], Rt=Rt_all[sl], v=v[sl], tot=tot,
                                   Bt=Bt_all[sl], Kt=Kt_all[sl], Bg=bvec[sl] * e_rem, Kg=kd[sl] * e_rem))
        bonus_ref[bi] = bonus

    def heads_bd(x):
        xb = _bf(x)
        return jnp.where(same_head, jnp.concatenate([xb] * C_HEADS, axis=0), jnp.zeros((W, W), BF16))

    def each(fn, *lists):
        return [fn(*args) for args in zip(*lists)]

    X = [jnp.concatenate([ch["At"], ch["Rt"]], axis=0) for ch in chunks]
    GB = each(lambda x, ch: _dot(_bf(x), heads_bd(ch["Bt"]), 1, 1), X, chunks)
    GK = each(lambda x, ch: _dot(_bf(x), heads_bd(ch["Kt"]), 1, 1), X, chunks)
    strict = {False: cc < rc, True: cc > rc}
    incl = {False: cc <= rc, True: cc >= rc}
    Aab = each(lambda g_, ch: jnp.where(strict[ch["rev"]], g_[0:C], 0.0), GB, chunks)
    Arb = each(lambda g_, ch: jnp.where(incl[ch["rev"]], g_[C:], 0.0), GB, chunks)
    Aak = each(lambda g_, ch: jnp.where(strict[ch["rev"]], g_[0:C], 0.0), GK, chunks)
    Ark = each(lambda g_, ch: jnp.where(incl[ch["rev"]], g_[C:], 0.0), GK, chunks)
    T = [eye + jnp.where(pair[0], a, 0.0) for a in Aab]
    for lvl in range(1, len(pair)):
        P = each(lambda t, a: _dot(_bf(t), heads_bd(jnp.where(pair[lvl], a, 0.0))), T, Aab)
        T = each(lambda t, p: t + _dot(_bf(p), heads_bd(t)), T, P)
    AV = each(lambda aak, ark, ch: _dot(_bf(jnp.concatenate([aak, ark], axis=0)), heads_bd(ch["v"])), Aak, Ark, chunks)
    Wt = each(lambda t, ch: _dot(_bf(t), heads_bd(ch["At"])), T, chunks)
    U0 = each(lambda t, av: _dot(_bf(t), heads_bd(av[0:C])), T, AV)
    Q = each(lambda a, wt, ch: ch["Rt"] + _dot(_bf(a), heads_bd(wt)), Arb, Wt, chunks)
    Y0 = each(lambda arb, u0, av: _dot(_bf(arb), heads_bd(u0)) + av[C:], Arb, U0, AV)

    for ch, wt, u0, q, y0 in zip(chunks, Wt, U0, Q, Y0):
        q_ref, y0_ref, m_ref, n_ref = outs[ch["d"]]
        q_ref[ch["bi"], ch["sl"], :] = _bf(q)
        y0_ref[ch["bi"], ch["sl"], :] = y0
        Mbd = jnp.where(same_head, _mm_tn(wt, ch["Bg"]), 0.0)
        Nbd = jnp.where(same_head, _mm_tn(u0, ch["Bg"]) + _mm_tn(ch["v"], ch["Kg"]), 0.0)
        Mc = Mbd[0:C] + Mbd[C:2 * C] + Mbd[2 * C:3 * C] + Mbd[3 * C:]
        m_ref[ch["bi"], ch["g"]] = _bf(Mc + jnp.where(rc == cc, jnp.exp(ch["tot"]), 0.0))
        n_ref[ch["bi"], ch["g"]] = Nbd[0:C] + Nbd[C:2 * C] + Nbd[2 * C:3 * C] + Nbd[3 * C:]


def _rwkv_local(pr, mu_prev, mu_next, w0, w2p, a0, a2p, g2, k_k, k_a, r_k, bd):
    B, S, _ = pr.shape
    R, C, W = ROW_TILE, CHUNK, C_WIDTH
    NB = 2 if B % 2 == 0 else 1
    nt, gpt = S // R, R // C
    bpt = R // SUBLANE
    nblk = S // SUBLANE
    rowspec = lambda w: pl.BlockSpec((NB, R, w), lambda b, i: (b, i, 0))
    mspec = pl.BlockSpec((NB, gpt, C, W), lambda b, i: (b, i, 0, 0))
    seq = lambda dt: jax.ShapeDtypeStruct((B, S, W), dt)
    mat = lambda dt: jax.ShapeDtypeStruct((B, S // C, C, W), dt)
    per_dir = [seq(BF16), seq(F32), mat(BF16), mat(F32)]
    vec = _full((1, W))
    return pl.pallas_call(
        functools.partial(_rwkv_local_kernel, nt),
        out_shape=per_dir * 2 + [seq(F32), seq(F32)],
        grid=(B // NB, nt),
        in_specs=[rowspec(C_COLS),
                  pl.BlockSpec((NB, SUBLANE, C_COLS), lambda b, i: (b, jnp.maximum(i * bpt - 1, 0), 0)),
                  pl.BlockSpec((NB, SUBLANE, C_COLS), lambda b, i: (b, jnp.minimum((i + 1) * bpt, nblk - 1), 0)),
                  _full((1, C_COLS)), _full((1, C_COLS)), _full(w0.shape), _full(w2p.shape), _full(a0.shape),
                  _full(a2p.shape), _full(g2.shape), vec, vec, vec, _full(bd.shape)],
        out_specs=[rowspec(W), rowspec(W), mspec, mspec] * 2 + [rowspec(W), rowspec(W)],
        compiler_params=_params(("parallel", "parallel")),
        name="rwkv_local",
    )(pr, pr, pr, mu_prev, mu_next, w0, w2p, a0, a2p, g2, k_k, k_a, r_k, bd)


def _rwkv_state_kernel(qf_ref, y0f_ref, mf_ref, nf_ref, qb_ref, y0b_ref, mb_ref, nb_ref, yf_ref, yb_ref, s_ref):
    C, W = CHUNK, C_WIDTH
    nb, G = mf_ref.shape[0], mf_ref.shape[1]

    @pl.when(pl.program_id(0) == 0)
    def _():
        s_ref[...] = jnp.zeros(s_ref.shape, F32)

    head = _lane_iota((C, W)) >> 6

    def expand(mc):
        return jnp.concatenate([jnp.where(head == h, mc, jnp.zeros_like(mc)) for h in range(C_HEADS)], axis=0)

    dirs = ((qf_ref, y0f_ref, mf_ref, nf_ref, yf_ref, range(G)),
            (qb_ref, y0b_ref, mb_ref, nb_ref, yb_ref, range(G - 1, -1, -1)))
    for step in range(G):
        for d, (q_ref, y0_ref, m_ref, n_ref, y_ref, order) in enumerate(dirs):
            g = order[step]
            sl = slice(g * C, (g + 1) * C)
            for b in range(nb):
                S = _bf(s_ref[d, b])
                y_ref[b, sl, :] = _dot(q_ref[b, sl, :], S, 1, 1) + y0_ref[b, sl, :]
                s_ref[d, b] = _dot(S, expand(m_ref[b, g])) + expand(n_ref[b, g])


def _rwkv_state(loc, n_ctx):
    qf = loc[0]
    B, S, W = qf.shape
    C = CHUNK
    G = 4
    nblk, ncb = S // (G * C), n_ctx // (G * C)
    fwd = lambda j: j
    bwd = functools.partial(_rwkv_chunk_index, True, ncb, nblk)
    seqspec = lambda idx: pl.BlockSpec((B, G * C, W), lambda j: (0, idx(j), 0))
    matspec = lambda idx: pl.BlockSpec((B, G, C, W), lambda j: (0, idx(j), 0, 0))
    out = jax.ShapeDtypeStruct((B, S, W), F32)
    return pl.pallas_call(
        _rwkv_state_kernel,
        out_shape=[out, out],
        grid=(nblk,),
        in_specs=[seqspec(fwd), seqspec(fwd), matspec(fwd), matspec(fwd),
                  seqspec(bwd), seqspec(bwd), matspec(bwd), matspec(bwd)],
        out_specs=[seqspec(fwd), seqspec(bwd)],
        scratch_shapes=[pltpu.VMEM((2, B, W, W), F32)],
        compiler_params=_params(("arbitrary",)),
        name="rwkv_state",
    )(*loc[:8])


def _outproj_kernel(t_start, xc_ref, xl_ref, oac_ref, oal_ref, obc_ref, obl_ref,
                    yf_ref, yb_ref, bonus_ref, gate_ref, gng_ref, gnb_ref, bdm_ref,
                    mod_ref, w_ref, gpost_ref, gpre_ref, x1_ref, h2_ref):
    D = xc_ref.shape[-1]
    rows = range(xc_ref.shape[0])
    is_ctx = pl.program_id(1) + t_start == 0
    bdm = bdm_ref[...]

    ys = [yf_ref[r] + yb_ref[r] for r in rows]
    devs = [y - _mm_hl(y, bdm) for y in ys]
    variances = [_mm_hl(dev * dev, bdm) for dev in devs]
    ocs = [_bf((dev * lax.rsqrt(var + C_GN_EPS) * gng_ref[...] + gnb_ref[...] + bonus_ref[r]) * gate_ref[r])
           for r, dev, var in zip(rows, devs, variances)]

    for r in rows:
        x = jnp.where(is_ctx, xc_ref[r], xl_ref[r])
        oa = jnp.where(is_ctx, oac_ref[r], oal_ref[r])
        ob = jnp.where(is_ctx, obc_ref[r], obl_ref[r])
        mod = mod_ref[r, 0]
        gt1 = mod[:, 2 * D:3 * D]
        sh2, sc2 = mod[:, 3 * D:4 * D], mod[:, 4 * D:5 * D]
        nb = ob.shape[-1]
        o = (_dot(oa, w_ref[0:A_WIDTH, :])
             + _dot(ob, w_ref[A_WIDTH:A_WIDTH + nb, :])
             + _dot(ocs[r], w_ref[A_WIDTH + nb:, :]))
        x1 = x + gt1 * (_rms(o, NORM_EPS) * gpost_ref[...])
        x1_ref[r] = x1
        h2_ref[r] = _bf(_rms(x1, NORM_EPS) * gpre_ref[...] * (1.0 + sc2) + sh2)


def _outproj(Xc, Xl, oa, ob, rwkv, gn_g, gn_b, bdm, modsel, w_out, g_post, g_pre, t_start):
    B, _, D = Xc.shape
    tm = ROW_TILE
    first = 0 if Xl is Xc else Xc.shape[1] // tm
    S = rwkv[0].shape[1]
    nt = S // tm - t_start
    nb = _batch_rows(B, 4)
    row = lambda w: pl.BlockSpec((nb, tm, w), lambda b, i: (b, i, 0))
    full_row = lambda w: pl.BlockSpec((nb, tm, w), lambda b, i: (b, i + t_start, 0))
    pair = lambda w: _stream_specs(nb, tm, w, 1, t_start)
    W = C_WIDTH
    return pl.pallas_call(
        functools.partial(_outproj_kernel, t_start),
        out_shape=[jax.ShapeDtypeStruct((B, nt * tm, D), F32), jax.ShapeDtypeStruct((B, nt * tm, D), BF16)],
        grid=(B // nb, nt),
        in_specs=_stream_specs(nb, tm, D, first, t_start) + pair(A_WIDTH) + pair(ob[1].shape[-1])
                 + [full_row(W)] * 4
                 + [_full((1, W)), _full((1, W)), _full((W, W)),
                    pl.BlockSpec((nb, 1, 1, modsel.shape[-1]), lambda b, i: (b, jnp.minimum(i + t_start, 1), 0, 0)),
                    _full(w_out.shape), _full((1, D)), _full((1, D))],
        out_specs=[row(D), row(D)],
        compiler_params=_params(("parallel", "parallel")),
        name="out_proj",
    )(Xc, Xl, *oa, *ob, *rwkv, gn_g, gn_b, bdm, modsel, w_out, g_post, g_pre)


def _ffn_kernel(n_ctx_tiles, n_tiles, t_start, ff_tile,
                x_ref, h_ref, hp_ref, hn_ref, mod_ref, wup_ref, cw_ref, cb_ref, wdn_ref, gpost_ref, o_ref,
                act_ref):
    D = x_ref.shape[-1]
    tm = x_ref.shape[1]
    dff = wdn_ref.shape[0]
    i = pl.program_id(1) + t_start
    seq_first = (i == 0) | (i == n_ctx_tiles)
    seq_last = (i == n_ctx_tiles - 1) | (i == n_tiles - 1)
    hp = hp_ref[0]
    hn = hn_ref[0]
    hp = jnp.where(seq_first, jnp.zeros_like(hp), hp)
    hn = jnp.where(seq_last, jnp.zeros_like(hn), hn)
    hext = jnp.concatenate([hp, h_ref[0], hn], axis=0)
    halo = hp.shape[0]
    for f in range(dff // ff_tile):
        parts = []
        for base in (0, dff):
            lo = base + f * ff_tile
            u = _dot(hext, wup_ref[:, lo:lo + ff_tile])
            cw = cw_ref[:, lo:lo + ff_tile]
            n = u.shape[0]
            up = pltpu.roll(u, 1, axis=0)[halo:halo + tm]
            un = pltpu.roll(u, n - 1, axis=0)[halo:halo + tm]
            parts.append(up * cw[0:1] + u[halo:halo + tm] * cw[1:2] + un * cw[2:3]
                         + cb_ref[:, lo:lo + ff_tile])
        act_ref[0, :, f * ff_tile:(f + 1) * ff_tile] = _bf(_silu(parts[0]) * parts[1])
    acc = _dot(act_ref[0], wdn_ref[...])
    gt2 = mod_ref[0, 0][:, 5 * D:6 * D]
    o_ref[0] = x_ref[0] + gt2 * (_rms(acc, NORM_EPS) * gpost_ref[...])


def _ffn(X1, H2, modsel, w_up, conv_w, conv_b, w_dn, g_post, n_ctx, t_start):
    B, S, D = X1.shape
    tm = ROW_TILE
    halo = 16
    nt = S // tm
    hb = tm // halo
    nhb = S // halo
    nb = _batch_rows(B, 4)
    kern = _per_batch_row(functools.partial(_ffn_kernel, n_ctx // tm, nt + t_start, t_start, 256),
                          [True] * 5 + [False] * 5 + [True] * 2)
    row = pl.BlockSpec((nb, tm, D), lambda b, i: (b, i, 0))
    once = lambda shape: pl.BlockSpec(shape, lambda b, i: (0, 0), pipeline_mode=pl.Buffered(1))
    return pl.pallas_call(
        kern,
        out_shape=jax.ShapeDtypeStruct((B, S, D), F32),
        grid=(B // nb, nt),
        in_specs=[row, row,
                  pl.BlockSpec((nb, halo, D), lambda b, i: (b, jnp.maximum(i * hb - 1, 0), 0)),
                  pl.BlockSpec((nb, halo, D), lambda b, i: (b, jnp.minimum((i + 1) * hb, nhb - 1), 0)),
                  pl.BlockSpec((nb, 1, 1, modsel.shape[-1]), lambda b, i: (b, jnp.minimum(i + t_start, 1), 0, 0)),
                  once(w_up.shape), _full(conv_w.shape), _full(conv_b.shape), once(w_dn.shape), _full((1, D))],
        out_specs=row,
        scratch_shapes=[pltpu.VMEM((nb, tm, w_dn.shape[0]), BF16)],
        compiler_params=_params(("parallel", "parallel")),
        name="conv_ffn",
    )(X1, H2, H2, H2, modsel, w_up, conv_w, conv_b, w_dn, g_post)


def _rope_tables(S, n_ctx, dim, lane_lo):
    nf = dim // 4
    t = jnp.arange(S - n_ctx, dtype=jnp.int32)
    rows = (t // GRID_W).astype(F32)
    cols = (t % GRID_W).astype(F32)
    inv = ROPE_BASE ** (-jnp.arange(nf, dtype=F32) / nf)
    ar, ac = rows[:, None] * inv, cols[:, None] * inv
    cos = jnp.concatenate([jnp.cos(ar), jnp.cos(ar), jnp.cos(ac), jnp.cos(ac)], axis=-1)
    sin = jnp.concatenate([-jnp.sin(ar), jnp.sin(ar), -jnp.sin(ac), jnp.sin(ac)], axis=-1)
    if lane_lo == 0:
        reps = LANE // dim
        cos, sin = jnp.tile(cos, (1, reps)), jnp.tile(sin, (1, reps))
    else:
        pad = ((0, 0), (lane_lo, LANE - lane_lo - dim))
        cos = jnp.pad(cos, pad, constant_values=1.0)
        sin = jnp.pad(sin, pad)
    cos = jnp.concatenate([jnp.ones((n_ctx, LANE), F32), cos], axis=0)
    sin = jnp.concatenate([jnp.zeros((n_ctx, LANE), F32), sin], axis=0)
    return cos, sin


def _layout_w_in(w):
    D = w.shape[0]
    a = w[:, :3 * A_WIDTH]
    o = 3 * A_WIDTH
    cq = w[:, o:o + B_Q_RANK]
    ckv = w[:, o + B_Q_RANK:o + B_Q_RANK + B_KV_RANK]
    kr = w[:, o + B_Q_RANK + B_KV_RANK:o + B_Q_RANK + B_KV_RANK + B_ROPE]
    c = w[:, o + B_Q_RANK + B_KV_RANK + B_ROPE:]
    z = jnp.zeros((D, 2 * LANE - B_Q_RANK - B_ROPE), w.dtype)
    return _bf(jnp.concatenate([a, cq, kr, z, ckv, c], axis=1))


def _layout_wq(w):
    hd = B_NOPE + B_ROPE
    w = w.reshape(B_Q_RANK, B_HEADS, hd)
    w = jnp.pad(w, ((0, 2 * LANE - B_Q_RANK), (0, 0), (0, LANE - hd)))
    return _bf(w.reshape(2 * LANE, B_HEADS * LANE))


def _layout_wkv(w):
    w = w.reshape(B_KV_RANK, B_HEADS, B_NOPE + B_VDIM)
    pad = lambda t: jnp.pad(t, ((0, 0), (0, 0), (0, LANE - t.shape[-1]))).reshape(B_KV_RANK, B_HEADS * LANE)
    return _bf(jnp.concatenate([pad(w[:, :, :B_NOPE]), pad(w[:, :, B_NOPE:])], axis=1))


def _layout_w_out(w):
    D = w.shape[1]
    wb = w[A_WIDTH:A_WIDTH + B_WIDTH].reshape(B_HEADS, B_VDIM, D)
    wb = jnp.pad(wb, ((0, 0), (0, LANE - B_VDIM), (0, 0))).reshape(B_HEADS * LANE, D)
    return _bf(jnp.concatenate([w[:A_WIDTH], wb, w[A_WIDTH + B_WIDTH:]], axis=0))


def _layout_lora(w, d):
    z = jnp.zeros_like(w)
    return jnp.concatenate([z, w] if d else [w, z], axis=0)


def _block_diag_ones(scale):
    i = np.arange(C_WIDTH) // C_DIM
    return jnp.asarray((i[:, None] == i[None, :]).astype(np.float32) * scale, dtype=BF16)


def kernel(x, c, ctx, c_ctx, ada_w, ada_b, mix_pre_g, mix_post_g, ffn_pre_g, ffn_post_g, w_in, w_out, lam_q1, lam_k1, lam_q2, lam_k2, a_subln_g, b_q_norm_g, b_w_q_up, b_kv_norm_g, b_w_kv_up, c_mu_prev, c_mu_next, c_w0, c_w2, c_a0, c_a2, c_g2, c_k_k, c_k_a, c_r_k, c_gn_g, c_gn_b, ffn_w_up, ffn_conv_w, ffn_conv_b, ffn_w_down):
    B, T, D = x.shape
    n_ctx = ctx.shape[1]
    S = n_ctx + T
    L = ada_w.shape[0]
    assert n_ctx == ROW_TILE and T % ROW_TILE == 0 and D % LANE == 0

    Xc, Xl = ctx, x
    cond = jnp.concatenate([c, c_ctx[None, :], jnp.zeros((SUBLANE - B - 1, D), F32)], axis=0)
    mod = _modulation(cond, ada_w, ada_b)
    ropeA = _rope_tables(S, n_ctx, A_DIM, 0)
    ropeB = _rope_tables(S, n_ctx, B_ROPE, B_NOPE)
    bd1 = _block_diag_ones(1.0)
    bdm = _block_diag_ones(1.0 / C_DIM)
    row = lambda v: v.reshape(1, -1)

    for i in range(L):
        last = i == L - 1
        t0 = 1 if last else 0
        lam_init = 0.8 - 0.6 * math.exp(-0.3 * i)
        modsel = jnp.stack([jnp.broadcast_to(mod[i, B], (B, 6 * D)), mod[i, :B]], axis=1)[:, :, None, :]
        gq = jnp.pad(b_q_norm_g[i], (0, 2 * LANE - B_Q_RANK)).reshape(1, -1)
        qa, ka, va, qb, kb, vb, pr = _inproj(
            Xc, Xl, modsel, row(mix_pre_g[i]), _layout_w_in(w_in[i]), ropeA, ropeB,
            gq, row(b_kv_norm_g[i]), _layout_wq(b_w_q_up[i]), _layout_wkv(b_w_kv_up[i]))
        dargs = (row(lam_q1[i]), row(lam_k1[i]), row(lam_q2[i]), row(lam_k2[i]), row(a_subln_g[i]), lam_init)
        oa = _attention(qa, ka, va, n_ctx, True, dargs)
        ob = _attention(qb, kb, vb, n_ctx, True)
        oa = (oa if last else _attention(qa, ka, va, n_ctx, False, dargs), oa)
        ob = (ob if last else _attention(qb, kb, vb, n_ctx, False), ob)
        loc = _rwkv_local(pr, row(c_mu_prev[i]), row(c_mu_next[i]),
                          c_w0[i], jnp.stack([_layout_lora(c_w2[i, d], d) for d in (0, 1)]),
                          c_a0[i], jnp.stack([_layout_lora(c_a2[i, d], d) for d in (0, 1)]),
                          c_g2[i], row(c_k_k[i]), row(c_k_a[i]), row(c_r_k[i]), bd1)
        yf, yb = _rwkv_state(loc, n_ctx)
        X1, H2 = _outproj(Xc, Xl, oa, ob, (yf, yb, loc[8], loc[9]), row(c_gn_g[i]), row(c_gn_b[i]), bdm, modsel,
                          _layout_w_out(w_out[i]), row(mix_post_g[i]), row(ffn_pre_g[i]), t0)
        Xc = Xl = _ffn(X1, H2, modsel, _bf(ffn_w_up[i]), ffn_conv_w[i], row(ffn_conv_b[i]), _bf(ffn_w_down[i]),
                       row(ffn_post_g[i]), n_ctx, t0)
    return Xl
```

```python
import functools
import math

import jax
import jax.numpy as jnp
import numpy as np
from jax import lax
from jax.experimental import pallas as pl
from jax.experimental.pallas import tpu as pltpu

F32 = jnp.float32
BF16 = jnp.bfloat16

GRID_W = 64
ROPE_BASE = 10000.0
NORM_EPS = 1e-6
SUBLN_EPS = 1e-5
A_HEADS, A_DIM = 4, 64
A_WIDTH = A_HEADS * 2 * A_DIM
B_HEADS, B_NOPE, B_ROPE, B_VDIM = 4, 64, 32, 64
B_Q_RANK, B_KV_RANK = 192, 128
B_WIDTH = B_HEADS * B_VDIM
C_HEADS, C_DIM = 4, 64
C_WIDTH = C_HEADS * C_DIM
C_LORA = 64
C_GATE_LORA = 128
C_GN_EPS = 64e-5
C_COLS = 3 * C_WIDTH + 4 * C_LORA + C_GATE_LORA

LANE = 128
SUBLANE = 8
ROW_TILE = 256
Q_TILE = 1024
LAT_BLOCK = 512
LOG2E = 1.4426950408889634
CHUNK = 64
VMEM_LIMIT = 56 * 1024 * 1024

P_AQ, P_AK, P_AV = 0, 512, 1024
P_BQ = 1536
P_BKV = 1792
P_C = 1920
P_TOTAL = 3072


def _bf(x):
    return x.astype(BF16)


def _dot(a, b, ca=1, cb=0):
    return lax.dot_general(a, b, (((ca,), (cb,)), ((), ())), preferred_element_type=F32)


def _mm(a, b):
    return _dot(_bf(a), _bf(b))


def _mm_nt(a, b):
    return _dot(_bf(a), _bf(b), 1, 1)


def _mm_tn(a, b):
    return _dot(_bf(a.T), _bf(b))


def _split2(x):
    hi = _bf(x)
    lo = _bf(x - hi.astype(F32))
    return hi, lo


def _mm_hl(a, b):
    hi, lo = _split2(a)
    return _dot(hi, b) + _dot(lo, b)


def _mm3(a, b):
    ah, al = _split2(a)
    bh, bl = _split2(b)
    return _dot(ah, bh) + (_dot(ah, bl) + _dot(al, bh))


def _rms(x, eps):
    return x * lax.rsqrt(jnp.mean(x * x, axis=-1, keepdims=True) + eps)


def _sigmoid(x):
    return 1.0 / (1.0 + jnp.exp(-x))


def _silu(x):
    return x * _sigmoid(x)


def _softplus(x):
    return jnp.maximum(x, 0.0) + jnp.log(1.0 + jnp.exp(-jnp.abs(x)))


def _lane_iota(shape):
    return lax.broadcasted_iota(jnp.int32, shape, len(shape) - 1)


def _row_iota(shape):
    return lax.broadcasted_iota(jnp.int32, shape, len(shape) - 2)


def _rope(x, cos, sin, half):
    n = x.shape[-1]
    up = pltpu.roll(x, n - half, axis=1)
    dn = pltpu.roll(x, half, axis=1)
    first = (_lane_iota(x.shape) & half) == 0
    return x * cos + jnp.where(first, up, dn) * sin


def _params(sem):
    return pltpu.CompilerParams(dimension_semantics=sem, vmem_limit_bytes=VMEM_LIMIT)


def _full(shape):
    nd = len(shape)
    return pl.BlockSpec(shape, lambda *_: (0,) * nd)


def _batch_rows(B, most=2):
    return max(n for n in (1, 2, 4) if n <= most and B % n == 0)


def _per_batch_row(tile_fn, is_row):
    def kern(*refs):
        n = next(r for r, m in zip(refs, is_row) if m).shape[0]
        for bi in range(n):
            tile_fn(*[r.at[pl.ds(bi, 1)] if m else r for r, m in zip(refs, is_row)])
    return kern


def _mod_kernel(c_ref, w_ref, b_ref, o_ref):
    act = _silu(c_ref[...])
    o_ref[0] = _mm3(act, w_ref[0]) + b_ref[0]


def _modulation(cond, ada_w, ada_b):
    L, D, N = ada_w.shape
    R = cond.shape[0]
    tn = 1024
    return pl.pallas_call(
        _mod_kernel,
        out_shape=jax.ShapeDtypeStruct((L, R, N), F32),
        grid=(L, N // tn),
        in_specs=[pl.BlockSpec((R, D), lambda l, j: (0, 0)),
                  pl.BlockSpec((1, D, tn), lambda l, j: (l, 0, j)),
                  pl.BlockSpec((1, 1, tn), lambda l, j: (l, 0, j))],
        out_specs=pl.BlockSpec((1, R, tn), lambda l, j: (l, 0, j)),
        compiler_params=_params(("parallel", "parallel")),
        name="adaln_mod",
    )(cond, ada_w, ada_b.reshape(L, 1, N))


def _inproj_kernel(xc_ref, xl_ref, mod_ref, g_ref, w_ref, ca_ref, sa_ref, cb_ref, sb_ref,
                   gq_ref, gkv_ref, wq_ref, wkv_ref,
                   qa_ref, ka_ref, va_ref, qb_ref, kb_ref, vb_ref, pr_ref):
    D = xc_ref.shape[-1]
    x = jnp.where(pl.program_id(1) == 0, xc_ref[0], xl_ref[0])
    mod = mod_ref[0, 0]
    sh1, sc1 = mod[:, 0:D], mod[:, D:2 * D]
    h = _bf(_rms(x, NORM_EPS) * g_ref[...] * (1.0 + sc1) + sh1)

    def proj(lo, hi):
        return _dot(h, w_ref[:, lo:hi])

    ca, sa = ca_ref[...], sa_ref[...]
    cb, sb = cb_ref[...], sb_ref[...]

    pbq = proj(P_BQ, P_BKV)
    ckv = proj(P_BKV, P_C)

    pq = proj(P_AQ, P_AK)
    pk = proj(P_AK, P_AV)
    for j in range(A_WIDTH // LANE):
        sl = slice(j * LANE, (j + 1) * LANE)
        qa_ref[0, :, sl] = _bf(_rope(pq[:, sl], ca, sa, A_DIM // 4) * (A_DIM ** -0.5 * LOG2E))
        ka_ref[0, :, sl] = _bf(_rope(pk[:, sl], ca, sa, A_DIM // 4))
    va_ref[0] = _bf(proj(P_AV, P_BQ))

    pr_ref[0] = proj(P_C, P_TOTAL)

    lane = _lane_iota(pbq.shape)
    cq = jnp.where(lane < B_Q_RANK, pbq, 0.0)
    cqn = cq * lax.rsqrt(jnp.sum(cq * cq, axis=-1, keepdims=True) * (1.0 / B_Q_RANK) + NORM_EPS) * gq_ref[...]
    qb = _mm(cqn, wq_ref[...]) * ((B_NOPE + B_ROPE) ** -0.5 * LOG2E)
    ckvn = _rms(ckv, NORM_EPS) * gkv_ref[...]
    kv = _mm(ckvn, wkv_ref[...])
    krb = pbq[:, LANE:2 * LANE]
    l1 = _lane_iota(krb.shape)
    kr = _rope(jnp.where((l1 >= B_NOPE) & (l1 < B_NOPE + B_ROPE), krb, 0.0), cb, sb, B_ROPE // 4)
    for j in range(B_HEADS):
        sl = slice(j * LANE, (j + 1) * LANE)
        qb_ref[0, :, sl] = _bf(_rope(qb[:, sl], cb, sb, B_ROPE // 4))
        kb_ref[0, :, sl] = _bf(kv[:, sl] + kr)
        vh = kv[:, B_HEADS * LANE + j * LANE:B_HEADS * LANE + (j + 1) * LANE]
        vb_ref[0, :, sl] = _bf(jnp.where(l1 < B_VDIM, vh, 1.0))


def _stream_specs(nb, tm, D, first, t_start=0):
    return [pl.BlockSpec((nb, tm, D), lambda b, i: (b, 0, 0)),
            pl.BlockSpec((nb, tm, D), lambda b, i: (b, jnp.maximum(i + t_start - first, 0), 0))]


def _inproj(Xc, Xl, modsel, g, w_in, ropeA, ropeB, gq, gkv, wq, wkv):
    B, _, D = Xc.shape
    tm = ROW_TILE
    first = 0 if Xl is Xc else Xc.shape[1] // tm
    S = Xl.shape[1] + first * tm
    nb = _batch_rows(B, 4)
    row = lambda w: pl.BlockSpec((nb, tm, w), lambda b, i: (b, i, 0))
    tab = pl.BlockSpec((tm, LANE), lambda b, i: (i, 0))
    outs = [jax.ShapeDtypeStruct((B, S, 512), BF16)] * 6 + [jax.ShapeDtypeStruct((B, S, C_COLS), F32)]
    return pl.pallas_call(
        _per_batch_row(_inproj_kernel, [True, True, True] + [False] * 10 + [True] * 7),
        out_shape=outs,
        grid=(B // nb, S // tm),
        in_specs=_stream_specs(nb, tm, D, first) + [
                  pl.BlockSpec((nb, 1, 1, modsel.shape[-1]), lambda b, i: (b, jnp.minimum(i, 1), 0, 0)),
                  _full((1, D)), _full(w_in.shape), tab, tab, tab, tab,
                  _full(gq.shape), _full(gkv.shape), _full(wq.shape), _full(wkv.shape)],
        out_specs=[row(512)] * 6 + [row(C_COLS)],
        compiler_params=_params(("parallel", "parallel")),
        name="in_proj",
    )(Xc, Xl, modsel, g, w_in, ropeA[0], ropeA[1], ropeB[0], ropeB[1], gq, gkv, wq, wkv)


def _attn_kernel(diff, lam_init, n_ctx, n_lat, n_q, *refs):
    q_refs, (k_ref, v_ref), rest = refs[:n_q], refs[n_q:n_q + 2], refs[n_q + 2:]
    if diff:
        lq1, lk1, lq2, lk2, g_ref, o_ref, sc_ref, sl_ref = rest
    else:
        o_ref, sc_ref, sl_ref = rest
    q = jnp.concatenate([r[0] for r in q_refs], axis=0) if n_q > 1 else q_refs[0][0]
    tq = q.shape[0]
    lane = _lane_iota((tq, LANE))
    if diff:
        zero = jnp.zeros_like(q)
        maps = [(jnp.where(lane < A_DIM, q, zero), slice(0, LANE)), (jnp.where(lane >= A_DIM, q, zero), slice(0, LANE))]
    else:
        maps = [(q[:, h * LANE:(h + 1) * LANE], slice(h * LANE, (h + 1) * LANE)) for h in range(q.shape[1] // LANE)]
    blocks = [(0, n_ctx, lambda mi: sc_ref.at[mi])]
    blocks += [(n_ctx + j * LAT_BLOCK, LAT_BLOCK, lambda mi, j=j: sl_ref.at[mi * n_lat + j]) for j in range(n_lat)]

    ms = []
    for mi, (qm, ksl) in enumerate(maps):
        mrun = jnp.full((tq, LANE), -jnp.inf, F32)
        for off, size, buf in blocks:
            s = _dot(qm, k_ref[0, off:off + size, ksl], 1, 1)
            buf(mi)[...] = s
            for c in range(size // LANE):
                mrun = jnp.maximum(mrun, s[:, c * LANE:(c + 1) * LANE])
        ms.append(jnp.max(mrun, axis=-1, keepdims=True))

    outs = []
    for mi, (qm, ksl) in enumerate(maps):
        acc = None
        for off, size, buf in blocks:
            v = v_ref[0, off:off + size, ksl]
            if diff:
                v = jnp.concatenate([v, jnp.ones_like(v)], axis=1)
            pv = _dot(_bf(jnp.exp2(buf(mi)[...] - ms[mi])), v)
            acc = pv if acc is None else acc + pv
        if diff:
            outs.append(acc[:, :LANE] / acc[:, LANE:])
        else:
            outs.append(jnp.where(lane < B_VDIM, acc / pltpu.roll(acc, B_VDIM, axis=1), 0.0))

    if diff:
        lam = (jnp.exp(jnp.sum(lq1[...] * lk1[...], axis=-1, keepdims=True))
               - jnp.exp(jnp.sum(lq2[...] * lk2[...], axis=-1, keepdims=True)) + lam_init)
        o = outs[0] - lam * outs[1]
        o_ref[0] = _bf(_rms(o, SUBLN_EPS) * g_ref[...] * (1.0 - lam_init))
    else:
        for h, o in enumerate(outs):
            o_ref[0, :, h * LANE:(h + 1) * LANE] = _bf(o)


def _attention(q, k, v, n_ctx, latent, diff_args=None):
    B, S, _ = k.shape
    tq = math.gcd(Q_TILE, S - n_ctx) if latent else ROW_TILE
    tb = ROW_TILE
    n_q = tq // tb
    t0 = n_ctx // tb if latent else 0
    nq = (S - n_ctx) // tq if latent else n_ctx // tq
    n_keys = S if latent else n_ctx
    diff = diff_args is not None
    bw = LANE if diff else 2 * LANE
    assert (n_keys - n_ctx) % LAT_BLOCK == 0
    n_lat = (n_keys - n_ctx) // LAT_BLOCK
    if diff:
        lq1, lk1, lq2, lk2, subln_g, lam_init = diff_args
        extra = [lq1, lk1, lq2, lk2, subln_g]
        vec = _full((1, A_DIM))
        extra_specs = [vec, vec, vec, vec, _full((1, LANE))]
    else:
        lam_init, extra, extra_specs = None, [], []
    kern = functools.partial(_attn_kernel, diff, lam_init, n_ctx, n_lat, n_q)
    qspecs = [pl.BlockSpec((1, tb, bw), lambda b, h, i, j=j: (b, t0 + i * n_q + j, h)) for j in range(n_q)]
    kvspec = pl.BlockSpec((1, n_keys, bw), lambda b, h, i: (b, 0, h))
    return pl.pallas_call(
        kern,
        out_shape=jax.ShapeDtypeStruct((B, nq * tq, q.shape[-1]), BF16),
        grid=(B, q.shape[-1] // bw, nq),
        in_specs=qspecs + [kvspec, kvspec] + extra_specs,
        out_specs=pl.BlockSpec((1, tq, bw), lambda b, h, i: (b, i, h)),
        scratch_shapes=[pltpu.VMEM((2, tq, n_ctx), F32),
                        pltpu.VMEM((max(2 * n_lat, 1), tq, LAT_BLOCK), F32)],
        compiler_params=_params(("parallel", "parallel", "parallel")),
        name=("diff_attn" if diff else "mla_attn") + ("" if latent else "_ctx"),
    )(*([q] * n_q), k, v, *extra)


def _rwkv_chunk_index(rev, n_ctx_chunks, n_chunks, j):
    if not rev:
        return j
    return jnp.where(j < n_ctx_chunks, n_ctx_chunks - 1 - j, n_chunks + n_ctx_chunks - 1 - j)


def _rwkv_local_kernel(n_tiles,
                       cur_ref, prv_ref, nxt_ref, mup_ref, mun_ref, w0_ref, w2_ref, a0_ref, a2_ref, g2_ref,
                       kk_ref, ka_ref, rk_ref, bd_ref,
                       qf_ref, y0f_ref, mf_ref, nf_ref, qb_ref, y0b_ref, mb_ref, nb_ref, bonus_ref, gate_ref):
    C, W = CHUNK, C_WIDTH
    NB, R = cur_ref.shape[0], cur_ref.shape[1]
    i = pl.program_id(1)
    seq_first = i <= 1
    seq_last = (i == 0) | (i == n_tiles - 1)

    rr = _row_iota((R, R))
    cr = _lane_iota((R, R))
    same_chunk = (rr >> 6) == (cr >> 6)
    rows = _row_iota((R, C_COLS))
    rc = _row_iota((C, W))
    cc = _lane_iota((C, W)) & (C - 1)
    eye = jnp.where(rc == cc, 1.0, 0.0)
    pair = [((rc >> (lvl + 1)) == (cc >> (lvl + 1))) & ((rc >> lvl) != (cc >> lvl))
            for lvl in range(int(math.log2(C)))]
    same_head = (_row_iota((W, W)) >> 6) == (_lane_iota((W, W)) >> 6)
    bd = bd_ref[...]
    outs = ((qf_ref, y0f_ref, mf_ref, nf_ref), (qb_ref, y0b_ref, mb_ref, nb_ref))

    chunks = []
    for bi in range(NB):
        x = cur_ref[bi]
        prev_row = jnp.where(seq_first, 0.0, prv_ref[bi, SUBLANE - 1:SUBLANE, :])
        next_row = jnp.where(seq_last, 0.0, nxt_ref[bi, 0:1, :])
        xp = jnp.where(rows == 0, prev_row, pltpu.roll(x, 1, axis=0))
        xn = jnp.where(rows == R - 1, next_row, pltpu.roll(x, R - 1, axis=0))
        xs = x + mup_ref[...] * (xp - x) + mun_ref[...] * (xn - x)

        r, k, v = xs[:, 0:W], xs[:, W:2 * W], xs[:, 2 * W:3 * W]
        wl = jnp.tanh(xs[:, 3 * W:3 * W + 2 * C_LORA])
        al = xs[:, 3 * W + 2 * C_LORA:3 * W + 4 * C_LORA]
        gl = xs[:, 3 * W + 4 * C_LORA:]
        kkr = k * kk_ref[...]
        kk = kkr / jnp.maximum(jnp.sqrt(_mm(kkr * kkr, bd)), 1e-12)
        gate_ref[bi] = _mm(_sigmoid(gl), g2_ref[...])

        bonus = jnp.zeros((R, W), F32)
        for d, rev in enumerate((False, True)):
            w = -_softplus(-(w0_ref[d:d + 1] + _mm(wl, w2_ref[d]))) - 0.5
            lw = -jnp.exp(w)
            a_ic = _sigmoid(a0_ref[d:d + 1] + _mm(al, a2_ref[d]))
            kd = k * (1.0 + (a_ic - 1.0) * ka_ref[...])
            avec = -kk
            bvec = kk * a_ic
            bonus = bonus + _mm(r * kd * rk_ref[...], bd) * v

            tri = jnp.where(same_chunk & ((cr >= rr) if rev else (cr <= rr)), 1.0, 0.0).astype(BF16)
            l_hi = _bf(lw)
            l_md = _bf(lw - l_hi.astype(F32))
            cs = _dot(tri, l_hi) + _dot(tri, l_md)
            e_neg = jnp.exp(-cs)
            At_all = avec * jnp.exp(cs - lw)
            Rt_all = r * jnp.exp(cs)
            Bt_all = bvec * e_neg
            Kt_all = kd * e_neg
            for g in range(R // C):
                sl = slice(g * C, (g + 1) * C)
                csg = cs[sl]
                tot = csg[0:1, :] if rev else csg[C - 1:C, :]
                e_rem = jnp.exp(tot - csg)
                chunks.append(dict(bi=bi, d=d, g=g, rev=rev, sl=sl, At=At_all[sl], Rt=Rt_all[sl], v=v[sl], tot=tot,
                                   Bt=Bt_all[sl], Kt=Kt_all[sl], Bg=bvec[sl] * e_rem, Kg=kd[sl] * e_rem))
        bonus_ref[bi] = bonus

    def heads_bd(x):
        xb = _bf(x)
        return jnp.where(same_head, jnp.concatenate([xb] * C_HEADS, axis=0), jnp.zeros((W, W), BF16))

    def each(fn, *lists):
        return [fn(*args) for args in zip(*lists)]

    X = [jnp.concatenate([ch["At"], ch["Rt"]], axis=0) for ch in chunks]
    GB = each(lambda x, ch: _dot(_bf(x), heads_bd(ch["Bt"]), 1, 1), X, chunks)
    GK = each(lambda x, ch: _dot(_bf(x), heads_bd(ch["Kt"]), 1, 1), X, chunks)
    strict = {False: cc < rc, True: cc > rc}
    incl = {False: cc <= rc, True: cc >= rc}
    Aab = each(lambda g_, ch: jnp.where(strict[ch["rev"]], g_[0:C], 0.0), GB, chunks)
    Arb = each(lambda g_, ch: jnp.where(incl[ch["rev"]], g_[C:], 0.0), GB, chunks)
    Aak = each(lambda g_, ch: jnp.where(strict[ch["rev"]], g_[0:C], 0.0), GK, chunks)
    Ark = each(lambda g_, ch: jnp.where(incl[ch["rev"]], g_[C:], 0.0), GK, chunks)
    T = [eye + jnp.where(pair[0], a, 0.0) for a in Aab]
    for lvl in range(1, len(pair)):
        P = each(lambda t, a: _dot(_bf(t), heads_bd(jnp.where(pair[lvl], a, 0.0))), T, Aab)
        T = each(lambda t, p: t + _dot(_bf(p), heads_bd(t)), T, P)
    AV = each(lambda aak, ark, ch: _dot(_bf(jnp.concatenate([aak, ark], axis=0)), heads_bd(ch["v"])), Aak, Ark, chunks)
    Wt = each(lambda t, ch: _dot(_bf(t), heads_bd(ch["At"])), T, chunks)
    U0 = each(lambda t, av: _dot(_bf(t), heads_bd(av[0:C])), T, AV)
    Q = each(lambda a, wt, ch: ch["Rt"] + _dot(_bf(a), heads_bd(wt)), Arb, Wt, chunks)
    Y0 = each(lambda arb, u0, av: _dot(_bf(arb), heads_bd(u0)) + av[C:], Arb, U0, AV)

    for ch, wt, u0, q, y0 in zip(chunks, Wt, U0, Q, Y0):
        q_ref, y0_ref, m_ref, n_ref = outs[ch["d"]]
        q_ref[ch["bi"], ch["sl"], :] = _bf(q)
        y0_ref[ch["bi"], ch["sl"], :] = y0
        Mbd = jnp.where(same_head, _mm_tn(wt, ch["Bg"]), 0.0)
        Nbd = jnp.where(same_head, _mm_tn(u0, ch["Bg"]) + _mm_tn(ch["v"], ch["Kg"]), 0.0)
        Mc = Mbd[0:C] + Mbd[C:2 * C] + Mbd[2 * C:3 * C] + Mbd[3 * C:]
        m_ref[ch["bi"], ch["g"]] = _bf(Mc + jnp.where(rc == cc, jnp.exp(ch["tot"]), 0.0))
        n_ref[ch["bi"], ch["g"]] = Nbd[0:C] + Nbd[C:2 * C] + Nbd[2 * C:3 * C] + Nbd[3 * C:]


def _rwkv_local(pr, mu_prev, mu_next, w0, w2p, a0, a2p, g2, k_k, k_a, r_k, bd):
    B, S, _ = pr.shape
    R, C, W = ROW_TILE, CHUNK, C_WIDTH
    NB = 2 if B % 2 == 0 else 1
    nt, gpt = S // R, R // C
    bpt = R // SUBLANE
    nblk = S // SUBLANE
    rowspec = lambda w: pl.BlockSpec((NB, R, w), lambda b, i: (b, i, 0))
    mspec = pl.BlockSpec((NB, gpt, C, W), lambda b, i: (b, i, 0, 0))
    seq = lambda dt: jax.ShapeDtypeStruct((B, S, W), dt)
    mat = lambda dt: jax.ShapeDtypeStruct((B, S // C, C, W), dt)
    per_dir = [seq(BF16), seq(F32), mat(BF16), mat(F32)]
    vec = _full((1, W))
    return pl.pallas_call(
        functools.partial(_rwkv_local_kernel, nt),
        out_shape=per_dir * 2 + [seq(F32), seq(F32)],
        grid=(B // NB, nt),
        in_specs=[rowspec(C_COLS),
                  pl.BlockSpec((NB, SUBLANE, C_COLS), lambda b, i: (b, jnp.maximum(i * bpt - 1, 0), 0)),
                  pl.BlockSpec((NB, SUBLANE, C_COLS), lambda b, i: (b, jnp.minimum((i + 1) * bpt, nblk - 1), 0)),
                  _full((1, C_COLS)), _full((1, C_COLS)), _full(w0.shape), _full(w2p.shape), _full(a0.shape),
                  _full(a2p.shape), _full(g2.shape), vec, vec, vec, _full(bd.shape)],
        out_specs=[rowspec(W), rowspec(W), mspec, mspec] * 2 + [rowspec(W), rowspec(W)],
        compiler_params=_params(("parallel", "parallel")),
        name="rwkv_local",
    )(pr, pr, pr, mu_prev, mu_next, w0, w2p, a0, a2p, g2, k_k, k_a, r_k, bd)


def _rwkv_state_kernel(qf_ref, y0f_ref, mf_ref, nf_ref, qb_ref, y0b_ref, mb_ref, nb_ref, yf_ref, yb_ref, s_ref):
    C, W = CHUNK, C_WIDTH
    nb, G = mf_ref.shape[0], mf_ref.shape[1]

    @pl.when(pl.program_id(0) == 0)
    def _():
        s_ref[...] = jnp.zeros(s_ref.shape, F32)

    head = _lane_iota((C, W)) >> 6

    def expand(mc):
        return jnp.concatenate([jnp.where(head == h, mc, jnp.zeros_like(mc)) for h in range(C_HEADS)], axis=0)

    dirs = ((qf_ref, y0f_ref, mf_ref, nf_ref, yf_ref, range(G)),
            (qb_ref, y0b_ref, mb_ref, nb_ref, yb_ref, range(G - 1, -1, -1)))
    for step in range(G):
        for d, (q_ref, y0_ref, m_ref, n_ref, y_ref, order) in enumerate(dirs):
            g = order[step]
            sl = slice(g * C, (g + 1) * C)
            for b in range(nb):
                S = _bf(s_ref[d, b])
                y_ref[b, sl, :] = _dot(q_ref[b, sl, :], S, 1, 1) + y0_ref[b, sl, :]
                s_ref[d, b] = _dot(S, expand(m_ref[b, g])) + expand(n_ref[b, g])


def _rwkv_state(loc, n_ctx):
    qf = loc[0]
    B, S, W = qf.shape
    C = CHUNK
    G = 4
    nblk, ncb = S // (G * C), n_ctx // (G * C)
    fwd = lambda j: j
    bwd = functools.partial(_rwkv_chunk_index, True, ncb, nblk)
    seqspec = lambda idx: pl.BlockSpec((B, G * C, W), lambda j: (0, idx(j), 0))
    matspec = lambda idx: pl.BlockSpec((B, G, C, W), lambda j: (0, idx(j), 0, 0))
    out = jax.ShapeDtypeStruct((B, S, W), F32)
    return pl.pallas_call(
        _rwkv_state_kernel,
        out_shape=[out, out],
        grid=(nblk,),
        in_specs=[seqspec(fwd), seqspec(fwd), matspec(fwd), matspec(fwd),
                  seqspec(bwd), seqspec(bwd), matspec(bwd), matspec(bwd)],
        out_specs=[seqspec(fwd), seqspec(bwd)],
        scratch_shapes=[pltpu.VMEM((2, B, W, W), F32)],
        compiler_params=_params(("arbitrary",)),
        name="rwkv_state",
    )(*loc[:8])


def _outproj_kernel(t_start, xc_ref, xl_ref, oac_ref, oal_ref, obc_ref, obl_ref,
                    yf_ref, yb_ref, bonus_ref, gate_ref, gng_ref, gnb_ref, bdm_ref,
                    mod_ref, w_ref, gpost_ref, gpre_ref, x1_ref, h2_ref):
    D = xc_ref.shape[-1]
    rows = range(xc_ref.shape[0])
    is_ctx = pl.program_id(1) + t_start == 0
    bdm = bdm_ref[...]

    ys = [yf_ref[r] + yb_ref[r] for r in rows]
    devs = [y - _mm_hl(y, bdm) for y in ys]
    variances = [_mm_hl(dev * dev, bdm) for dev in devs]
    ocs = [_bf((dev * lax.rsqrt(var + C_GN_EPS) * gng_ref[...] + gnb_ref[...] + bonus_ref[r]) * gate_ref[r])
           for r, dev, var in zip(rows, devs, variances)]

    for r in rows:
        x = jnp.where(is_ctx, xc_ref[r], xl_ref[r])
        oa = jnp.where(is_ctx, oac_ref[r], oal_ref[r])
        ob = jnp.where(is_ctx, obc_ref[r], obl_ref[r])
        mod = mod_ref[r, 0]
        gt1 = mod[:, 2 * D:3 * D]
        sh2, sc2 = mod[:, 3 * D:4 * D], mod[:, 4 * D:5 * D]
        nb = ob.shape[-1]
        o = (_dot(oa, w_ref[0:A_WIDTH, :])
             + _dot(ob, w_ref[A_WIDTH:A_WIDTH + nb, :])
             + _dot(ocs[r], w_ref[A_WIDTH + nb:, :]))
        x1 = x + gt1 * (_rms(o, NORM_EPS) * gpost_ref[...])
        x1_ref[r] = x1
        h2_ref[r] = _bf(_rms(x1, NORM_EPS) * gpre_ref[...] * (1.0 + sc2) + sh2)


def _outproj(Xc, Xl, oa, ob, rwkv, gn_g, gn_b, bdm, modsel, w_out, g_post, g_pre, t_start):
    B, _, D = Xc.shape
    tm = ROW_TILE
    first = 0 if Xl is Xc else Xc.shape[1] // tm
    S = rwkv[0].shape[1]
    nt = S // tm - t_start
    nb = _batch_rows(B, 4)
    row = lambda w: pl.BlockSpec((nb, tm, w), lambda b, i: (b, i, 0))
    full_row = lambda w: pl.BlockSpec((nb, tm, w), lambda b, i: (b, i + t_start, 0))
    pair = lambda w: _stream_specs(nb, tm, w, 1, t_start)
    W = C_WIDTH
    return pl.pallas_call(
        functools.partial(_outproj_kernel, t_start),
        out_shape=[jax.ShapeDtypeStruct((B, nt * tm, D), F32), jax.ShapeDtypeStruct((B, nt * tm, D), BF16)],
        grid=(B // nb, nt),
        in_specs=_stream_specs(nb, tm, D, first, t_start) + pair(A_WIDTH) + pair(ob[1].shape[-1])
                 + [full_row(W)] * 4
                 + [_full((1, W)), _full((1, W)), _full((W, W)),
                    pl.BlockSpec((nb, 1, 1, modsel.shape[-1]), lambda b, i: (b, jnp.minimum(i + t_start, 1), 0, 0)),
                    _full(w_out.shape), _full((1, D)), _full((1, D))],
        out_specs=[row(D), row(D)],
        compiler_params=_params(("parallel", "parallel")),
        name="out_proj",
    )(Xc, Xl, *oa, *ob, *rwkv, gn_g, gn_b, bdm, modsel, w_out, g_post, g_pre)


def _ffn_kernel(n_ctx_tiles, n_tiles, t_start, ff_tile,
                x_ref, h_ref, hp_ref, hn_ref, mod_ref, wup_ref, cw_ref, cb_ref, wdn_ref, gpost_ref, o_ref,
                act_ref):
    D = x_ref.shape[-1]
    tm = x_ref.shape[1]
    dff = wdn_ref.shape[0]
    i = pl.program_id(1) + t_start
    seq_first = (i == 0) | (i == n_ctx_tiles)
    seq_last = (i == n_ctx_tiles - 1) | (i == n_tiles - 1)
    hp = hp_ref[0]
    hn = hn_ref[0]
    hp = jnp.where(seq_first, jnp.zeros_like(hp), hp)
    hn = jnp.where(seq_last, jnp.zeros_like(hn), hn)
    hext = jnp.concatenate([hp, h_ref[0], hn], axis=0)
    halo = hp.shape[0]
    for f in range(dff // ff_tile):
        parts = []
        for base in (0, dff):
            lo = base + f * ff_tile
            u = _dot(hext, wup_ref[:, lo:lo + ff_tile])
            cw = cw_ref[:, lo:lo + ff_tile]
            n = u.shape[0]
            up = pltpu.roll(u, 1, axis=0)[halo:halo + tm]
            un = pltpu.roll(u, n - 1, axis=0)[halo:halo + tm]
            parts.append(up * cw[0:1] + u[halo:halo + tm] * cw[1:2] + un * cw[2:3]
                         + cb_ref[:, lo:lo + ff_tile])
        act_ref[0, :, f * ff_tile:(f + 1) * ff_tile] = _bf(_silu(parts[0]) * parts[1])
    acc = _dot(act_ref[0], wdn_ref[...])
    gt2 = mod_ref[0, 0][:, 5 * D:6 * D]
    o_ref[0] = x_ref[0] + gt2 * (_rms(acc, NORM_EPS) * gpost_ref[...])


def _ffn(X1, H2, modsel, w_up, conv_w, conv_b, w_dn, g_post, n_ctx, t_start):
    B, S, D = X1.shape
    tm = ROW_TILE
    halo = 16
    nt = S // tm
    hb = tm // halo
    nhb = S // halo
    nb = _batch_rows(B, 4)
    kern = _per_batch_row(functools.partial(_ffn_kernel, n_ctx // tm, nt + t_start, t_start, 256),
                          [True] * 5 + [False] * 5 + [True] * 2)
    row = pl.BlockSpec((nb, tm, D), lambda b, i: (b, i, 0))
    once = lambda shape: pl.BlockSpec(shape, lambda b, i: (0, 0), pipeline_mode=pl.Buffered(1))
    return pl.pallas_call(
        kern,
        out_shape=jax.ShapeDtypeStruct((B, S, D), F32),
        grid=(B // nb, nt),
        in_specs=[row, row,
                  pl.BlockSpec((nb, halo, D), lambda b, i: (b, jnp.maximum(i * hb - 1, 0), 0)),
                  pl.BlockSpec((nb, halo, D), lambda b, i: (b, jnp.minimum((i + 1) * hb, nhb - 1), 0)),
                  pl.BlockSpec((nb, 1, 1, modsel.shape[-1]), lambda b, i: (b, jnp.minimum(i + t_start, 1), 0, 0)),
                  once(w_up.shape), _full(conv_w.shape), _full(conv_b.shape), once(w_dn.shape), _full((1, D))],
        out_specs=row,
        scratch_shapes=[pltpu.VMEM((nb, tm, w_dn.shape[0]), BF16)],
        compiler_params=_params(("parallel", "parallel")),
        name="conv_ffn",
    )(X1, H2, H2, H2, modsel, w_up, conv_w, conv_b, w_dn, g_post)


def _rope_tables(S, n_ctx, dim, lane_lo):
    nf = dim // 4
    t = jnp.arange(S - n_ctx, dtype=jnp.int32)
    rows = (t // GRID_W).astype(F32)
    cols = (t % GRID_W).astype(F32)
    inv = ROPE_BASE ** (-jnp.arange(nf, dtype=F32) / nf)
    ar, ac = rows[:, None] * inv, cols[:, None] * inv
    cos = jnp.concatenate([jnp.cos(ar), jnp.cos(ar), jnp.cos(ac), jnp.cos(ac)], axis=-1)
    sin = jnp.concatenate([-jnp.sin(ar), jnp.sin(ar), -jnp.sin(ac), jnp.sin(ac)], axis=-1)
    if lane_lo == 0:
        reps = LANE // dim
        cos, sin = jnp.tile(cos, (1, reps)), jnp.tile(sin, (1, reps))
    else:
        pad = ((0, 0), (lane_lo, LANE - lane_lo - dim))
        cos = jnp.pad(cos, pad, constant_values=1.0)
        sin = jnp.pad(sin, pad)
    cos = jnp.concatenate([jnp.ones((n_ctx, LANE), F32), cos], axis=0)
    sin = jnp.concatenate([jnp.zeros((n_ctx, LANE), F32), sin], axis=0)
    return cos, sin


def _layout_w_in(w):
    D = w.shape[0]
    a = w[:, :3 * A_WIDTH]
    o = 3 * A_WIDTH
    cq = w[:, o:o + B_Q_RANK]
    ckv = w[:, o + B_Q_RANK:o + B_Q_RANK + B_KV_RANK]
    kr = w[:, o + B_Q_RANK + B_KV_RANK:o + B_Q_RANK + B_KV_RANK + B_ROPE]
    c = w[:, o + B_Q_RANK + B_KV_RANK + B_ROPE:]
    z = jnp.zeros((D, 2 * LANE - B_Q_RANK - B_ROPE), w.dtype)
    return _bf(jnp.concatenate([a, cq, kr, z, ckv, c], axis=1))


def _layout_wq(w):
    hd = B_NOPE + B_ROPE
    w = w.reshape(B_Q_RANK, B_HEADS, hd)
    w = jnp.pad(w, ((0, 2 * LANE - B_Q_RANK), (0, 0), (0, LANE - hd)))
    return _bf(w.reshape(2 * LANE, B_HEADS * LANE))


def _layout_wkv(w):
    w = w.reshape(B_KV_RANK, B_HEADS, B_NOPE + B_VDIM)
    pad = lambda t: jnp.pad(t, ((0, 0), (0, 0), (0, LANE - t.shape[-1]))).reshape(B_KV_RANK, B_HEADS * LANE)
    return _bf(jnp.concatenate([pad(w[:, :, :B_NOPE]), pad(w[:, :, B_NOPE:])], axis=1))


def _layout_w_out(w):
    D = w.shape[1]
    wb = w[A_WIDTH:A_WIDTH + B_WIDTH].reshape(B_HEADS, B_VDIM, D)
    wb = jnp.pad(wb, ((0, 0), (0, LANE - B_VDIM), (0, 0))).reshape(B_HEADS * LANE, D)
    return _bf(jnp.concatenate([w[:A_WIDTH], wb, w[A_WIDTH + B_WIDTH:]], axis=0))


def _layout_lora(w, d):
    z = jnp.zeros_like(w)
    return jnp.concatenate([z, w] if d else [w, z], axis=0)


def _block_diag_ones(scale):
    i = np.arange(C_WIDTH) // C_DIM
    return jnp.asarray((i[:, None] == i[None, :]).astype(np.float32) * scale, dtype=BF16)


def kernel(x, c, ctx, c_ctx, ada_w, ada_b, mix_pre_g, mix_post_g, ffn_pre_g, ffn_post_g, w_in, w_out, lam_q1, lam_k1, lam_q2, lam_k2, a_subln_g, b_q_norm_g, b_w_q_up, b_kv_norm_g, b_w_kv_up, c_mu_prev, c_mu_next, c_w0, c_w2, c_a0, c_a2, c_g2, c_k_k, c_k_a, c_r_k, c_gn_g, c_gn_b, ffn_w_up, ffn_conv_w, ffn_conv_b, ffn_w_down):
    B, T, D = x.shape
    n_ctx = ctx.shape[1]
    S = n_ctx + T
    L = ada_w.shape[0]
    assert n_ctx == ROW_TILE and T % ROW_TILE == 0 and D % LANE == 0

    Xc, Xl = ctx, x
    cond = jnp.concatenate([c, c_ctx[None, :], jnp.zeros((SUBLANE - B - 1, D), F32)], axis=0)
    mod = _modulation(cond, ada_w, ada_b)
    ropeA = _rope_tables(S, n_ctx, A_DIM, 0)
    ropeB = _rope_tables(S, n_ctx, B_ROPE, B_NOPE)
    bd1 = _block_diag_ones(1.0)
    bdm = _block_diag_ones(1.0 / C_DIM)
    row = lambda v: v.reshape(1, -1)

    for i in range(L):
        last = i == L - 1
        t0 = 1 if last else 0
        lam_init = 0.8 - 0.6 * math.exp(-0.3 * i)
        modsel = jnp.stack([jnp.broadcast_to(mod[i, B], (B, 6 * D)), mod[i, :B]], axis=1)[:, :, None, :]
        gq = jnp.pad(b_q_norm_g[i], (0, 2 * LANE - B_Q_RANK)).reshape(1, -1)
        qa, ka, va, qb, kb, vb, pr = _inproj(
            Xc, Xl, modsel, row(mix_pre_g[i]), _layout_w_in(w_in[i]), ropeA, ropeB,
            gq, row(b_kv_norm_g[i]), _layout_wq(b_w_q_up[i]), _layout_wkv(b_w_kv_up[i]))
        dargs = (row(lam_q1[i]), row(lam_k1[i]), row(lam_q2[i]), row(lam_k2[i]), row(a_subln_g[i]), lam_init)
        oa = _attention(qa, ka, va, n_ctx, True, dargs)
        ob = _attention(qb, kb, vb, n_ctx, True)
        oa = (oa if last else _attention(qa, ka, va, n_ctx, False, dargs), oa)
        ob = (ob if last else _attention(qb, kb, vb, n_ctx, False), ob)
        loc = _rwkv_local(pr, row(c_mu_prev[i]), row(c_mu_next[i]),
                          c_w0[i], jnp.stack([_layout_lora(c_w2[i, d], d) for d in (0, 1)]),
                          c_a0[i], jnp.stack([_layout_lora(c_a2[i, d], d) for d in (0, 1)]),
                          c_g2[i], row(c_k_k[i]), row(c_k_a[i]), row(c_r_k[i]), bd1)
        yf, yb = _rwkv_state(loc, n_ctx)
        X1, H2 = _outproj(Xc, Xl, oa, ob, (yf, yb, loc[8], loc[9]), row(c_gn_g[i]), row(c_gn_b[i]), bdm, modsel,
                          _layout_w_out(w_out[i]), row(mix_post_g[i]), row(ffn_pre_g[i]), t0)
        Xc = Xl = _ffn(X1, H2, modsel, _bf(ffn_w_up[i]), ffn_conv_w[i], row(ffn_conv_b[i]), _bf(ffn_w_down[i]),
                       row(ffn_post_g[i]), n_ctx, t0)
    return Xl
```
